```python
import math
import jax, jax.numpy as jnp
from jax import lax
import numpy as np

D_MODEL = 1024
BATCH = 8
SEQ = 4096
DEPTH = 4

CHUNK = 64
N_EVEN = (DEPTH + 1) // 2
N_ODD = DEPTH // 2
EPS = 1e-6

POOL_WIDTH = D_MODEL // 2
POOL_WINDOWS = (2, 4, 8, 16)
POOL_N_GROUPS = len(POOL_WINDOWS)
POOL_GROUP = POOL_WIDTH // POOL_N_GROUPS
ATT_HEAD_DIM = 64
ATT_HEADS = (D_MODEL // 2) // ATT_HEAD_DIM
ATT_WIDTH = ATT_HEADS * ATT_HEAD_DIM
ATT_PREV_CHUNKS = 8
REL_CLIP = 256
MIX_IN = POOL_WIDTH + 3 * ATT_WIDTH
MIX_OUT = POOL_WIDTH + ATT_WIDTH

SSM_EXPAND = 2
SSM_D_INNER = SSM_EXPAND * D_MODEL
SSM_HEAD_DIM = 64
SSM_HEADS = SSM_D_INNER // SSM_HEAD_DIM
SSM_GROUPS = 8
SSM_HEADS_PER_GROUP = SSM_HEADS // SSM_GROUPS
SSM_D_STATE = 128
SSM_D_CONV = 4
SSM_CONV_DIM = SSM_D_INNER + 2 * SSM_GROUPS * SSM_D_STATE
SSM_IN = SSM_D_INNER + SSM_CONV_DIM + SSM_HEADS
DT_MIN = 0.001
DT_MAX = 0.1

MOE_GROUPS = 4
MOE_EXPERTS_PER_GROUP = 8
MOE_EXPERTS = MOE_GROUPS * MOE_EXPERTS_PER_GROUP
MOE_TOP_K = 2
MOE_D_FF = D_MODEL // 8

kernel_name = 'hybrid_pool_chunkattn_ssd_hmoe_adaln'


def rmsnorm(x, w):
    xf = x.astype(jnp.float32)
    y = xf * lax.rsqrt(jnp.mean(xf * xf, axis=-1, keepdims=True) + EPS)
    return (y * w.astype(jnp.float32)).astype(x.dtype)


def modulate(h, shift, scale):
    return h * (1 + scale[:, None, :]) + shift[:, None, :]


def pool_mixer(u, pool_w, pool_scale):
    b, s, _ = u.shape
    uf = u.astype(jnp.float32)
    cs = jnp.pad(jnp.cumsum(uf, axis=1), ((0, 0), (1, 0), (0, 0)))
    pos = jnp.arange(1, s + 1, dtype=jnp.float32)
    outs = []
    for g, w in enumerate(POOL_WINDOWS):
        lo, hi = g * POOL_GROUP, (g + 1) * POOL_GROUP
        upper = cs[:, 1:, lo:hi]
        lower = jnp.pad(cs[:, :s + 1 - w, lo:hi], ((0, 0), (w - 1, 0), (0, 0)))
        count = jnp.minimum(pos, w)[None, :, None]
        outs.append((upper - lower) / count - uf[:, :, lo:hi])
    pooled = jnp.stack(outs, axis=2).astype(u.dtype)
    y = jnp.einsum('bsgc,gcd->bsgd', pooled, pool_w).reshape(b, s, POOL_WIDTH)
    return y * pool_scale


def chunk_attention(q, k, v, rel_bias):
    b, s, _ = q.shape
    nc = s // CHUNK
    band = (ATT_PREV_CHUNKS + 1) * CHUNK
    q = q.reshape(b, nc, CHUNK, ATT_HEADS, ATT_HEAD_DIM)
    k = k.reshape(b, nc, CHUNK, ATT_HEADS, ATT_HEAD_DIM)
    v = v.reshape(b, nc, CHUNK, ATT_HEADS, ATT_HEAD_DIM)
    pad = ((0, 0), (ATT_PREV_CHUNKS, 0), (0, 0), (0, 0), (0, 0))
    kp = jnp.pad(k, pad)
    vp = jnp.pad(v, pad)
    k_band = jnp.concatenate([kp[:, j:j + nc] for j in range(ATT_PREV_CHUNKS + 1)], axis=2)
    v_band = jnp.concatenate([vp[:, j:j + nc] for j in range(ATT_PREV_CHUNKS + 1)], axis=2)
    q_off = jnp.arange(CHUNK)[:, None] + ATT_PREV_CHUNKS * CHUNK
    k_off = jnp.arange(band)[None, :]
    rel_idx = jnp.clip(q_off - k_off, -REL_CLIP, REL_CLIP) + REL_CLIP
    bias = rel_bias[:, rel_idx].astype(jnp.float32)
    key_chunk = jnp.arange(nc)[:, None] - ATT_PREV_CHUNKS + jnp.arange(band)[None, :] // CHUNK
    valid = key_chunk >= 0
    scores = jnp.einsum('bclhd,bcshd->bchls', q, k_band).astype(jnp.float32) * (ATT_HEAD_DIM ** -0.5)
    scores = scores + bias[None, None]
    scores = jnp.where(valid[None, :, None, None, :], scores, -1e30)
    p = jax.nn.softmax(scores, axis=-1).astype(v.dtype)
    o = jnp.einsum('bchls,bcshd->bclhd', p, v_band)
    return o.reshape(b, s, ATT_WIDTH)


def mamba2_mixer(h, w_in, conv_w, conv_b, dt_bias, A_log, D_skip, norm_w, w_out):
    b, s, _ = h.shape
    nc = s // CHUNK
    G, R, P, N = SSM_GROUPS, SSM_HEADS_PER_GROUP, SSM_HEAD_DIM, SSM_D_STATE
    f32 = jnp.float32
    zxbcdt = h @ w_in
    z, xbc, dt = jnp.split(zxbcdt, [SSM_D_INNER, SSM_D_INNER + SSM_CONV_DIM], axis=-1)
    xp = jnp.pad(xbc, ((0, 0), (SSM_D_CONV - 1, 0), (0, 0)))
    conv = xp[:, 0:s] * conv_w[0]
    for tap in range(1, SSM_D_CONV):
        conv = conv + xp[:, tap:tap + s] * conv_w[tap]
    xbc = jax.nn.silu(conv + conv_b)
    xs, Bm, Cm = jnp.split(xbc, [SSM_D_INNER, SSM_D_INNER + G * N], axis=-1)
    dt = jax.nn.softplus(dt.astype(f32) + dt_bias.astype(f32))
    A = -jnp.exp(A_log.astype(f32))
    a = (dt * A).reshape(b, nc, CHUNK, G, R)
    xs = xs.astype(f32).reshape(b, nc, CHUNK, G, R, P)
    x_dt = xs * dt.reshape(b, nc, CHUNK, G, R)[..., None]
    Bm = Bm.astype(f32).reshape(b, nc, CHUNK, G, N)
    Cm = Cm.astype(f32).reshape(b, nc, CHUNK, G, N)
    a_cs = jnp.moveaxis(jnp.cumsum(a, axis=2), 2, -1)
    tril = jnp.tril(jnp.ones((CHUNK, CHUNK), dtype=bool))
    seg = jnp.exp(jnp.where(tril, a_cs[..., :, None] - a_cs[..., None, :], -jnp.inf))
    cb = jnp.einsum('bclgn,bcsgn->bcgls', Cm, Bm)
    y_diag = jnp.einsum('bcgls,bcgrls,bcsgrp->bclgrp', cb, seg, x_dt)
    decay_states = jnp.exp(a_cs[..., -1:] - a_cs)
    states = jnp.einsum('bclgn,bcgrl,bclgrp->bcgrpn', Bm, decay_states, x_dt)
    chunk_decay = jnp.exp(a_cs[..., -1])

    def step(carry, inp):
        st, dec = inp
        return carry * dec[..., None, None] + st, carry

    init = jnp.zeros((b, G, R, P, N), f32)
    _, prev = lax.scan(step, init, (jnp.moveaxis(states, 1, 0), jnp.moveaxis(chunk_decay, 1, 0)))
    prev = jnp.moveaxis(prev, 0, 1)
    y_off = jnp.einsum('bclgn,bcgrpn,bcgrl->bclgrp', Cm, prev, jnp.exp(a_cs))
    y = y_diag + y_off + xs * D_skip.astype(f32).reshape(G, R)[:, :, None]
    y = y.reshape(b, s, SSM_D_INNER)
    g = (y * jax.nn.silu(z.astype(f32))).reshape(b, s, G, SSM_D_INNER // G)
    g = g * lax.rsqrt(jnp.mean(g * g, axis=-1, keepdims=True) + EPS)
    g = g.reshape(b, s, SSM_D_INNER) * norm_w.astype(f32)
    return g.astype(h.dtype) @ w_out


def hier_moe(h, w_group, b_group, w_expert, b_expert, w1, w3, w2):
    b, s, d = h.shape
    t = h.reshape(b * s, d)
    tf = t.astype(jnp.float32)
    g_prob = jax.nn.softmax(tf @ w_group.astype(jnp.float32) + b_group.astype(jnp.float32), axis=-1)
    g_val, g_idx = lax.top_k(g_prob, 1)
    e_logits = jnp.einsum('td,dge->tge', tf, w_expert.astype(jnp.float32)) + b_expert.astype(jnp.float32)
    e_logits = e_logits[jnp.arange(b * s), g_idx[:, 0]]
    e_val, e_idx = lax.top_k(jax.nn.softmax(e_logits, axis=-1), MOE_TOP_K)
    e_val = e_val / jnp.sum(e_val, axis=-1, keepdims=True)
    weights = g_val * e_val
    expert_id = g_idx * MOE_EXPERTS_PER_GROUP + e_idx
    combine = jnp.sum(jax.nn.one_hot(expert_id, MOE_EXPERTS, dtype=jnp.float32) * weights[..., None], axis=1)
    hid = jax.nn.silu(jnp.einsum('td,edf->tef', t, w1)) * jnp.einsum('td,edf->tef', t, w3)
    hid = hid * combine.astype(hid.dtype)[:, :, None]
    y = jnp.einsum('tef,efd->td', hid, w2)
    return y.reshape(b, s, d)


def _normal(key, shape, scale):
    return jax.random.normal(key, shape, jnp.float32) * scale


def setup_inputs(seed: int = 0) -> dict:
    key = jax.random.key(seed)
    ks = jax.random.split(key, 28)
    D = D_MODEL
    f32 = jnp.float32
    dt0 = jnp.exp(jax.random.uniform(ks[14], (N_ODD, SSM_HEADS), f32) * (math.log(DT_MAX) - math.log(DT_MIN)) + math.log(DT_MIN))
    dt0 = jnp.maximum(dt0, 1e-4)
    return {
        'x': _normal(ks[0], (BATCH, SEQ, D), 1.0),
        'c': _normal(ks[1], (BATCH, D), 1.0),
        'ada_w': _normal(ks[2], (DEPTH, D, 6 * D), 0.5 * D ** -0.5),
        'ada_b': _normal(ks[3], (DEPTH, 6 * D), 0.02),
        'norm1_w': 1.0 + _normal(ks[4], (DEPTH, D), 0.05),
        'norm2_w': 1.0 + _normal(ks[5], (DEPTH, D), 0.05),
        'mix_w_in': _normal(ks[6], (N_EVEN, D, MIX_IN), D ** -0.5),
        'pool_w': _normal(ks[7], (N_EVEN, POOL_N_GROUPS, POOL_GROUP, POOL_GROUP), POOL_GROUP ** -0.5),
        'pool_scale': 1.0 + _normal(ks[8], (N_EVEN, POOL_WIDTH), 0.05),
        'rel_bias': _normal(ks[9], (N_EVEN, ATT_HEADS, 2 * REL_CLIP + 1), 0.2),
        'mix_w_out': _normal(ks[10], (N_EVEN, MIX_OUT, D), MIX_OUT ** -0.5),
        'ssm_w_in': _normal(ks[11], (N_ODD, D, SSM_IN), D ** -0.5),
        'ssm_conv_w': _normal(ks[12], (N_ODD, SSM_D_CONV, SSM_CONV_DIM), SSM_D_CONV ** -0.5),
        'ssm_conv_b': _normal(ks[13], (N_ODD, SSM_CONV_DIM), 0.02),
        'ssm_dt_bias': dt0 + jnp.log(-jnp.expm1(-dt0)),
        'ssm_A_log': jnp.log(jax.random.uniform(ks[15], (N_ODD, SSM_HEADS), f32, 1.0, 16.0)),
        'ssm_D': 1.0 + _normal(ks[16], (N_ODD, SSM_HEADS), 0.1),
        'ssm_norm_w': 1.0 + _normal(ks[17], (N_ODD, SSM_D_INNER), 0.05),
        'ssm_w_out': _normal(ks[18], (N_ODD, SSM_D_INNER, D), SSM_D_INNER ** -0.5),
        'moe_w_group': _normal(ks[19], (DEPTH, D, MOE_GROUPS), D ** -0.5),
        'moe_b_group': _normal(ks[20], (DEPTH, MOE_GROUPS), 0.01),
        'moe_w_expert': _normal(ks[21], (DEPTH, D, MOE_GROUPS, MOE_EXPERTS_PER_GROUP), D ** -0.5),
        'moe_b_expert': _normal(ks[22], (DEPTH, MOE_GROUPS, MOE_EXPERTS_PER_GROUP), 0.01),
        'moe_w1': _normal(ks[23], (DEPTH, MOE_EXPERTS, D, MOE_D_FF), D ** -0.5),
        'moe_w3': _normal(ks[24], (DEPTH, MOE_EXPERTS, D, MOE_D_FF), D ** -0.5),
        'moe_w2': _normal(ks[25], (DEPTH, MOE_EXPERTS, MOE_D_FF, D), MOE_D_FF ** -0.5),
        'final_norm_w': 1.0 + _normal(ks[26], (D,), 0.05),
    }


def reference(x, c, ada_w, ada_b, norm1_w, norm2_w, mix_w_in, pool_w, pool_scale, rel_bias, mix_w_out,
              ssm_w_in, ssm_conv_w, ssm_conv_b, ssm_dt_bias, ssm_A_log, ssm_D, ssm_norm_w, ssm_w_out,
              moe_w_group, moe_b_group, moe_w_expert, moe_b_expert, moe_w1, moe_w3, moe_w2, final_norm_w):
    c_act = jax.nn.silu(c)
    for layer in range(DEPTH):
        mod = c_act @ ada_w[layer] + ada_b[layer]
        shift1, scale1, gate1, shift2, scale2, gate2 = jnp.split(mod, 6, axis=-1)
        h = modulate(rmsnorm(x, norm1_w[layer]), shift1, scale1)
        i = layer // 2
        if layer % 2 == 0:
            proj = h @ mix_w_in[i]
            u, q, k, v = jnp.split(proj, [POOL_WIDTH, POOL_WIDTH + ATT_WIDTH, POOL_WIDTH + 2 * ATT_WIDTH], axis=-1)
            a_out = pool_mixer(u, pool_w[i], pool_scale[i])
            b_out = chunk_attention(q, k, v, rel_bias[i])
            mix = jnp.concatenate([a_out, b_out], axis=-1) @ mix_w_out[i]
        else:
            mix = mamba2_mixer(h, ssm_w_in[i], ssm_conv_w[i], ssm_conv_b[i], ssm_dt_bias[i], ssm_A_log[i],
                               ssm_D[i], ssm_norm_w[i], ssm_w_out[i])
        x = x + gate1[:, None, :] * mix
        h2 = modulate(rmsnorm(x, norm2_w[layer]), shift2, scale2)
        ffn = hier_moe(h2, moe_w_group[layer], moe_b_group[layer], moe_w_expert[layer], moe_b_expert[layer],
                       moe_w1[layer], moe_w3[layer], moe_w2[layer])
        x = x + gate2[:, None, :] * ffn
    return rmsnorm(x, final_norm_w)
```

```python
import functools
import math

import jax
import jax.numpy as jnp
from jax import lax
from jax.experimental import pallas as pl
from jax.experimental.pallas import tpu as pltpu

F32 = jnp.float32
BF16 = jnp.bfloat16

EPS = 1e-6
LANES = 128

POOL_WINDOWS = (2, 4, 8, 16)
POOL_GROUP = 128
POOL_WIDTH = POOL_GROUP * len(POOL_WINDOWS)
POOL_HALO = 16
ATT_HEAD_DIM = 64
ATT_HEADS = 8
ATT_WIDTH = ATT_HEADS * ATT_HEAD_DIM
ATT_CHUNK = 64
ATT_PREV_CHUNKS = 8
ATT_TQ = 256
ATT_PREV = ATT_PREV_CHUNKS * ATT_CHUNK
ATT_TK = ATT_PREV + ATT_TQ
REL_CLIP = 256
NEG_BIG = -1e30

SSM_GROUPS = 8
SSM_HEADS = 32
SSM_HEADS_PER_GROUP = 4
SSM_HEAD_DIM = 64
SSM_D_STATE = 128
SSM_D_INNER = SSM_HEADS * SSM_HEAD_DIM
SSM_GROUP_WIDTH = SSM_HEADS_PER_GROUP * SSM_HEAD_DIM
SSM_D_CONV = 4
SSM_CONV_HALO = 8
SSD_L = 256

MOE_GROUPS = 4
MOE_EPG = 8
MOE_EXPERTS = 32
MOE_D_FF = 128

VMEM_LIMIT = 56 * 1024 * 1024


def _cparams(sem):
    return pltpu.CompilerParams(dimension_semantics=sem, vmem_limit_bytes=VMEM_LIMIT)


def _resident(shape):
    nd = len(shape)
    return pl.BlockSpec(shape, lambda *_: (0,) * nd, pipeline_mode=pl.Buffered(1))


def _silu(v):
    return v * jax.nn.sigmoid(v)


def _rms_mod(xv, nw, shift, scale):
    ms = jnp.mean(xv * xv, axis=-1, keepdims=True)
    y = xv * lax.rsqrt(ms + EPS) * nw
    return y * (1.0 + scale) + shift


def _adaln_kernel(c_ref, w_ref, b_ref, o_ref):
    c = c_ref[...]
    ca = _silu(c).astype(BF16)
    o_ref[0] = jnp.dot(ca, w_ref[0].astype(BF16), preferred_element_type=F32) + b_ref[0]


def adaln(c, ada_w, ada_b):
    depth, d, n = ada_w.shape
    b = c.shape[0]
    tn = 1536
    return pl.pallas_call(
        _adaln_kernel,
        grid=(depth, n // tn),
        in_specs=[
            pl.BlockSpec((b, d), lambda l, j: (0, 0)),
            pl.BlockSpec((1, d, tn), lambda l, j: (l, 0, j)),
            pl.BlockSpec((1, 1, tn), lambda l, j: (l, 0, j)),
        ],
        out_specs=pl.BlockSpec((1, b, tn), lambda l, j: (l, 0, j)),
        out_shape=jax.ShapeDtypeStruct((depth, b, n), F32),
        compiler_params=_cparams(("arbitrary", "arbitrary")),
        name="adaln",
    )(c, ada_w, ada_b.reshape(depth, 1, n))


def _norm_proj_kernel(x_ref, nw_ref, mod_ref, w_ref, *o_refs, chunks):
    m = mod_ref[0]
    h = _rms_mod(x_ref[0], nw_ref[...], m[0:1], m[1:2]).astype(BF16)
    for col, width, stores in chunks:
        r = jnp.dot(h, w_ref[:, col:col + width], preferred_element_type=F32)
        for off, w, oi, g, dst in stores:
            o_ref = o_refs[oi]
            v = r[:, off:off + w].astype(o_ref.dtype)
            if g is None:
                o_ref[0, :, dst:dst + w] = v
            else:
                o_ref[0, g, :, dst:dst + w] = v


def norm_proj(x, nw, mod, w, out_shapes, chunks, tm):
    b, s, d = x.shape
    out_specs = []
    for sh in out_shapes:
        if len(sh.shape) == 3:
            out_specs.append(pl.BlockSpec((1, tm, sh.shape[2]), lambda i, j: (i, j, 0)))
        else:
            out_specs.append(pl.BlockSpec((1, sh.shape[1], tm, sh.shape[3]), lambda i, j: (i, 0, j, 0)))
    return pl.pallas_call(
        functools.partial(_norm_proj_kernel, chunks=chunks),
        grid=(b, s // tm),
        in_specs=[
            pl.BlockSpec((1, tm, d), lambda i, j: (i, j, 0)),
            _resident((1, d)),
            pl.BlockSpec((1, 6, d), lambda i, j: (i, 0, 0)),
            _resident(w.shape),
        ],
        out_specs=out_specs,
        out_shape=out_shapes,
        compiler_params=_cparams(("parallel", "parallel")),
        name="norm_proj",
    )(x, nw.reshape(1, d), mod, w)


def _even_mixer_kernel(uq_ref, kv2_ref, kv1_ref, kv0_ref, halo_ref, bias_ref, pw_ref, ps_ref, o_ref, ubuf):
    j = pl.program_id(1)
    tq = ATT_TQ

    u = uq_ref[0, :, 0:POOL_WIDTH].astype(F32)
    halo = halo_ref[0].astype(F32)
    ubuf[0:POOL_HALO, :] = jnp.where(j > 0, halo, 0.0)
    ubuf[POOL_HALO:POOL_HALO + tq, :] = u
    pos1 = j * tq + 1 + lax.broadcasted_iota(jnp.int32, (tq, 1), 0)
    for g, win in enumerate(POOL_WINDOWS):
        lo = g * POOL_GROUP
        acc = ubuf[POOL_HALO:POOL_HALO + tq, lo:lo + POOL_GROUP]
        for k in range(1, win):
            acc = acc + ubuf[POOL_HALO - k:POOL_HALO - k + tq, lo:lo + POOL_GROUP]
        cnt = jnp.minimum(pos1, win).astype(F32)
        pooled = acc / cnt - u[:, lo:lo + POOL_GROUP]
        y = jnp.dot(pooled.astype(BF16), pw_ref[g], preferred_element_type=F32)
        o_ref[0, :, lo:lo + POOL_GROUP] = (y * ps_ref[:, lo:lo + POOL_GROUP]).astype(o_ref.dtype)

    kpos = j * tq - ATT_PREV + lax.broadcasted_iota(jnp.int32, (1, ATT_TK), 1)
    kvalid = kpos >= 0
    lane_hi = lax.broadcasted_iota(jnp.int32, (1, LANES), 1) >= ATT_HEAD_DIM
    scale = ATT_HEAD_DIM ** -0.5
    for hp in range(ATT_HEADS // 2):
        c0 = hp * LANES
        q2 = uq_ref[0, :, POOL_WIDTH + c0:POOL_WIDTH + c0 + LANES]
        k2 = jnp.concatenate([r[0, :, c0:c0 + LANES] for r in (kv2_ref, kv1_ref, kv0_ref)], axis=0)
        v2 = jnp.concatenate(
            [r[0, :, ATT_WIDTH + c0:ATT_WIDTH + c0 + LANES] for r in (kv2_ref, kv1_ref, kv0_ref)], axis=0)
        outs = []
        for jj in range(2):
            qm = jnp.where(lane_hi if jj else jnp.logical_not(lane_hi), q2, jnp.zeros_like(q2))
            sc = lax.dot_general(qm, k2, (((1,), (1,)), ((), ())), preferred_element_type=F32)
            sc = sc * scale + bias_ref[2 * hp + jj]
            sc = jnp.where(kvalid, sc, NEG_BIG)
            mx = jnp.max(sc, axis=-1, keepdims=True)
            p = jnp.exp(sc - mx)
            den = jnp.sum(p, axis=-1, keepdims=True)
            o = jnp.dot(p.astype(BF16), v2, preferred_element_type=F32)
            outs.append(o / den)
        o_pair = jnp.where(lane_hi, outs[1], outs[0])
        o_ref[0, :, POOL_WIDTH + c0:POOL_WIDTH + c0 + LANES] = o_pair.astype(o_ref.dtype)


def _attn_bias_tile(rel_bias):
    i = jnp.arange(ATT_TQ)[:, None]
    s = jnp.arange(ATT_TK)[None, :]
    rel = jnp.clip(ATT_PREV + i - s, -REL_CLIP, REL_CLIP) + REL_CLIP
    band = s - (i // ATT_CHUNK) * ATT_CHUNK
    ok = (band >= 0) & (band < (ATT_PREV_CHUNKS + 1) * ATT_CHUNK)
    return jnp.where(ok[None], rel_bias[:, rel].astype(F32), NEG_BIG)


def even_mixer(proj, bias_tile, pool_w, pool_scale):
    b, s, n = proj.shape
    tq = ATT_TQ
    half = n // 2
    halo_blocks = tq // POOL_HALO
    return pl.pallas_call(
        _even_mixer_kernel,
        grid=(b, s // tq),
        in_specs=[
            pl.BlockSpec((1, tq, half), lambda i, j: (i, j, 0)),
            pl.BlockSpec((1, tq, half), lambda i, j: (i, jnp.maximum(j - 2, 0), 1)),
            pl.BlockSpec((1, tq, half), lambda i, j: (i, jnp.maximum(j - 1, 0), 1)),
            pl.BlockSpec((1, tq, half), lambda i, j: (i, j, 1)),
            pl.BlockSpec((1, POOL_HALO, POOL_WIDTH), lambda i, j: (i, jnp.maximum(j * halo_blocks - 1, 0), 0)),
            _resident(bias_tile.shape),
            _resident(pool_w.shape),
            _resident((1, POOL_WIDTH)),
        ],
        out_specs=pl.BlockSpec((1, tq, POOL_WIDTH + ATT_WIDTH), lambda i, j: (i, j, 0)),
        out_shape=jax.ShapeDtypeStruct((b, s, POOL_WIDTH + ATT_WIDTH), BF16),
        scratch_shapes=[pltpu.VMEM((POOL_HALO + tq, POOL_WIDTH), F32)],
        compiler_params=_cparams(("parallel", "parallel")),
        name="even_mixer",
    )(proj, proj, proj, proj, proj, bias_tile, pool_w, pool_scale.reshape(1, POOL_WIDTH))


def _out_proj_kernel(a_ref, w_ref, x_ref, mod_ref, nw_ref, xo_ref, h_ref, *, grouped):
    m = mod_ref[0]
    if grouped:
        ng, _, kw = a_ref.shape[1], a_ref.shape[2], a_ref.shape[3]
        mix = jnp.dot(a_ref[0, 0], w_ref[0:kw, :], preferred_element_type=F32)
        for g in range(1, ng):
            mix = mix + jnp.dot(a_ref[0, g], w_ref[g * kw:(g + 1) * kw, :], preferred_element_type=F32)
    else:
        mix = jnp.dot(a_ref[0], w_ref[...], preferred_element_type=F32)
    xn = x_ref[0] + m[2:3] * mix
    xo_ref[0] = xn
    h_ref[0] = _rms_mod(xn, nw_ref[...], m[3:4], m[4:5]).astype(h_ref.dtype)


def out_proj(a, w, x, mod, nw2, tm):
    b, s, d = x.shape
    grouped = a.ndim == 4
    if grouped:
        a_spec = pl.BlockSpec((1, a.shape[1], tm, a.shape[3]), lambda i, j: (i, 0, j, 0))
    else:
        a_spec = pl.BlockSpec((1, tm, a.shape[2]), lambda i, j: (i, j, 0))
    return pl.pallas_call(
        functools.partial(_out_proj_kernel, grouped=grouped),
        grid=(b, s // tm),
        in_specs=[
            a_spec,
            _resident(w.shape),
            pl.BlockSpec((1, tm, d), lambda i, j: (i, j, 0)),
            pl.BlockSpec((1, 6, d), lambda i, j: (i, 0, 0)),
            _resident((1, d)),
        ],
        out_specs=[
            pl.BlockSpec((1, tm, d), lambda i, j: (i, j, 0)),
            pl.BlockSpec((1, tm, d), lambda i, j: (i, j, 0)),
        ],
        out_shape=[jax.ShapeDtypeStruct((b, s, d), F32), jax.ShapeDtypeStruct((b, s, d), BF16)],
        compiler_params=_cparams(("parallel", "parallel")),
        name="out_proj",
    )(a, w, x, mod, nw2.reshape(1, d))


def _ssd_kernel(z_ref, x_ref, b_ref, c_ref, dt_ref, cwx_ref, cwb_ref, cwc_ref, cbx_ref, cbb_ref, cbc_ref,
                dtb_ref, alog_ref, dskip_ref, nw_ref, o_ref,
                state, hx, hb, hc, ext, acsb, acst, dtt, wt):
    j = pl.program_id(1)
    L = SSD_L
    H = SSM_CONV_HALO

    @pl.when(j == 0)
    def _():
        state[...] = jnp.zeros_like(state)
        hx[...] = jnp.zeros_like(hx)
        hb[...] = jnp.zeros_like(hb)
        hc[...] = jnp.zeros_like(hc)

    dt = jax.nn.softplus(dt_ref[0] + dtb_ref[...])
    a = dt * (-jnp.exp(alog_ref[...]))
    row = lax.broadcasted_iota(jnp.int32, (L, L), 0)
    col = lax.broadcasted_iota(jnp.int32, (L, L), 1)
    tril = row >= col
    a_cs = jnp.dot(tril.astype(F32), a, preferred_element_type=F32, precision=lax.Precision.HIGHEST)
    a_last = a_cs[L - 1:L, :]
    acst[...] = a_cs.T
    dtt[...] = dt.T
    wt[...] = (dt * jnp.exp(a_last - a_cs)).T
    for h in range(SSM_HEADS):
        acsb[h] = jnp.broadcast_to(a_cs[:, h:h + 1], (L, LANES))

    lane_hi = lax.broadcasted_iota(jnp.int32, (1, LANES), 1) >= SSM_HEAD_DIM

    def conv_silu(src_ref, halo_ref, cw_ref, cb_ref, g, width):
        cur = src_ref[0, g].astype(F32)
        ext[0:H, 0:width] = halo_ref[g]
        ext[H:H + L, 0:width] = cur
        halo_ref[g] = cur[L - H:L, :]
        w = cw_ref[g]
        acc = cb_ref[g]
        for tap in range(SSM_D_CONV):
            o = H - (SSM_D_CONV - 1) + tap
            acc = acc + ext[o:o + L, 0:width] * w[tap:tap + 1, :]
        return _silu(acc)

    def group_body(g, carry):
        xs = conv_silu(x_ref, hx, cwx_ref, cbx_ref, g, SSM_GROUP_WIDTH)
        bm = conv_silu(b_ref, hb, cwb_ref, cbb_ref, g, SSM_D_STATE)
        cm = conv_silu(c_ref, hc, cwc_ref, cbc_ref, g, SSM_D_STATE)
        bmb = bm.astype(BF16)
        cmb = cm.astype(BF16)
        cb = lax.dot_general(cmb, bmb, (((1,), (1,)), ((), ())), preferred_element_type=F32)
        bmt = bm.T
        st = state[g]
        yoff = jnp.dot(cmb, st.astype(BF16), preferred_element_type=F32)
        zz = z_ref[0, g].astype(F32)
        dsk = dskip_ref[g]
        gated = []
        for q in range(2):
            c0 = q * LANES
            xs2 = xs[:, c0:c0 + LANES]
            xs2b = xs2.astype(BF16)
            ys, sns, cds = [], [], []
            for jj in range(2):
                h = g * SSM_HEADS_PER_GROUP + 2 * q + jj
                alb = acsb[h]
                asr = acst[pl.ds(h, 1), :]
                dtr = dtt[pl.ds(h, 1), :]
                wr = wt[pl.ds(h, 1), :]
                al2 = jnp.concatenate([alb] * (L // LANES), axis=1)
                seg = jnp.exp(jnp.where(tril, al2 - asr, -jnp.inf))
                mr = (cb * seg * dtr).astype(BF16)
                yd = jnp.dot(mr, xs2b, preferred_element_type=F32)
                ys.append(yd + jnp.exp(alb) * yoff[:, c0:c0 + LANES])
                lhs = (bmt * wr).astype(BF16)
                sns.append(jnp.dot(lhs, xs2b, preferred_element_type=F32))
                cds.append(jnp.exp(alb[L - 1:L, :]))
            y = jnp.where(lane_hi, ys[1], ys[0]) + xs2 * dsk[:, c0:c0 + LANES]
            sn = jnp.where(lane_hi, sns[1], sns[0])
            cd = jnp.where(lane_hi, cds[1], cds[0])
            state[g, :, c0:c0 + LANES] = st[:, c0:c0 + LANES] * cd + sn
            gated.append(y * _silu(zz[:, c0:c0 + LANES]))
        ss = jnp.sum(gated[0] * gated[0], axis=-1, keepdims=True) + jnp.sum(gated[1] * gated[1], axis=-1, keepdims=True)
        rs = lax.rsqrt(ss / SSM_GROUP_WIDTH + EPS)
        nw = nw_ref[g]
        for q in range(2):
            c0 = q * LANES
            o_ref[0, g, :, c0:c0 + LANES] = (gated[q] * rs * nw[:, c0:c0 + LANES]).astype(o_ref.dtype)
        return carry

    lax.fori_loop(0, SSM_GROUPS, group_body, 0)


def ssd(z, xs, bm, cm, dt, conv_w, conv_b, dt_bias, a_log, d_skip, norm_w):
    b, ng, s, gw = z.shape
    L = SSD_L
    n = SSM_D_STATE
    di = SSM_D_INNER
    pad = LANES - SSM_HEADS
    cwx = conv_w[:, :di].reshape(SSM_D_CONV, ng, gw).transpose(1, 0, 2)
    cwb = conv_w[:, di:di + ng * n].reshape(SSM_D_CONV, ng, n).transpose(1, 0, 2)
    cwc = conv_w[:, di + ng * n:].reshape(SSM_D_CONV, ng, n).transpose(1, 0, 2)
    cbx = conv_b[:di].reshape(ng, 1, gw)
    cbb = conv_b[di:di + ng * n].reshape(ng, 1, n)
    cbc = conv_b[di + ng * n:].reshape(ng, 1, n)
    dtb = jnp.pad(dt_bias, (0, pad)).reshape(1, LANES)
    alog = jnp.pad(a_log, (0, pad)).reshape(1, LANES)
    dsk = jnp.repeat(d_skip, SSM_HEAD_DIM).reshape(ng, 1, gw)
    nw = norm_w.reshape(ng, 1, gw)
    tile4 = lambda w: pl.BlockSpec((1, ng, L, w), lambda i, j: (i, 0, j, 0))
    return pl.pallas_call(
        _ssd_kernel,
        grid=(b, s // L),
        in_specs=[
            tile4(gw), tile4(gw), tile4(n), tile4(n),
            pl.BlockSpec((1, L, LANES), lambda i, j: (i, j, 0)),
            _resident(cwx.shape), _resident(cwb.shape), _resident(cwc.shape),
            _resident(cbx.shape), _resident(cbb.shape), _resident(cbc.shape),
            _resident(dtb.shape), _resident(alog.shape), _resident(dsk.shape), _resident(nw.shape),
        ],
        out_specs=tile4(gw),
        out_shape=jax.ShapeDtypeStruct((b, ng, s, gw), BF16),
        scratch_shapes=[
            pltpu.VMEM((ng, n, gw), F32),
            pltpu.VMEM((ng, SSM_CONV_HALO, gw), F32),
            pltpu.VMEM((ng, SSM_CONV_HALO, n), F32),
            pltpu.VMEM((ng, SSM_CONV_HALO, n), F32),
            pltpu.VMEM((SSM_CONV_HALO + L, gw), F32),
            pltpu.VMEM((SSM_HEADS, L, LANES), F32),
            pltpu.VMEM((LANES, L), F32),
            pltpu.VMEM((LANES, L), F32),
            pltpu.VMEM((LANES, L), F32),
        ],
        compiler_params=_cparams(("parallel", "arbitrary")),
        name="ssd",
    )(z, xs, bm, cm, dt, cwx, cwb, cwc, cbx, cbb, cbc, dtb, alog, dsk, nw)


def _route(logits):
    lane = lax.broadcasted_iota(jnp.int32, logits.shape, 1)
    is_g = (lane >= MOE_EXPERTS) & (lane < MOE_EXPERTS + MOE_GROUPS)
    gl = jnp.where(is_g, logits, -jnp.inf)
    gmax = jnp.max(gl, axis=-1, keepdims=True)
    gsum = jnp.sum(jnp.exp(gl - gmax), axis=-1, keepdims=True)
    g_val = 1.0 / gsum
    g_lane = jnp.min(jnp.where(gl == gmax, lane, 2 * LANES), axis=-1, keepdims=True)
    g_idx = g_lane - MOE_EXPERTS
    in_grp = (lane >= g_idx * MOE_EPG) & (lane < (g_idx + 1) * MOE_EPG)
    el = jnp.where(in_grp, logits, -jnp.inf)
    emax = jnp.max(el, axis=-1, keepdims=True)
    ee = jnp.exp(el - emax)
    p = ee / jnp.sum(ee, axis=-1, keepdims=True)
    p = jnp.where(in_grp, p, -1.0)
    m1 = jnp.max(p, axis=-1, keepdims=True)
    i1 = jnp.min(jnp.where(p == m1, lane, 2 * LANES), axis=-1, keepdims=True)
    p2 = jnp.where(lane == i1, -1.0, p)
    m2 = jnp.max(p2, axis=-1, keepdims=True)
    i2 = jnp.min(jnp.where(p2 == m2, lane, 2 * LANES), axis=-1, keepdims=True)
    tot = m1 + m2
    w1 = g_val * (m1 / tot)
    w2 = g_val * (m2 / tot)
    return jnp.where(lane == i1, w1, 0.0) + jnp.where(lane == i2, w2, 0.0)


def _moe_dense_kernel(h_ref, wr_ref, br_ref, w1_ref, w3_ref, w2_ref, x_ref, mod_ref, o_ref, acc, comb, *, eb):
    e = pl.program_id(1)
    h = h_ref[...]

    @pl.when(e == 0)
    def _():
        logits = jnp.dot(h, wr_ref[...], preferred_element_type=F32) + br_ref[...]
        comb[...] = _route(logits)
        acc[...] = jnp.zeros_like(acc)

    a = jnp.dot(h, w1_ref[...], preferred_element_type=F32)
    b = jnp.dot(h, w3_ref[...], preferred_element_type=F32)
    hid = _silu(a) * b
    cm = comb[...]
    lane = lax.broadcasted_iota(jnp.int32, cm.shape, 1)
    parts = []
    for q in range(eb):
        cw = jnp.sum(jnp.where(lane == e * eb + q, cm, 0.0), axis=-1, keepdims=True)
        parts.append((hid[:, q * MOE_D_FF:(q + 1) * MOE_D_FF] * cw).astype(BF16))
    hb = jnp.concatenate(parts, axis=1)
    acc[...] += jnp.dot(hb, w2_ref[...], preferred_element_type=F32)

    @pl.when(e == pl.num_programs(1) - 1)
    def _():
        m = mod_ref[0]
        o_ref[...] = x_ref[...] + m[5:6] * acc[...]


def moe_dense(h2, wr, br, w1, w3, w2, x, mod, tm, eb=4):
    b, s, d = x.shape
    t = b * s
    tiles_per_batch = s // tm
    ebw = eb * MOE_D_FF
    out = pl.pallas_call(
        functools.partial(_moe_dense_kernel, eb=eb),
        grid=(t // tm, MOE_EXPERTS // eb),
        in_specs=[
            pl.BlockSpec((tm, d), lambda i, e: (i, 0)),
            _resident(wr.shape),
            _resident(br.shape),
            pl.BlockSpec((d, ebw), lambda i, e: (0, e)),
            pl.BlockSpec((d, ebw), lambda i, e: (0, e)),
            pl.BlockSpec((ebw, d), lambda i, e: (e, 0)),
            pl.BlockSpec((tm, d), lambda i, e: (i, 0)),
            pl.BlockSpec((1, 6, d), lambda i, e: (i // tiles_per_batch, 0, 0)),
        ],
        out_specs=pl.BlockSpec((tm, d), lambda i, e: (i, 0)),
        out_shape=jax.ShapeDtypeStruct((t, d), F32),
        scratch_shapes=[pltpu.VMEM((tm, d), F32), pltpu.VMEM((tm, LANES), F32)],
        compiler_params=_cparams(("parallel", "arbitrary")),
        name="moe_dense",
    )(h2.reshape(t, d), wr, br, w1, w3, w2, x.reshape(t, d), mod)
    return out.reshape(b, s, d)


def _final_norm_kernel(x_ref, nw_ref, o_ref):
    xv = x_ref[...]
    ms = jnp.mean(xv * xv, axis=-1, keepdims=True)
    o_ref[...] = xv * lax.rsqrt(ms + EPS) * nw_ref[...]


def final_norm(x, nw, tm):
    b, s, d = x.shape
    t = b * s
    return pl.pallas_call(
        _final_norm_kernel,
        grid=(t // tm,),
        in_specs=[pl.BlockSpec((tm, d), lambda i: (i, 0)), _resident((1, d))],
        out_specs=pl.BlockSpec((tm, d), lambda i: (i, 0)),
        out_shape=jax.ShapeDtypeStruct((t, d), F32),
        compiler_params=_cparams(("parallel",)),
        name="final_norm",
    )(x.reshape(t, d), nw.reshape(1, d)).reshape(b, s, d)


def _even_chunks(n):
    cw = 512
    return tuple((c, cw, ((0, cw, 0, None, c),)) for c in range(0, n, cw))


def _ssm_chunks():
    gw, n, ng = SSM_GROUP_WIDTH, SSM_D_STATE, SSM_GROUPS
    chunks = []
    col = 0
    for oi in (0, 1):
        for g in range(0, ng, 2):
            chunks.append((col, 2 * gw, ((0, gw, oi, g, 0), (gw, gw, oi, g + 1, 0))))
            col += 2 * gw
    for oi in (2, 3):
        for g in range(0, ng, 4):
            chunks.append((col, 4 * n, tuple((k * n, n, oi, g + k, 0) for k in range(4))))
            col += 4 * n
    chunks.append((col, LANES, ((0, LANES, 4, None, 0),)))
    return tuple(chunks), col + LANES


def _router_weights(w_group, b_group, w_expert, b_expert):
    d = w_group.shape[0]
    pad = LANES - MOE_EXPERTS - MOE_GROUPS
    wr = jnp.concatenate([w_expert.reshape(d, MOE_EXPERTS), w_group, jnp.zeros((d, pad), F32)], axis=1)
    br = jnp.concatenate([b_expert.reshape(MOE_EXPERTS), b_group, jnp.zeros((pad,), F32)]).reshape(1, LANES)
    return wr.astype(BF16), br


def kernel(x, c, ada_w, ada_b, norm1_w, norm2_w, mix_w_in, pool_w, pool_scale, rel_bias, mix_w_out, ssm_w_in, ssm_conv_w, ssm_conv_b, ssm_dt_bias, ssm_A_log, ssm_D, ssm_norm_w, ssm_w_out, moe_w_group, moe_b_group, moe_w_expert, moe_b_expert, moe_w1, moe_w3, moe_w2, final_norm_w):
    b, s, d = x.shape
    depth = ada_w.shape[0]
    tm = 512
    mod_all = adaln(c, ada_w, ada_b)
    ssm_chunks, ssm_cols = _ssm_chunks()
    for layer in range(depth):
        mod = mod_all[layer].reshape(b, 6, d)
        i = layer // 2
        if layer % 2 == 0:
            w_in = mix_w_in[i].astype(BF16)
            (proj,) = norm_proj(x, norm1_w[layer], mod, w_in,
                                [jax.ShapeDtypeStruct((b, s, w_in.shape[1]), BF16)], _even_chunks(w_in.shape[1]), tm)
            a = even_mixer(proj, _attn_bias_tile(rel_bias[i]), pool_w[i].astype(BF16), pool_scale[i])
            x, h2 = out_proj(a, mix_w_out[i].astype(BF16), x, mod, norm2_w[layer], tm)
        else:
            w_in = jnp.pad(ssm_w_in[i], ((0, 0), (0, ssm_cols - ssm_w_in.shape[2]))).astype(BF16)
            ng, gw, n = SSM_GROUPS, SSM_GROUP_WIDTH, SSM_D_STATE
            shapes = [
                jax.ShapeDtypeStruct((b, ng, s, gw), BF16),
                jax.ShapeDtypeStruct((b, ng, s, gw), BF16),
                jax.ShapeDtypeStruct((b, ng, s, n), BF16),
                jax.ShapeDtypeStruct((b, ng, s, n), BF16),
                jax.ShapeDtypeStruct((b, s, LANES), F32),
            ]
            z, xs, bm, cm, dt = norm_proj(x, norm1_w[layer], mod, w_in, shapes, ssm_chunks, tm)
            g = ssd(z, xs, bm, cm, dt, ssm_conv_w[i], ssm_conv_b[i], ssm_dt_bias[i], ssm_A_log[i], ssm_D[i],
                    ssm_norm_w[i])
            x, h2 = out_proj(g, ssm_w_out[i].astype(BF16), x, mod, norm2_w[layer], tm)
        wr, br = _router_weights(moe_w_group[layer], moe_b_group[layer], moe_w_expert[layer], moe_b_expert[layer])
        e, _, f = moe_w1[layer].shape
        w1 = moe_w1[layer].transpose(1, 0, 2).reshape(d, e * f).astype(BF16)
        w3 = moe_w3[layer].transpose(1, 0, 2).reshape(d, e * f).astype(BF16)
        w2 = moe_w2[layer].reshape(e * f, d).astype(BF16)
        x = moe_dense(h2, wr, br, w1, w3, w2, x, mod, tm)
    return final_norm(x, final_norm_w, tm)
```

```python
import functools
import math

import jax
import jax.numpy as jnp
from jax import lax
from jax.experimental import pallas as pl
from jax.experimental.pallas import tpu as pltpu

F32 = jnp.float32
BF16 = jnp.bfloat16

EPS = 1e-6
LANES = 128

POOL_WINDOWS = (2, 4, 8, 16)
POOL_GROUP = 128
POOL_WIDTH = POOL_GROUP * len(POOL_WINDOWS)
POOL_HALO = 16
ATT_HEAD_DIM = 64
ATT_HEADS = 8
ATT_WIDTH = ATT_HEADS * ATT_HEAD_DIM
ATT_CHUNK = 64
ATT_PREV_CHUNKS = 8
ATT_TQ = 256
ATT_PREV = ATT_PREV_CHUNKS * ATT_CHUNK
ATT_TK = ATT_PREV + ATT_TQ
REL_CLIP = 256
NEG_BIG = -1e30

SSM_GROUPS = 8
SSM_HEADS = 32
SSM_HEADS_PER_GROUP = 4
SSM_HEAD_DIM = 64
SSM_D_STATE = 128
SSM_D_INNER = SSM_HEADS * SSM_HEAD_DIM
SSM_GROUP_WIDTH = SSM_HEADS_PER_GROUP * SSM_HEAD_DIM
SSM_D_CONV = 4
SSM_CONV_HALO = 8
SSD_L = 256

MOE_GROUPS = 4
MOE_EPG = 8
MOE_EXPERTS = 32
MOE_D_FF = 128

VMEM_LIMIT = 56 * 1024 * 1024


def _cparams(sem):
    return pltpu.CompilerParams(dimension_semantics=sem, vmem_limit_bytes=VMEM_LIMIT)


def _resident(shape):
    nd = len(shape)
    return pl.BlockSpec(shape, lambda *_: (0,) * nd, pipeline_mode=pl.Buffered(1))


def _silu(v):
    return v * jax.nn.sigmoid(v)


def _rms_mod(xv, nw, shift, scale):
    ms = jnp.mean(xv * xv, axis=-1, keepdims=True)
    y = xv * lax.rsqrt(ms + EPS) * nw
    return y * (1.0 + scale) + shift


def _adaln_kernel(c_ref, w_ref, b_ref, o_ref):
    c = c_ref[...]
    ca = _silu(c).astype(BF16)
    o_ref[0] = jnp.dot(ca, w_ref[0].astype(BF16), preferred_element_type=F32) + b_ref[0]


def adaln(c, ada_w, ada_b):
    depth, d, n = ada_w.shape
    b = c.shape[0]
    tn = 1536
    return pl.pallas_call(
        _adaln_kernel,
        grid=(depth, n // tn),
        in_specs=[
            pl.BlockSpec((b, d), lambda l, j: (0, 0)),
            pl.BlockSpec((1, d, tn), lambda l, j: (l, 0, j)),
            pl.BlockSpec((1, 1, tn), lambda l, j: (l, 0, j)),
        ],
        out_specs=pl.BlockSpec((1, b, tn), lambda l, j: (l, 0, j)),
        out_shape=jax.ShapeDtypeStruct((depth, b, n), F32),
        compiler_params=_cparams(("arbitrary", "arbitrary")),
        name="adaln",
    )(c, ada_w, ada_b.reshape(depth, 1, n))


def _norm_proj_kernel(x_ref, nw_ref, mod_ref, w_ref, *o_refs, chunks):
    m = mod_ref[0]
    h = _rms_mod(x_ref[0], nw_ref[...], m[0:1], m[1:2]).astype(BF16)
    for col, width, stores in chunks:
        r = jnp.dot(h, w_ref[:, col:col + width], preferred_element_type=F32)
        for off, w, oi, g, dst in stores:
            o_ref = o_refs[oi]
            v = r[:, off:off + w].astype(o_ref.dtype)
            if g is None:
                o_ref[0, :, dst:dst + w] = v
            else:
                o_ref[0, g, :, dst:dst + w] = v


def norm_proj(x, nw, mod, w, out_shapes, chunks, tm):
    b, s, d = x.shape
    out_specs = []
    for sh in out_shapes:
        if len(sh.shape) == 3:
            out_specs.append(pl.BlockSpec((1, tm, sh.shape[2]), lambda i, j: (i, j, 0)))
        else:
            out_specs.append(pl.BlockSpec((1, sh.shape[1], tm, sh.shape[3]), lambda i, j: (i, 0, j, 0)))
    return pl.pallas_call(
        functools.partial(_norm_proj_kernel, chunks=chunks),
        grid=(b, s // tm),
        in_specs=[
            pl.BlockSpec((1, tm, d), lambda i, j: (i, j, 0)),
            _resident((1, d)),
            pl.BlockSpec((1, 6, d), lambda i, j: (i, 0, 0)),
            _resident(w.shape),
        ],
        out_specs=out_specs,
        out_shape=out_shapes,
        compiler_params=_cparams(("parallel", "parallel")),
        name="norm_proj",
    )(x, nw.reshape(1, d), mod, w)


def _even_mixer_kernel(uq_ref, kv2_ref, kv1_ref, kv0_ref, halo_ref, bias_ref, pw_ref, ps_ref, o_ref, ubuf):
    j = pl.program_id(1)
    tq = ATT_TQ

    u = uq_ref[0, :, 0:POOL_WIDTH].astype(F32)
    halo = halo_ref[0].astype(F32)
    ubuf[0:POOL_HALO, :] = jnp.where(j > 0, halo, 0.0)
    ubuf[POOL_HALO:POOL_HALO + tq, :] = u
    pos1 = j * tq + 1 + lax.broadcasted_iota(jnp.int32, (tq, 1), 0)
    for g, win in enumerate(POOL_WINDOWS):
        lo = g * POOL_GROUP
        acc = ubuf[POOL_HALO:POOL_HALO + tq, lo:lo + POOL_GROUP]
        for k in range(1, win):
            acc = acc + ubuf[POOL_HALO - k:POOL_HALO - k + tq, lo:lo + POOL_GROUP]
        cnt = jnp.minimum(pos1, win).astype(F32)
        pooled = acc / cnt - u[:, lo:lo + POOL_GROUP]
        y = jnp.dot(pooled.astype(BF16), pw_ref[g], preferred_element_type=F32)
        o_ref[0, :, lo:lo + POOL_GROUP] = (y * ps_ref[:, lo:lo + POOL_GROUP]).astype(o_ref.dtype)

    kpos = j * tq - ATT_PREV + lax.broadcasted_iota(jnp.int32, (1, ATT_TK), 1)
    kvalid = kpos >= 0
    lane_hi = lax.broadcasted_iota(jnp.int32, (1, LANES), 1) >= ATT_HEAD_DIM
    scale = ATT_HEAD_DIM ** -0.5
    for hp in range(ATT_HEADS // 2):
        c0 = hp * LANES
        q2 = uq_ref[0, :, POOL_WIDTH + c0:POOL_WIDTH + c0 + LANES]
        k2 = jnp.concatenate([r[0, :, c0:c0 + LANES] for r in (kv2_ref, kv1_ref, kv0_ref)], axis=0)
        v2 = jnp.concatenate(
            [r[0, :, ATT_WIDTH + c0:ATT_WIDTH + c0 + LANES] for r in (kv2_ref, kv1_ref, kv0_ref)], axis=0)
        outs = []
        for jj in range(2):
            qm = jnp.where(lane_hi if jj else jnp.logical_not(lane_hi), q2, jnp.zeros_like(q2))
            sc = lax.dot_general(qm, k2, (((1,), (1,)), ((), ())), preferred_element_type=F32)
            sc = sc * scale + bias_ref[2 * hp + jj]
            sc = jnp.where(kvalid, sc, NEG_BIG)
            mx = jnp.max(sc, axis=-1, keepdims=True)
            p = jnp.exp(sc - mx)
            den = jnp.sum(p, axis=-1, keepdims=True)
            o = jnp.dot(p.astype(BF16), v2, preferred_element_type=F32)
            outs.append(o / den)
        o_pair = jnp.where(lane_hi, outs[1], outs[0])
        o_ref[0, :, POOL_WIDTH + c0:POOL_WIDTH + c0 + LANES] = o_pair.astype(o_ref.dtype)


def _attn_bias_tile(rel_bias):
    nh = rel_bias.shape[0]
    i = jnp.arange(ATT_TQ)[:, None]
    s = jnp.arange(ATT_TK)[None, :]
    n_const = ATT_PREV + ATT_TQ - REL_CLIP
    n_tail = ATT_TK + ATT_TQ - 1 - n_const
    v = jnp.concatenate([jnp.broadcast_to(rel_bias[:, 2 * REL_CLIP:], (nh, n_const)),
                         rel_bias[:, 2 * REL_CLIP - n_tail:2 * REL_CLIP][:, ::-1]], axis=1).astype(F32)
    bias = jnp.stack([v[:, ATT_TQ - 1 - r:ATT_TQ - 1 - r + ATT_TK] for r in range(ATT_TQ)], axis=1)
    band = s - (i // ATT_CHUNK) * ATT_CHUNK
    ok = (band >= 0) & (band < (ATT_PREV_CHUNKS + 1) * ATT_CHUNK)
    return jnp.where(ok[None], bias, NEG_BIG)


def even_mixer(proj, bias_tile, pool_w, pool_scale):
    b, s, n = proj.shape
    tq = ATT_TQ
    half = n // 2
    halo_blocks = tq // POOL_HALO
    return pl.pallas_call(
        _even_mixer_kernel,
        grid=(b, s // tq),
        in_specs=[
            pl.BlockSpec((1, tq, half), lambda i, j: (i, j, 0)),
            pl.BlockSpec((1, tq, half), lambda i, j: (i, jnp.maximum(j - 2, 0), 1)),
            pl.BlockSpec((1, tq, half), lambda i, j: (i, jnp.maximum(j - 1, 0), 1)),
            pl.BlockSpec((1, tq, half), lambda i, j: (i, j, 1)),
            pl.BlockSpec((1, POOL_HALO, POOL_WIDTH), lambda i, j: (i, jnp.maximum(j * halo_blocks - 1, 0), 0)),
            _resident(bias_tile.shape),
            _resident(pool_w.shape),
            _resident((1, POOL_WIDTH)),
        ],
        out_specs=pl.BlockSpec((1, tq, POOL_WIDTH + ATT_WIDTH), lambda i, j: (i, j, 0)),
        out_shape=jax.ShapeDtypeStruct((b, s, POOL_WIDTH + ATT_WIDTH), BF16),
        scratch_shapes=[pltpu.VMEM((POOL_HALO + tq, POOL_WIDTH), F32)],
        compiler_params=_cparams(("parallel", "parallel")),
        name="even_mixer",
    )(proj, proj, proj, proj, proj, bias_tile, pool_w, pool_scale.reshape(1, POOL_WIDTH))


def _out_proj_kernel(a_ref, w_ref, x_ref, mod_ref, nw_ref, xo_ref, h_ref, *, grouped):
    m = mod_ref[0]
    if grouped:
        ng, _, kw = a_ref.shape[1], a_ref.shape[2], a_ref.shape[3]
        mix = jnp.dot(a_ref[0, 0], w_ref[0:kw, :], preferred_element_type=F32)
        for g in range(1, ng):
            mix = mix + jnp.dot(a_ref[0, g], w_ref[g * kw:(g + 1) * kw, :], preferred_element_type=F32)
    else:
        mix = jnp.dot(a_ref[0], w_ref[...], preferred_element_type=F32)
    xn = x_ref[0] + m[2:3] * mix
    xo_ref[0] = xn
    h_ref[0] = _rms_mod(xn, nw_ref[...], m[3:4], m[4:5]).astype(h_ref.dtype)


def out_proj(a, w, x, mod, nw2, tm):
    b, s, d = x.shape
    grouped = a.ndim == 4
    if grouped:
        a_spec = pl.BlockSpec((1, a.shape[1], tm, a.shape[3]), lambda i, j: (i, 0, j, 0))
    else:
        a_spec = pl.BlockSpec((1, tm, a.shape[2]), lambda i, j: (i, j, 0))
    return pl.pallas_call(
        functools.partial(_out_proj_kernel, grouped=grouped),
        grid=(b, s // tm),
        in_specs=[
            a_spec,
            _resident(w.shape),
            pl.BlockSpec((1, tm, d), lambda i, j: (i, j, 0)),
            pl.BlockSpec((1, 6, d), lambda i, j: (i, 0, 0)),
            _resident((1, d)),
        ],
        out_specs=[
            pl.BlockSpec((1, tm, d), lambda i, j: (i, j, 0)),
            pl.BlockSpec((1, tm, d), lambda i, j: (i, j, 0)),
        ],
        out_shape=[jax.ShapeDtypeStruct((b, s, d), F32), jax.ShapeDtypeStruct((b, s, d), BF16)],
        compiler_params=_cparams(("parallel", "parallel")),
        name="out_proj",
    )(a, w, x, mod, nw2.reshape(1, d))


def _ssd_kernel(z_ref, x_ref, b_ref, c_ref, dt_ref, cwx_ref, cwb_ref, cwc_ref, cbx_ref, cbb_ref, cbc_ref,
                dtb_ref, alog_ref, dskip_ref, nw_ref, o_ref,
                state, hx, hb, hc, ext, acsb, acst, dtt, wt):
    j = pl.program_id(1)
    L = SSD_L
    H = SSM_CONV_HALO

    @pl.when(j == 0)
    def _():
        state[...] = jnp.zeros_like(state)
        hx[...] = jnp.zeros_like(hx)
        hb[...] = jnp.zeros_like(hb)
        hc[...] = jnp.zeros_like(hc)

    dt = jax.nn.softplus(dt_ref[0] + dtb_ref[...])
    a = dt * (-jnp.exp(alog_ref[...]))
    row = lax.broadcasted_iota(jnp.int32, (L, L), 0)
    col = lax.broadcasted_iota(jnp.int32, (L, L), 1)
    tril = row >= col
    a_cs = jnp.dot(tril.astype(F32), a, preferred_element_type=F32, precision=lax.Precision.HIGHEST)
    a_last = a_cs[L - 1:L, :]
    acst[...] = a_cs.T
    dtt[...] = dt.T
    wt[...] = (dt * jnp.exp(a_last - a_cs)).T
    for h in range(SSM_HEADS):
        acsb[h] = jnp.broadcast_to(a_cs[:, h:h + 1], (L, LANES))

    lane_hi = lax.broadcasted_iota(jnp.int32, (1, LANES), 1) >= SSM_HEAD_DIM

    def conv_silu(src_ref, halo_ref, cw_ref, cb_ref, g, width):
        cur = src_ref[0, g].astype(F32)
        ext[0:H, 0:width] = halo_ref[g]
        ext[H:H + L, 0:width] = cur
        halo_ref[g] = cur[L - H:L, :]
        w = cw_ref[g]
        acc = cb_ref[g]
        for tap in range(SSM_D_CONV):
            o = H - (SSM_D_CONV - 1) + tap
            acc = acc + ext[o:o + L, 0:width] * w[tap:tap + 1, :]
        return _silu(acc)

    def group_body(g, carry):
        xs = conv_silu(x_ref, hx, cwx_ref, cbx_ref, g, SSM_GROUP_WIDTH)
        bm = conv_silu(b_ref, hb, cwb_ref, cbb_ref, g, SSM_D_STATE)
        cm = conv_silu(c_ref, hc, cwc_ref, cbc_ref, g, SSM_D_STATE)
        bmb = bm.astype(BF16)
        cmb = cm.astype(BF16)
        cb = lax.dot_general(cmb, bmb, (((1,), (1,)), ((), ())), preferred_element_type=F32)
        bmt = bm.T
        st = state[g]
        yoff = jnp.dot(cmb, st.astype(BF16), preferred_element_type=F32)
        zz = z_ref[0, g].astype(F32)
        dsk = dskip_ref[g]
        gated = []
        for q in range(2):
            c0 = q * LANES
            xs2 = xs[:, c0:c0 + LANES]
            xs2b = xs2.astype(BF16)
            ys, sns, cds = [], [], []
            for jj in range(2):
                h = g * SSM_HEADS_PER_GROUP + 2 * q + jj
                alb = acsb[h]
                asr = acst[pl.ds(h, 1), :]
                dtr = dtt[pl.ds(h, 1), :]
                wr = wt[pl.ds(h, 1), :]
                al2 = jnp.concatenate([alb] * (L // LANES), axis=1)
                seg = jnp.exp(jnp.where(tril, al2 - asr, -jnp.inf))
                mr = (cb * seg * dtr).astype(BF16)
                yd = jnp.dot(mr, xs2b, preferred_element_type=F32)
                ys.append(yd + jnp.exp(alb) * yoff[:, c0:c0 + LANES])
                lhs = (bmt * wr).astype(BF16)
                sns.append(jnp.dot(lhs, xs2b, preferred_element_type=F32))
                cds.append(jnp.exp(alb[L - 1:L, :]))
            y = jnp.where(lane_hi, ys[1], ys[0]) + xs2 * dsk[:, c0:c0 + LANES]
            sn = jnp.where(lane_hi, sns[1], sns[0])
            cd = jnp.where(lane_hi, cds[1], cds[0])
            state[g, :, c0:c0 + LANES] = st[:, c0:c0 + LANES] * cd + sn
            gated.append(y * _silu(zz[:, c0:c0 + LANES]))
        ss = jnp.sum(gated[0] * gated[0], axis=-1, keepdims=True) + jnp.sum(gated[1] * gated[1], axis=-1, keepdims=True)
        rs = lax.rsqrt(ss / SSM_GROUP_WIDTH + EPS)
        nw = nw_ref[g]
        for q in range(2):
            c0 = q * LANES
            o_ref[0, g, :, c0:c0 + LANES] = (gated[q] * rs * nw[:, c0:c0 + LANES]).astype(o_ref.dtype)
        return carry

    lax.fori_loop(0, SSM_GROUPS, group_body, 0)


def ssd(z, xs, bm, cm, dt, conv_w, conv_b, dt_bias, a_log, d_skip, norm_w):
    b, ng, s, gw = z.shape
    L = SSD_L
    n = SSM_D_STATE
    di = SSM_D_INNER
    pad = LANES - SSM_HEADS
    cwx = conv_w[:, :di].reshape(SSM_D_CONV, ng, gw).transpose(1, 0, 2)
    cwb = conv_w[:, di:di + ng * n].reshape(SSM_D_CONV, ng, n).transpose(1, 0, 2)
    cwc = conv_w[:, di + ng * n:].reshape(SSM_D_CONV, ng, n).transpose(1, 0, 2)
    cbx = conv_b[:di].reshape(ng, 1, gw)
    cbb = conv_b[di:di + ng * n].reshape(ng, 1, n)
    cbc = conv_b[di + ng * n:].reshape(ng, 1, n)
    dtb = jnp.pad(dt_bias, (0, pad)).reshape(1, LANES)
    alog = jnp.pad(a_log, (0, pad)).reshape(1, LANES)
    dsk = jnp.repeat(d_skip, SSM_HEAD_DIM).reshape(ng, 1, gw)
    nw = norm_w.reshape(ng, 1, gw)
    tile4 = lambda w: pl.BlockSpec((1, ng, L, w), lambda i, j: (i, 0, j, 0))
    return pl.pallas_call(
        _ssd_kernel,
        grid=(b, s // L),
        in_specs=[
            tile4(gw), tile4(gw), tile4(n), tile4(n),
            pl.BlockSpec((1, L, LANES), lambda i, j: (i, j, 0)),
            _resident(cwx.shape), _resident(cwb.shape), _resident(cwc.shape),
            _resident(cbx.shape), _resident(cbb.shape), _resident(cbc.shape),
            _resident(dtb.shape), _resident(alog.shape), _resident(dsk.shape), _resident(nw.shape),
        ],
        out_specs=tile4(gw),
        out_shape=jax.ShapeDtypeStruct((b, ng, s, gw), BF16),
        scratch_shapes=[
            pltpu.VMEM((ng, n, gw), F32),
            pltpu.VMEM((ng, SSM_CONV_HALO, gw), F32),
            pltpu.VMEM((ng, SSM_CONV_HALO, n), F32),
            pltpu.VMEM((ng, SSM_CONV_HALO, n), F32),
            pltpu.VMEM((SSM_CONV_HALO + L, gw), F32),
            pltpu.VMEM((SSM_HEADS, L, LANES), F32),
            pltpu.VMEM((LANES, L), F32),
            pltpu.VMEM((LANES, L), F32),
            pltpu.VMEM((LANES, L), F32),
        ],
        compiler_params=_cparams(("parallel", "arbitrary")),
        name="ssd",
    )(z, xs, bm, cm, dt, cwx, cwb, cwc, cbx, cbb, cbc, dtb, alog, dsk, nw)


def _route(logits):
    lane = lax.broadcasted_iota(jnp.int32, logits.shape, 1)
    is_g = (lane >= MOE_EXPERTS) & (lane < MOE_EXPERTS + MOE_GROUPS)
    gl = jnp.where(is_g, logits, -jnp.inf)
    gmax = jnp.max(gl, axis=-1, keepdims=True)
    gsum = jnp.sum(jnp.exp(gl - gmax), axis=-1, keepdims=True)
    g_val = 1.0 / gsum
    g_lane = jnp.min(jnp.where(gl == gmax, lane, 2 * LANES), axis=-1, keepdims=True)
    g_idx = g_lane - MOE_EXPERTS
    in_grp = (lane >= g_idx * MOE_EPG) & (lane < (g_idx + 1) * MOE_EPG)
    el = jnp.where(in_grp, logits, -jnp.inf)
    emax = jnp.max(el, axis=-1, keepdims=True)
    ee = jnp.exp(el - emax)
    p = ee / jnp.sum(ee, axis=-1, keepdims=True)
    p = jnp.where(in_grp, p, -1.0)
    m1 = jnp.max(p, axis=-1, keepdims=True)
    i1 = jnp.min(jnp.where(p == m1, lane, 2 * LANES), axis=-1, keepdims=True)
    p2 = jnp.where(lane == i1, -1.0, p)
    m2 = jnp.max(p2, axis=-1, keepdims=True)
    i2 = jnp.min(jnp.where(p2 == m2, lane, 2 * LANES), axis=-1, keepdims=True)
    tot = m1 + m2
    w1 = g_val * (m1 / tot)
    w2 = g_val * (m2 / tot)
    return jnp.where(lane == i1, w1, 0.0) + jnp.where(lane == i2, w2, 0.0)


def _moe_dense_kernel(h_ref, wr_ref, br_ref, w1_ref, w3_ref, w2_ref, x_ref, mod_ref, o_ref, acc, comb, *, eb):
    e = pl.program_id(1)
    h = h_ref[...]

    @pl.when(e == 0)
    def _():
        logits = jnp.dot(h, wr_ref[...], preferred_element_type=F32) + br_ref[...]
        comb[...] = _route(logits)
        acc[...] = jnp.zeros_like(acc)

    a = jnp.dot(h, w1_ref[...], preferred_element_type=F32)
    b = jnp.dot(h, w3_ref[...], preferred_element_type=F32)
    hid = _silu(a) * b
    cm = comb[...]
    lane = lax.broadcasted_iota(jnp.int32, cm.shape, 1)
    parts = []
    for q in range(eb):
        cw = jnp.sum(jnp.where(lane == e * eb + q, cm, 0.0), axis=-1, keepdims=True)
        parts.append((hid[:, q * MOE_D_FF:(q + 1) * MOE_D_FF] * cw).astype(BF16))
    hb = jnp.concatenate(parts, axis=1)
    acc[...] += jnp.dot(hb, w2_ref[...], preferred_element_type=F32)

    @pl.when(e == pl.num_programs(1) - 1)
    def _():
        m = mod_ref[0]
        o_ref[...] = x_ref[...] + m[5:6] * acc[...]


def moe_dense(h2, wr, br, w1, w3, w2, x, mod, tm, eb=4):
    b, s, d = x.shape
    t = b * s
    tiles_per_batch = s // tm
    ebw = eb * MOE_D_FF
    out = pl.pallas_call(
        functools.partial(_moe_dense_kernel, eb=eb),
        grid=(t // tm, MOE_EXPERTS // eb),
        in_specs=[
            pl.BlockSpec((tm, d), lambda i, e: (i, 0)),
            _resident(wr.shape),
            _resident(br.shape),
            pl.BlockSpec((d, ebw), lambda i, e: (0, e)),
            pl.BlockSpec((d, ebw), lambda i, e: (0, e)),
            pl.BlockSpec((ebw, d), lambda i, e: (e, 0)),
            pl.BlockSpec((tm, d), lambda i, e: (i, 0)),
            pl.BlockSpec((1, 6, d), lambda i, e: (i // tiles_per_batch, 0, 0)),
        ],
        out_specs=pl.BlockSpec((tm, d), lambda i, e: (i, 0)),
        out_shape=jax.ShapeDtypeStruct((t, d), F32),
        scratch_shapes=[pltpu.VMEM((tm, d), F32), pltpu.VMEM((tm, LANES), F32)],
        compiler_params=_cparams(("parallel", "arbitrary")),
        name="moe_dense",
    )(h2.reshape(t, d), wr, br, w1, w3, w2, x.reshape(t, d), mod)
    return out.reshape(b, s, d)


def _final_norm_kernel(x_ref, nw_ref, o_ref):
    xv = x_ref[...]
    ms = jnp.mean(xv * xv, axis=-1, keepdims=True)
    o_ref[...] = xv * lax.rsqrt(ms + EPS) * nw_ref[...]


def final_norm(x, nw, tm):
    b, s, d = x.shape
    t = b * s
    return pl.pallas_call(
        _final_norm_kernel,
        grid=(t // tm,),
        in_specs=[pl.BlockSpec((tm, d), lambda i: (i, 0)), _resident((1, d))],
        out_specs=pl.BlockSpec((tm, d), lambda i: (i, 0)),
        out_shape=jax.ShapeDtypeStruct((t, d), F32),
        compiler_params=_cparams(("parallel",)),
        name="final_norm",
    )(x.reshape(t, d), nw.reshape(1, d)).reshape(b, s, d)


def _even_chunks(n):
    cw = 512
    return tuple((c, cw, ((0, cw, 0, None, c),)) for c in range(0, n, cw))


def _ssm_chunks():
    gw, n, ng = SSM_GROUP_WIDTH, SSM_D_STATE, SSM_GROUPS
    chunks = []
    col = 0
    for oi in (0, 1):
        for g in range(0, ng, 2):
            chunks.append((col, 2 * gw, ((0, gw, oi, g, 0), (gw, gw, oi, g + 1, 0))))
            col += 2 * gw
    for oi in (2, 3):
        for g in range(0, ng, 4):
            chunks.append((col, 4 * n, tuple((k * n, n, oi, g + k, 0) for k in range(4))))
            col += 4 * n
    chunks.append((col, LANES, ((0, LANES, 4, None, 0),)))
    return tuple(chunks), col + LANES


def _router_weights(w_group, b_group, w_expert, b_expert):
    d = w_group.shape[0]
    pad = LANES - MOE_EXPERTS - MOE_GROUPS
    wr = jnp.concatenate([w_expert.reshape(d, MOE_EXPERTS), w_group, jnp.zeros((d, pad), F32)], axis=1)
    br = jnp.concatenate([b_expert.reshape(MOE_EXPERTS), b_group, jnp.zeros((pad,), F32)]).reshape(1, LANES)
    return wr.astype(BF16), br


def kernel(x, c, ada_w, ada_b, norm1_w, norm2_w, mix_w_in, pool_w, pool_scale, rel_bias, mix_w_out, ssm_w_in, ssm_conv_w, ssm_conv_b, ssm_dt_bias, ssm_A_log, ssm_D, ssm_norm_w, ssm_w_out, moe_w_group, moe_b_group, moe_w_expert, moe_b_expert, moe_w1, moe_w3, moe_w2, final_norm_w):
    b, s, d = x.shape
    depth = ada_w.shape[0]
    tm = 512
    mod_all = adaln(c, ada_w, ada_b)
    ssm_chunks, ssm_cols = _ssm_chunks()
    for layer in range(depth):
        mod = mod_all[layer].reshape(b, 6, d)
        i = layer // 2
        if layer % 2 == 0:
            w_in = mix_w_in[i].astype(BF16)
            (proj,) = norm_proj(x, norm1_w[layer], mod, w_in,
                                [jax.ShapeDtypeStruct((b, s, w_in.shape[1]), BF16)], _even_chunks(w_in.shape[1]), tm)
            a = even_mixer(proj, _attn_bias_tile(rel_bias[i]), pool_w[i].astype(BF16), pool_scale[i])
            x, h2 = out_proj(a, mix_w_out[i].astype(BF16), x, mod, norm2_w[layer], tm)
        else:
            w_in = jnp.pad(ssm_w_in[i], ((0, 0), (0, ssm_cols - ssm_w_in.shape[2]))).astype(BF16)
            ng, gw, n = SSM_GROUPS, SSM_GROUP_WIDTH, SSM_D_STATE
            shapes = [
                jax.ShapeDtypeStruct((b, ng, s, gw), BF16),
                jax.ShapeDtypeStruct((b, ng, s, gw), BF16),
                jax.ShapeDtypeStruct((b, ng, s, n), BF16),
                jax.ShapeDtypeStruct((b, ng, s, n), BF16),
                jax.ShapeDtypeStruct((b, s, LANES), F32),
            ]
            z, xs, bm, cm, dt = norm_proj(x, norm1_w[layer], mod, w_in, shapes, ssm_chunks, tm)
            g = ssd(z, xs, bm, cm, dt, ssm_conv_w[i], ssm_conv_b[i], ssm_dt_bias[i], ssm_A_log[i], ssm_D[i],
                    ssm_norm_w[i])
            x, h2 = out_proj(g, ssm_w_out[i].astype(BF16), x, mod, norm2_w[layer], tm)
        wr, br = _router_weights(moe_w_group[layer], moe_b_group[layer], moe_w_expert[layer], moe_b_expert[layer])
        e, _, f = moe_w1[layer].shape
        w1 = moe_w1[layer].transpose(1, 0, 2).reshape(d, e * f).astype(BF16)
        w3 = moe_w3[layer].transpose(1, 0, 2).reshape(d, e * f).astype(BF16)
        w2 = moe_w2[layer].reshape(e * f, d).astype(BF16)
        x = moe_dense(h2, wr, br, w1, w3, w2, x, mod, tm)
    return final_norm(x, final_norm_w, tm)
```

```python
import functools
import math

import jax
import jax.numpy as jnp
from jax import lax
from jax.experimental import pallas as pl
from jax.experimental.pallas import tpu as pltpu

F32 = jnp.float32
BF16 = jnp.bfloat16

EPS = 1e-6
LANES = 128

POOL_WINDOWS = (2, 4, 8, 16)
POOL_GROUP = 128
POOL_WIDTH = POOL_GROUP * len(POOL_WINDOWS)
POOL_HALO = 16
ATT_HEAD_DIM = 64
ATT_HEADS = 8
ATT_WIDTH = ATT_HEADS * ATT_HEAD_DIM
ATT_CHUNK = 64
ATT_PREV_CHUNKS = 8
ATT_TQ = 256
ATT_PREV = ATT_PREV_CHUNKS * ATT_CHUNK
ATT_TK = ATT_PREV + ATT_TQ
REL_CLIP = 256
NEG_BIG = -1e30

SSM_GROUPS = 8
SSM_HEADS = 32
SSM_HEADS_PER_GROUP = 4
SSM_HEAD_DIM = 64
SSM_D_STATE = 128
SSM_D_INNER = SSM_HEADS * SSM_HEAD_DIM
SSM_GROUP_WIDTH = SSM_HEADS_PER_GROUP * SSM_HEAD_DIM
SSM_D_CONV = 4
SSM_CONV_HALO = 8
SSD_L = 256

MOE_GROUPS = 4
MOE_EPG = 8
MOE_EXPERTS = 32
MOE_D_FF = 128
MOE_TM = 512
MOE_CH = 32
MOE_LMAX = MOE_TM + MOE_GROUPS * MOE_CH
MOE_RT = 256
MOE_META = 3 * MOE_GROUPS

VMEM_LIMIT = 56 * 1024 * 1024


def _cparams(sem):
    return pltpu.CompilerParams(dimension_semantics=sem, vmem_limit_bytes=VMEM_LIMIT)


def _resident(shape):
    nd = len(shape)
    return pl.BlockSpec(shape, lambda *_: (0,) * nd, pipeline_mode=pl.Buffered(1))


def _silu(v):
    return v * jax.nn.sigmoid(v)


def _rms_mod(xv, nw, shift, scale):
    ms = jnp.mean(xv * xv, axis=-1, keepdims=True)
    y = xv * lax.rsqrt(ms + EPS) * nw
    return y * (1.0 + scale) + shift


def _adaln_kernel(c_ref, w_ref, b_ref, o_ref):
    c = c_ref[...]
    ca = _silu(c).astype(BF16)
    o_ref[0] = jnp.dot(ca, w_ref[0].astype(BF16), preferred_element_type=F32) + b_ref[0]


def adaln(c, ada_w, ada_b):
    depth, d, n = ada_w.shape
    b = c.shape[0]
    tn = 1536
    return pl.pallas_call(
        _adaln_kernel,
        grid=(depth, n // tn),
        in_specs=[
            pl.BlockSpec((b, d), lambda l, j: (0, 0)),
            pl.BlockSpec((1, d, tn), lambda l, j: (l, 0, j)),
            pl.BlockSpec((1, 1, tn), lambda l, j: (l, 0, j)),
        ],
        out_specs=pl.BlockSpec((1, b, tn), lambda l, j: (l, 0, j)),
        out_shape=jax.ShapeDtypeStruct((depth, b, n), F32),
        compiler_params=_cparams(("arbitrary", "arbitrary")),
        name="adaln",
    )(c, ada_w, ada_b.reshape(depth, 1, n))


def _norm_proj_kernel(x_ref, nw_ref, mod_ref, w_ref, *o_refs, chunks):
    m = mod_ref[0]
    h = _rms_mod(x_ref[0], nw_ref[...], m[0:1], m[1:2]).astype(BF16)
    for col, width, stores in chunks:
        r = jnp.dot(h, w_ref[:, col:col + width], preferred_element_type=F32)
        for off, w, oi, g, dst in stores:
            o_ref = o_refs[oi]
            v = r[:, off:off + w].astype(o_ref.dtype)
            if g is None:
                o_ref[0, :, dst:dst + w] = v
            else:
                o_ref[0, g, :, dst:dst + w] = v


def norm_proj(x, nw, mod, w, out_shapes, chunks, tm):
    b, s, d = x.shape
    out_specs = []
    for sh in out_shapes:
        if len(sh.shape) == 3:
            out_specs.append(pl.BlockSpec((1, tm, sh.shape[2]), lambda i, j: (i, j, 0)))
        else:
            out_specs.append(pl.BlockSpec((1, sh.shape[1], tm, sh.shape[3]), lambda i, j: (i, 0, j, 0)))
    return pl.pallas_call(
        functools.partial(_norm_proj_kernel, chunks=chunks),
        grid=(b, s // tm),
        in_specs=[
            pl.BlockSpec((1, tm, d), lambda i, j: (i, j, 0)),
            _resident((1, d)),
            pl.BlockSpec((1, 6, d), lambda i, j: (i, 0, 0)),
            _resident(w.shape),
        ],
        out_specs=out_specs,
        out_shape=out_shapes,
        compiler_params=_cparams(("parallel", "parallel")),
        name="norm_proj",
    )(x, nw.reshape(1, d), mod, w)


def _even_mixer_kernel(uq_ref, kv2_ref, kv1_ref, kv0_ref, halo_ref, bias_ref, pw_ref, ps_ref, o_ref, ubuf):
    j = pl.program_id(1)
    tq = ATT_TQ

    u = uq_ref[0, :, 0:POOL_WIDTH].astype(F32)
    halo = halo_ref[0].astype(F32)
    ubuf[0:POOL_HALO, :] = jnp.where(j > 0, halo, 0.0)
    ubuf[POOL_HALO:POOL_HALO + tq, :] = u
    pos1 = j * tq + 1 + lax.broadcasted_iota(jnp.int32, (tq, 1), 0)
    for g, win in enumerate(POOL_WINDOWS):
        lo = g * POOL_GROUP
        acc = ubuf[POOL_HALO:POOL_HALO + tq, lo:lo + POOL_GROUP]
        for k in range(1, win):
            acc = acc + ubuf[POOL_HALO - k:POOL_HALO - k + tq, lo:lo + POOL_GROUP]
        cnt = jnp.minimum(pos1, win).astype(F32)
        pooled = acc / cnt - u[:, lo:lo + POOL_GROUP]
        y = jnp.dot(pooled.astype(BF16), pw_ref[g], preferred_element_type=F32)
        o_ref[0, :, lo:lo + POOL_GROUP] = (y * ps_ref[:, lo:lo + POOL_GROUP]).astype(o_ref.dtype)

    kpos = j * tq - ATT_PREV + lax.broadcasted_iota(jnp.int32, (1, ATT_TK), 1)
    kvalid = kpos >= 0
    lane_hi = lax.broadcasted_iota(jnp.int32, (1, LANES), 1) >= ATT_HEAD_DIM
    scale = ATT_HEAD_DIM ** -0.5
    for hp in range(ATT_HEADS // 2):
        c0 = hp * LANES
        q2 = uq_ref[0, :, POOL_WIDTH + c0:POOL_WIDTH + c0 + LANES]
        k2 = jnp.concatenate([r[0, :, c0:c0 + LANES] for r in (kv2_ref, kv1_ref, kv0_ref)], axis=0)
        v2 = jnp.concatenate(
            [r[0, :, ATT_WIDTH + c0:ATT_WIDTH + c0 + LANES] for r in (kv2_ref, kv1_ref, kv0_ref)], axis=0)
        outs = []
        for jj in range(2):
            qm = jnp.where(lane_hi if jj else jnp.logical_not(lane_hi), q2, jnp.zeros_like(q2))
            sc = lax.dot_general(qm, k2, (((1,), (1,)), ((), ())), preferred_element_type=F32)
            sc = sc * scale + bias_ref[2 * hp + jj]
            sc = jnp.where(kvalid, sc, NEG_BIG)
            mx = jnp.max(sc, axis=-1, keepdims=True)
            p = jnp.exp(sc - mx)
            den = jnp.sum(p, axis=-1, keepdims=True)
            o = jnp.dot(p.astype(BF16), v2, preferred_element_type=F32)
            outs.append(o / den)
        o_pair = jnp.where(lane_hi, outs[1], outs[0])
        o_ref[0, :, POOL_WIDTH + c0:POOL_WIDTH + c0 + LANES] = o_pair.astype(o_ref.dtype)


def _attn_bias_tile(rel_bias):
    nh = rel_bias.shape[0]
    i = jnp.arange(ATT_TQ)[:, None]
    s = jnp.arange(ATT_TK)[None, :]
    n_const = ATT_PREV + ATT_TQ - REL_CLIP
    n_tail = ATT_TK + ATT_TQ - 1 - n_const
    v = jnp.concatenate([jnp.broadcast_to(rel_bias[:, 2 * REL_CLIP:], (nh, n_const)),
                         rel_bias[:, 2 * REL_CLIP - n_tail:2 * REL_CLIP][:, ::-1]], axis=1).astype(F32)
    bias = jnp.stack([v[:, ATT_TQ - 1 - r:ATT_TQ - 1 - r + ATT_TK] for r in range(ATT_TQ)], axis=1)
    band = s - (i // ATT_CHUNK) * ATT_CHUNK
    ok = (band >= 0) & (band < (ATT_PREV_CHUNKS + 1) * ATT_CHUNK)
    return jnp.where(ok[None], bias, NEG_BIG)


def even_mixer(proj, bias_tile, pool_w, pool_scale):
    b, s, n = proj.shape
    tq = ATT_TQ
    half = n // 2
    halo_blocks = tq // POOL_HALO
    return pl.pallas_call(
        _even_mixer_kernel,
        grid=(b, s // tq),
        in_specs=[
            pl.BlockSpec((1, tq, half), lambda i, j: (i, j, 0)),
            pl.BlockSpec((1, tq, half), lambda i, j: (i, jnp.maximum(j - 2, 0), 1)),
            pl.BlockSpec((1, tq, half), lambda i, j: (i, jnp.maximum(j - 1, 0), 1)),
            pl.BlockSpec((1, tq, half), lambda i, j: (i, j, 1)),
            pl.BlockSpec((1, POOL_HALO, POOL_WIDTH), lambda i, j: (i, jnp.maximum(j * halo_blocks - 1, 0), 0)),
            _resident(bias_tile.shape),
            _resident(pool_w.shape),
            _resident((1, POOL_WIDTH)),
        ],
        out_specs=pl.BlockSpec((1, tq, POOL_WIDTH + ATT_WIDTH), lambda i, j: (i, j, 0)),
        out_shape=jax.ShapeDtypeStruct((b, s, POOL_WIDTH + ATT_WIDTH), BF16),
        scratch_shapes=[pltpu.VMEM((POOL_HALO + tq, POOL_WIDTH), F32)],
        compiler_params=_cparams(("parallel", "parallel")),
        name="even_mixer",
    )(proj, proj, proj, proj, proj, bias_tile, pool_w, pool_scale.reshape(1, POOL_WIDTH))


def _group_logits(logits):
    lane = lax.broadcasted_iota(jnp.int32, logits.shape, 1)
    is_g = (lane >= MOE_EXPERTS) & (lane < MOE_EXPERTS + MOE_GROUPS)
    gl = jnp.where(is_g, logits, -jnp.inf)
    return lane, gl, jnp.max(gl, axis=-1, keepdims=True)


def _chunk_copy(src, dst, sem):
    return pltpu.make_async_copy(src, dst, sem)


def _out_proj_dispatch_kernel(a_ref, w_ref, x_ref, mod_ref, nw_ref, wr_ref, br_ref, ltri_ref,
                              xo_ref, lp_ref, meta_ref, tot_ref, sorted_ref,
                              sbuf, zbuf, sem, run_s, nis_s, *, grouped, cap):
    i = pl.program_id(0)
    n = pl.num_programs(0)
    slot = i % 2
    tm, ch = MOE_TM, MOE_CH

    @pl.when(i == 0)
    def _():
        for g in range(MOE_GROUPS):
            run_s[g] = 0
        nis_s[0] = 0
        nis_s[1] = 0
        zbuf[...] = jnp.zeros_like(zbuf)

    m = mod_ref[0]
    if grouped:
        ng, kw = a_ref.shape[1], a_ref.shape[3]
        mix = jnp.dot(a_ref[0, 0], w_ref[0:kw, :], preferred_element_type=F32)
        for g in range(1, ng):
            mix = mix + jnp.dot(a_ref[0, g], w_ref[g * kw:(g + 1) * kw, :], preferred_element_type=F32)
    else:
        mix = jnp.dot(a_ref[...], w_ref[...], preferred_element_type=F32)
    xn = x_ref[...] + m[2:3] * mix
    xo_ref[...] = xn
    h2 = _rms_mod(xn, nw_ref[...], m[3:4], m[4:5]).astype(BF16)

    logits = jnp.dot(h2, wr_ref[...], preferred_element_type=F32) + br_ref[...]
    lane, gl, gmax = _group_logits(logits)
    g_lane = jnp.min(jnp.where(gl == gmax, lane, 2 * LANES), axis=-1, keepdims=True)
    onehot = (lane == g_lane).astype(F32)
    rank = jnp.dot(ltri_ref[...], onehot.astype(BF16), preferred_element_type=F32)
    cnt_row = jnp.sum(onehot, axis=0, keepdims=True)
    lane1 = lax.broadcasted_iota(jnp.int32, (1, LANES), 1)
    nch, los = [], []
    lo = jnp.int32(0)
    lo_row = jnp.zeros((1, LANES), F32)
    for g in range(MOE_GROUPS):
        cnt = jnp.sum(jnp.where(lane1 == MOE_EXPERTS + g, cnt_row, 0.0)).astype(jnp.int32)
        nch.append((cnt + (ch - 1)) // ch)
        los.append(lo)
        lo_row = jnp.where(lane1 == MOE_EXPERTS + g, lo.astype(F32), lo_row)
        lo = lo + nch[g] * ch
    lp = jnp.sum(onehot * (rank + lo_row), axis=-1, keepdims=True)
    lp_ref[...] = lp.astype(jnp.int32)
    lpt = jnp.broadcast_to(lp, (tm, LANES)).T[0:1, :]
    rowi = lax.broadcasted_iota(jnp.int32, (MOE_LMAX, tm), 0).astype(F32)
    pm = jnp.where(rowi == lpt, 1.0, 0.0).astype(BF16)

    def _wait(sl, count):
        def body(c, carry):
            _chunk_copy(sbuf.at[sl, pl.ds(0, ch)], sorted_ref.at[pl.ds(0, ch)], sem.at[sl]).wait()
            return carry
        lax.fori_loop(0, count, body, 0)

    _wait(slot, nis_s[slot])
    sbuf[slot] = jnp.dot(pm, h2, preferred_element_type=F32).astype(BF16)
    total = jnp.int32(0)
    for g in range(MOE_GROUPS):
        dst0 = g * cap + run_s[g]

        def issue(c, carry, g=g, dst0=dst0):
            off = pl.multiple_of(c * ch, ch)
            _chunk_copy(sbuf.at[slot, pl.ds(pl.multiple_of(los[g] + off, ch), ch)],
                        sorted_ref.at[pl.ds(pl.multiple_of(dst0 + off, ch), ch)], sem.at[slot]).start()
            return carry

        lax.fori_loop(0, nch[g], issue, 0)
        meta_ref[i * MOE_META + 3 * g] = dst0
        meta_ref[i * MOE_META + 3 * g + 1] = nch[g]
        meta_ref[i * MOE_META + 3 * g + 2] = los[g]
        run_s[g] = run_s[g] + nch[g] * ch
        total = total + nch[g]
    nis_s[slot] = total

    @pl.when(i == n - 1)
    def _():
        for g in range(MOE_GROUPS):
            r = run_s[g]
            full = (r + (MOE_RT - 1)) // MOE_RT * MOE_RT
            nfill = (full - r) // ch
            base = g * cap + r

            def fill(c, carry, base=base):
                off = pl.multiple_of(c * ch, ch)
                _chunk_copy(zbuf, sorted_ref.at[pl.ds(pl.multiple_of(base + off, ch), ch)], sem.at[2]).start()
                return carry

            def fill_wait(c, carry):
                _chunk_copy(zbuf, sorted_ref.at[pl.ds(0, ch)], sem.at[2]).wait()
                return carry

            lax.fori_loop(0, nfill, fill, 0)
            lax.fori_loop(0, nfill, fill_wait, 0)
            tot_ref[g] = full
            tot_ref[MOE_GROUPS + g] = 0
        _wait(0, nis_s[0])
        _wait(1, nis_s[1])


def out_proj_dispatch(a, w, x, mod, nw2, wr, br, cap):
    b, s, d = x.shape
    t = b * s
    tm = MOE_TM
    tpb = s // tm
    nt = t // tm
    grouped = a.ndim == 4
    if grouped:
        a_spec = pl.BlockSpec((1, a.shape[1], tm, a.shape[3]), lambda i: (i // tpb, 0, i % tpb, 0))
    else:
        a = a.reshape(t, a.shape[2])
        a_spec = pl.BlockSpec((tm, a.shape[1]), lambda i: (i, 0))
    ltri = (jnp.arange(tm)[:, None] > jnp.arange(tm)[None, :]).astype(BF16)
    smem = pl.BlockSpec(memory_space=pltpu.SMEM)
    xo, lp, meta, tot, srt = pl.pallas_call(
        functools.partial(_out_proj_dispatch_kernel, grouped=grouped, cap=cap),
        grid=(nt,),
        in_specs=[
            a_spec,
            _resident(w.shape),
            pl.BlockSpec((tm, d), lambda i: (i, 0)),
            pl.BlockSpec((1, 6, d), lambda i: (i // tpb, 0, 0)),
            _resident((1, d)),
            _resident(wr.shape),
            _resident(br.shape),
            _resident(ltri.shape),
        ],
        out_specs=[
            pl.BlockSpec((tm, d), lambda i: (i, 0)),
            pl.BlockSpec((tm, 1), lambda i: (i, 0)),
            smem,
            smem,
            pl.BlockSpec(memory_space=pl.ANY),
        ],
        out_shape=[
            jax.ShapeDtypeStruct((t, d), F32),
            jax.ShapeDtypeStruct((t, 1), jnp.int32),
            jax.ShapeDtypeStruct((nt * MOE_META,), jnp.int32),
            jax.ShapeDtypeStruct((2 * MOE_GROUPS,), jnp.int32),
            jax.ShapeDtypeStruct((MOE_GROUPS * cap, d), BF16),
        ],
        scratch_shapes=[
            pltpu.VMEM((2, MOE_LMAX, d), BF16),
            pltpu.VMEM((MOE_CH, d), BF16),
            pltpu.SemaphoreType.DMA((3,)),
            pltpu.SMEM((MOE_GROUPS,), jnp.int32),
            pltpu.SMEM((2,), jnp.int32),
        ],
        compiler_params=_cparams(("arbitrary",)),
        name="out_proj_dispatch",
    )(a, w, x.reshape(t, d), mod, nw2.reshape(1, d), wr, br, ltri)
    return xo, lp, meta, tot, srt


def _ssd_kernel(z_ref, x_ref, b_ref, c_ref, dt_ref, cwx_ref, cwb_ref, cwc_ref, cbx_ref, cbb_ref, cbc_ref,
                dtb_ref, alog_ref, dskip_ref, nw_ref, o_ref,
                state, hx, hb, hc, ext, acsb, acst, dtt, wt):
    j = pl.program_id(1)
    L = SSD_L
    H = SSM_CONV_HALO

    @pl.when(j == 0)
    def _():
        state[...] = jnp.zeros_like(state)
        hx[...] = jnp.zeros_like(hx)
        hb[...] = jnp.zeros_like(hb)
        hc[...] = jnp.zeros_like(hc)

    dt = jax.nn.softplus(dt_ref[0] + dtb_ref[...])
    a = dt * (-jnp.exp(alog_ref[...]))
    row = lax.broadcasted_iota(jnp.int32, (L, L), 0)
    col = lax.broadcasted_iota(jnp.int32, (L, L), 1)
    tril = row >= col
    a_cs = jnp.dot(tril.astype(F32), a, preferred_element_type=F32, precision=lax.Precision.HIGHEST)
    a_last = a_cs[L - 1:L, :]
    acst[...] = a_cs.T
    dtt[...] = dt.T
    wt[...] = (dt * jnp.exp(a_last - a_cs)).T
    for h in range(SSM_HEADS):
        acsb[h] = jnp.broadcast_to(a_cs[:, h:h + 1], (L, LANES))

    lane_hi = lax.broadcasted_iota(jnp.int32, (1, LANES), 1) >= SSM_HEAD_DIM

    def conv_silu(src_ref, halo_ref, cw_ref, cb_ref, g, width):
        cur = src_ref[0, g].astype(F32)
        ext[0:H, 0:width] = halo_ref[g]
        ext[H:H + L, 0:width] = cur
        halo_ref[g] = cur[L - H:L, :]
        w = cw_ref[g]
        acc = cb_ref[g]
        for tap in range(SSM_D_CONV):
            o = H - (SSM_D_CONV - 1) + tap
            acc = acc + ext[o:o + L, 0:width] * w[tap:tap + 1, :]
        return _silu(acc)

    def group_body(g, carry):
        xs = conv_silu(x_ref, hx, cwx_ref, cbx_ref, g, SSM_GROUP_WIDTH)
        bm = conv_silu(b_ref, hb, cwb_ref, cbb_ref, g, SSM_D_STATE)
        cm = conv_silu(c_ref, hc, cwc_ref, cbc_ref, g, SSM_D_STATE)
        bmb = bm.astype(BF16)
        cmb = cm.astype(BF16)
        cb = lax.dot_general(cmb, bmb, (((1,), (1,)), ((), ())), preferred_element_type=F32)
        bmt = bm.T
        st = state[g]
        yoff = jnp.dot(cmb, st.astype(BF16), preferred_element_type=F32)
        zz = z_ref[0, g].astype(F32)
        dsk = dskip_ref[g]
        gated = []
        for q in range(2):
            c0 = q * LANES
            xs2 = xs[:, c0:c0 + LANES]
            xs2b = xs2.astype(BF16)
            ys, sns, cds = [], [], []
            for jj in range(2):
                h = g * SSM_HEADS_PER_GROUP + 2 * q + jj
                alb = acsb[h]
                asr = acst[pl.ds(h, 1), :]
                dtr = dtt[pl.ds(h, 1), :]
                wr = wt[pl.ds(h, 1), :]
                al2 = jnp.concatenate([alb] * (L // LANES), axis=1)
                seg = jnp.exp(jnp.where(tril, al2 - asr, -jnp.inf))
                mr = (cb * seg * dtr).astype(BF16)
                yd = jnp.dot(mr, xs2b, preferred_element_type=F32)
                ys.append(yd + jnp.exp(alb) * yoff[:, c0:c0 + LANES])
                lhs = (bmt * wr).astype(BF16)
                sns.append(jnp.dot(lhs, xs2b, preferred_element_type=F32))
                cds.append(jnp.exp(alb[L - 1:L, :]))
            y = jnp.where(lane_hi, ys[1], ys[0]) + xs2 * dsk[:, c0:c0 + LANES]
            sn = jnp.where(lane_hi, sns[1], sns[0])
            cd = jnp.where(lane_hi, cds[1], cds[0])
            state[g, :, c0:c0 + LANES] = st[:, c0:c0 + LANES] * cd + sn
            gated.append(y * _silu(zz[:, c0:c0 + LANES]))
        ss = jnp.sum(gated[0] * gated[0], axis=-1, keepdims=True) + jnp.sum(gated[1] * gated[1], axis=-1, keepdims=True)
        rs = lax.rsqrt(ss / SSM_GROUP_WIDTH + EPS)
        nw = nw_ref[g]
        for q in range(2):
            c0 = q * LANES
            o_ref[0, g, :, c0:c0 + LANES] = (gated[q] * rs * nw[:, c0:c0 + LANES]).astype(o_ref.dtype)
        return carry

    lax.fori_loop(0, SSM_GROUPS, group_body, 0)


def ssd(z, xs, bm, cm, dt, conv_w, conv_b, dt_bias, a_log, d_skip, norm_w):
    b, ng, s, gw = z.shape
    L = SSD_L
    n = SSM_D_STATE
    di = SSM_D_INNER
    pad = LANES - SSM_HEADS
    cwx = conv_w[:, :di].reshape(SSM_D_CONV, ng, gw).transpose(1, 0, 2)
    cwb = conv_w[:, di:di + ng * n].reshape(SSM_D_CONV, ng, n).transpose(1, 0, 2)
    cwc = conv_w[:, di + ng * n:].reshape(SSM_D_CONV, ng, n).transpose(1, 0, 2)
    cbx = conv_b[:di].reshape(ng, 1, gw)
    cbb = conv_b[di:di + ng * n].reshape(ng, 1, n)
    cbc = conv_b[di + ng * n:].reshape(ng, 1, n)
    dtb = jnp.pad(dt_bias, (0, pad)).reshape(1, LANES)
    alog = jnp.pad(a_log, (0, pad)).reshape(1, LANES)
    dsk = jnp.repeat(d_skip, SSM_HEAD_DIM).reshape(ng, 1, gw)
    nw = norm_w.reshape(ng, 1, gw)
    tile4 = lambda w: pl.BlockSpec((1, ng, L, w), lambda i, j: (i, 0, j, 0))
    return pl.pallas_call(
        _ssd_kernel,
        grid=(b, s // L),
        in_specs=[
            tile4(gw), tile4(gw), tile4(n), tile4(n),
            pl.BlockSpec((1, L, LANES), lambda i, j: (i, j, 0)),
            _resident(cwx.shape), _resident(cwb.shape), _resident(cwc.shape),
            _resident(cbx.shape), _resident(cbb.shape), _resident(cbc.shape),
            _resident(dtb.shape), _resident(alog.shape), _resident(dsk.shape), _resident(nw.shape),
        ],
        out_specs=tile4(gw),
        out_shape=jax.ShapeDtypeStruct((b, ng, s, gw), BF16),
        scratch_shapes=[
            pltpu.VMEM((ng, n, gw), F32),
            pltpu.VMEM((ng, SSM_CONV_HALO, gw), F32),
            pltpu.VMEM((ng, SSM_CONV_HALO, n), F32),
            pltpu.VMEM((ng, SSM_CONV_HALO, n), F32),
            pltpu.VMEM((SSM_CONV_HALO + L, gw), F32),
            pltpu.VMEM((SSM_HEADS, L, LANES), F32),
            pltpu.VMEM((LANES, L), F32),
            pltpu.VMEM((LANES, L), F32),
            pltpu.VMEM((LANES, L), F32),
        ],
        compiler_params=_cparams(("parallel", "arbitrary")),
        name="ssd",
    )(z, xs, bm, cm, dt, cwx, cwb, cwc, cbx, cbb, cbc, dtb, alog, dsk, nw)


def _route_in_group(logits, g):
    lane, gl, gmax = _group_logits(logits)
    gsum = jnp.sum(jnp.exp(gl - gmax), axis=-1, keepdims=True)
    lg = jnp.sum(jnp.where(lane == MOE_EXPERTS + g, logits, 0.0), axis=-1, keepdims=True)
    g_val = jnp.exp(lg - gmax) / gsum
    in_grp = (lane >= g * MOE_EPG) & (lane < (g + 1) * MOE_EPG)
    el = jnp.where(in_grp, logits, -jnp.inf)
    emax = jnp.max(el, axis=-1, keepdims=True)
    ee = jnp.exp(el - emax)
    p = ee / jnp.sum(ee, axis=-1, keepdims=True)
    p = jnp.where(in_grp, p, -1.0)
    m1 = jnp.max(p, axis=-1, keepdims=True)
    i1 = jnp.min(jnp.where(p == m1, lane, 2 * LANES), axis=-1, keepdims=True)
    p2 = jnp.where(lane == i1, -1.0, p)
    m2 = jnp.max(p2, axis=-1, keepdims=True)
    i2 = jnp.min(jnp.where(p2 == m2, lane, 2 * LANES), axis=-1, keepdims=True)
    tot = m1 + m2
    w1 = g_val * (m1 / tot)
    w2 = g_val * (m2 / tot)
    return lane, jnp.where(lane == i1, w1, 0.0) + jnp.where(lane == i2, w2, 0.0)


def _moe_expert_kernel(tg_ref, tr_ref, nt_ref, rows_ref, wr_ref, br_ref, w1_ref, w3_ref, w2_ref, y_ref):
    i = pl.program_id(0)

    @pl.when(i < nt_ref[0])
    def _():
        g = tg_ref[i]
        rows = rows_ref[...]
        logits = jnp.dot(rows, wr_ref[...], preferred_element_type=F32) + br_ref[...]
        lane, comb = _route_in_group(logits, g)
        a = jnp.dot(rows, w1_ref[0], preferred_element_type=F32)
        b = jnp.dot(rows, w3_ref[0], preferred_element_type=F32)
        hid = _silu(a) * b
        parts = []
        for q in range(MOE_EPG):
            cw = jnp.sum(jnp.where(lane == g * MOE_EPG + q, comb, 0.0), axis=-1, keepdims=True)
            parts.append((hid[:, q * MOE_D_FF:(q + 1) * MOE_D_FF] * cw).astype(BF16))
        hb = jnp.concatenate(parts, axis=1)
        y_ref[...] = jnp.dot(hb, w2_ref[0], preferred_element_type=F32).astype(y_ref.dtype)


def moe_experts(tg, tr, nt, srt, wr, br, w1, w3, w2, nt_max):
    rows, d = srt.shape
    gw = MOE_EPG * MOE_D_FF
    grid_spec = pltpu.PrefetchScalarGridSpec(
        num_scalar_prefetch=3,
        grid=(nt_max,),
        in_specs=[
            pl.BlockSpec((MOE_RT, d), lambda i, tg, tr, nt: (tr[i], 0)),
            _resident(wr.shape),
            _resident(br.shape),
            pl.BlockSpec((1, d, gw), lambda i, tg, tr, nt: (tg[i], 0, 0)),
            pl.BlockSpec((1, d, gw), lambda i, tg, tr, nt: (tg[i], 0, 0)),
            pl.BlockSpec((1, gw, d), lambda i, tg, tr, nt: (tg[i], 0, 0)),
        ],
        out_specs=pl.BlockSpec((MOE_RT, d), lambda i, tg, tr, nt: (tr[i], 0)),
    )
    return pl.pallas_call(
        _moe_expert_kernel,
        grid_spec=grid_spec,
        out_shape=jax.ShapeDtypeStruct((rows, d), BF16),
        compiler_params=_cparams(("arbitrary",)),
        name="moe_experts",
    )(tg, tr, nt, srt, wr, br, w1, w3, w2)


def _moe_combine_kernel(meta_ref, x_ref, lp_ref, mod_ref, nw_ref, y_ref, o_ref, ybuf, sem, *, final):
    i = pl.program_id(0)
    n = pl.num_programs(0)
    ch = MOE_CH

    def run_copies(tile, sl, start):
        for g in range(MOE_GROUPS):
            dst0 = meta_ref[tile * MOE_META + 3 * g]
            nch = meta_ref[tile * MOE_META + 3 * g + 1]
            lo = meta_ref[tile * MOE_META + 3 * g + 2]

            def body(c, carry, dst0=dst0, lo=lo):
                off = pl.multiple_of(c * ch, ch)
                cp = _chunk_copy(y_ref.at[pl.ds(pl.multiple_of(dst0 + off, ch), ch)],
                                 ybuf.at[sl, pl.ds(pl.multiple_of(lo + off, ch), ch)], sem.at[sl])
                if start:
                    cp.start()
                else:
                    cp.wait()
                return carry

            lax.fori_loop(0, nch, body, 0)

    @pl.when(i == 0)
    def _():
        ybuf[...] = jnp.zeros_like(ybuf)
        run_copies(0, 0, True)

    @pl.when(i + 1 < n)
    def _():
        run_copies(i + 1, (i + 1) % 2, True)

    run_copies(i, i % 2, False)
    lane = lax.broadcasted_iota(jnp.int32, (MOE_TM, MOE_LMAX), 1)
    pmt = jnp.where(lane == lp_ref[...], 1.0, 0.0).astype(BF16)
    ffn = jnp.dot(pmt, ybuf[i % 2], preferred_element_type=F32)
    xn = x_ref[...] + mod_ref[0][5:6] * ffn
    if final:
        ms = jnp.mean(xn * xn, axis=-1, keepdims=True)
        xn = xn * lax.rsqrt(ms + EPS) * nw_ref[...]
    o_ref[...] = xn


def moe_combine(meta, x, lp, mod, y, nw, final, tpb):
    t, d = x.shape
    tm = MOE_TM
    grid_spec = pltpu.PrefetchScalarGridSpec(
        num_scalar_prefetch=1,
        grid=(t // tm,),
        in_specs=[
            pl.BlockSpec((tm, d), lambda i, m: (i, 0)),
            pl.BlockSpec((tm, 1), lambda i, m: (i, 0)),
            pl.BlockSpec((1, 6, d), lambda i, m: (i // tpb, 0, 0)),
            _resident((1, d)),
            pl.BlockSpec(memory_space=pl.ANY),
        ],
        out_specs=pl.BlockSpec((tm, d), lambda i, m: (i, 0)),
        scratch_shapes=[pltpu.VMEM((2, MOE_LMAX, d), BF16), pltpu.SemaphoreType.DMA((2,))],
    )
    return pl.pallas_call(
        functools.partial(_moe_combine_kernel, final=final),
        grid_spec=grid_spec,
        out_shape=jax.ShapeDtypeStruct((t, d), F32),
        compiler_params=_cparams(("arbitrary",)),
        name="moe_combine",
    )(meta, x, lp, mod, nw.reshape(1, d), y)


def _expert_schedule(tot, cap, nt_max):
    nt_g = tot[:MOE_GROUPS] // MOE_RT
    ends = jnp.cumsum(nt_g)
    starts = ends - nt_g
    nt = ends[MOE_GROUPS - 1]
    ic = jnp.minimum(jnp.arange(nt_max, dtype=jnp.int32), nt - 1)
    tg = jnp.sum((ic[:, None] >= ends[None, :]).astype(jnp.int32), axis=1)
    tr = tg * (cap // MOE_RT) + ic - starts[tg]
    return tg.astype(jnp.int32), tr.astype(jnp.int32), nt.reshape(1).astype(jnp.int32)


def _even_chunks(n):
    cw = 512
    return tuple((c, cw, ((0, cw, 0, None, c),)) for c in range(0, n, cw))


def _ssm_chunks():
    gw, n, ng = SSM_GROUP_WIDTH, SSM_D_STATE, SSM_GROUPS
    chunks = []
    col = 0
    for oi in (0, 1):
        for g in range(0, ng, 2):
            chunks.append((col, 2 * gw, ((0, gw, oi, g, 0), (gw, gw, oi, g + 1, 0))))
            col += 2 * gw
    for oi in (2, 3):
        for g in range(0, ng, 4):
            chunks.append((col, 4 * n, tuple((k * n, n, oi, g + k, 0) for k in range(4))))
            col += 4 * n
    chunks.append((col, LANES, ((0, LANES, 4, None, 0),)))
    return tuple(chunks), col + LANES


def _router_weights(w_group, b_group, w_expert, b_expert):
    d = w_group.shape[0]
    pad = LANES - MOE_EXPERTS - MOE_GROUPS
    wr = jnp.concatenate([w_expert.reshape(d, MOE_EXPERTS), w_group, jnp.zeros((d, pad), F32)], axis=1)
    br = jnp.concatenate([b_expert.reshape(MOE_EXPERTS), b_group, jnp.zeros((pad,), F32)]).reshape(1, LANES)
    return wr.astype(BF16), br


def kernel(x, c, ada_w, ada_b, norm1_w, norm2_w, mix_w_in, pool_w, pool_scale, rel_bias, mix_w_out, ssm_w_in, ssm_conv_w, ssm_conv_b, ssm_dt_bias, ssm_A_log, ssm_D, ssm_norm_w, ssm_w_out, moe_w_group, moe_b_group, moe_w_expert, moe_b_expert, moe_w1, moe_w3, moe_w2, final_norm_w):
    b, s, d = x.shape
    depth = ada_w.shape[0]
    tm = 512
    mod_all = adaln(c, ada_w, ada_b)
    ssm_chunks, ssm_cols = _ssm_chunks()
    n_tiles = (b * s) // MOE_TM
    cap = -(-(b * s + n_tiles * MOE_CH) // MOE_RT) * MOE_RT
    nt_max = (b * s + n_tiles * MOE_GROUPS * MOE_CH) // MOE_RT + MOE_GROUPS
    for layer in range(depth):
        mod = mod_all[layer].reshape(b, 6, d)
        i = layer // 2
        if layer % 2 == 0:
            w_in = mix_w_in[i].astype(BF16)
            (proj,) = norm_proj(x, norm1_w[layer], mod, w_in,
                                [jax.ShapeDtypeStruct((b, s, w_in.shape[1]), BF16)], _even_chunks(w_in.shape[1]), tm)
            a = even_mixer(proj, _attn_bias_tile(rel_bias[i]), pool_w[i].astype(BF16), pool_scale[i])
            mixed, w_out = a, mix_w_out[i].astype(BF16)
        else:
            w_in = jnp.pad(ssm_w_in[i], ((0, 0), (0, ssm_cols - ssm_w_in.shape[2]))).astype(BF16)
            ng, gw, n = SSM_GROUPS, SSM_GROUP_WIDTH, SSM_D_STATE
            shapes = [
                jax.ShapeDtypeStruct((b, ng, s, gw), BF16),
                jax.ShapeDtypeStruct((b, ng, s, gw), BF16),
                jax.ShapeDtypeStruct((b, ng, s, n), BF16),
                jax.ShapeDtypeStruct((b, ng, s, n), BF16),
                jax.ShapeDtypeStruct((b, s, LANES), F32),
            ]
            z, xs, bm, cm, dt = norm_proj(x, norm1_w[layer], mod, w_in, shapes, ssm_chunks, tm)
            g = ssd(z, xs, bm, cm, dt, ssm_conv_w[i], ssm_conv_b[i], ssm_dt_bias[i], ssm_A_log[i], ssm_D[i],
                    ssm_norm_w[i])
            mixed, w_out = g, ssm_w_out[i].astype(BF16)
        wr, br = _router_weights(moe_w_group[layer], moe_b_group[layer], moe_w_expert[layer], moe_b_expert[layer])
        gw = MOE_EPG * MOE_D_FF
        grp = lambda w: w.reshape(MOE_GROUPS, MOE_EPG, d, MOE_D_FF).transpose(0, 2, 1, 3).reshape(MOE_GROUPS, d, gw)
        w1 = grp(moe_w1[layer]).astype(BF16)
        w3 = grp(moe_w3[layer]).astype(BF16)
        w2 = moe_w2[layer].reshape(MOE_GROUPS, gw, d).astype(BF16)
        xk, lp, meta, tot, srt = out_proj_dispatch(mixed, w_out, x, mod, norm2_w[layer], wr, br, cap)
        tg, tr, nt = _expert_schedule(tot, cap, nt_max)
        y = moe_experts(tg, tr, nt, srt, wr, br, w1, w3, w2, nt_max)
        x = moe_combine(meta, xk, lp, mod, y, final_norm_w, layer == depth - 1, s // MOE_TM).reshape(b, s, d)
    return x
```

```python
import functools
import math

import jax
import jax.numpy as jnp
from jax import lax
from jax.experimental import pallas as pl
from jax.experimental.pallas import tpu as pltpu

F32 = jnp.float32
BF16 = jnp.bfloat16

EPS = 1e-6
LANES = 128

POOL_WINDOWS = (2, 4, 8, 16)
POOL_GROUP = 128
POOL_WIDTH = POOL_GROUP * len(POOL_WINDOWS)
POOL_HALO = 16
ATT_HEAD_DIM = 64
ATT_HEADS = 8
ATT_WIDTH = ATT_HEADS * ATT_HEAD_DIM
ATT_CHUNK = 64
ATT_PREV_CHUNKS = 8
ATT_TQ = 256
ATT_PREV = ATT_PREV_CHUNKS * ATT_CHUNK
ATT_TK = ATT_PREV + ATT_TQ
REL_CLIP = 256
NEG_BIG = -1e30

SSM_GROUPS = 8
SSM_HEADS = 32
SSM_HEADS_PER_GROUP = 4
SSM_HEAD_DIM = 64
SSM_D_STATE = 128
SSM_D_INNER = SSM_HEADS * SSM_HEAD_DIM
SSM_GROUP_WIDTH = SSM_HEADS_PER_GROUP * SSM_HEAD_DIM
SSM_D_CONV = 4
PROJ_HALO = 16
SSD_L = 256
LOG2E = math.log2(math.e)

MOE_GROUPS = 4
MOE_EPG = 8
MOE_EXPERTS = 32
MOE_D_FF = 128
MOE_TM = 512
MOE_CH = 32
MOE_LMAX = MOE_TM + MOE_GROUPS * MOE_CH
MOE_RT = 256
MOE_META = 3 * MOE_GROUPS

VMEM_LIMIT = 56 * 1024 * 1024


def _cparams(sem):
    return pltpu.CompilerParams(dimension_semantics=sem, vmem_limit_bytes=VMEM_LIMIT)


def _resident(shape):
    nd = len(shape)
    return pl.BlockSpec(shape, lambda *_: (0,) * nd, pipeline_mode=pl.Buffered(1))


def _silu(v):
    return v * jax.nn.sigmoid(v)


def _rms_mod(xv, nw, shift, scale):
    ms = jnp.mean(xv * xv, axis=-1, keepdims=True)
    y = xv * lax.rsqrt(ms + EPS) * nw
    return y * (1.0 + scale) + shift


def _adaln_kernel(c_ref, w_ref, b_ref, o_ref):
    c = c_ref[...]
    ca = _silu(c).astype(BF16)
    o_ref[0] = jnp.dot(ca, w_ref[0].astype(BF16), preferred_element_type=F32) + b_ref[0]


def adaln(c, ada_w, ada_b):
    depth, d, n = ada_w.shape
    b = c.shape[0]
    tn = 1536
    return pl.pallas_call(
        _adaln_kernel,
        grid=(depth, n // tn),
        in_specs=[
            pl.BlockSpec((b, d), lambda l, j: (0, 0)),
            pl.BlockSpec((1, d, tn), lambda l, j: (l, 0, j)),
            pl.BlockSpec((1, 1, tn), lambda l, j: (l, 0, j)),
        ],
        out_specs=pl.BlockSpec((1, b, tn), lambda l, j: (l, 0, j)),
        out_shape=jax.ShapeDtypeStruct((depth, b, n), F32),
        compiler_params=_cparams(("arbitrary", "arbitrary")),
        name="adaln",
    )(c, ada_w, ada_b.reshape(depth, 1, n))


def _norm_proj_kernel(x_ref, nw_ref, mod_ref, w_ref, *o_refs, chunks):
    m = mod_ref[0]
    h = _rms_mod(x_ref[0], nw_ref[...], m[0:1], m[1:2]).astype(BF16)
    for col, width, stores in chunks:
        r = jnp.dot(h, w_ref[:, col:col + width], preferred_element_type=F32)
        for off, w, oi, g, dst in stores:
            o_ref = o_refs[oi]
            v = r[:, off:off + w].astype(o_ref.dtype)
            if g is None:
                o_ref[0, :, dst:dst + w] = v
            else:
                o_ref[0, g, :, dst:dst + w] = v


def norm_proj(x, nw, mod, w, out_shapes, chunks, tm):
    b, s, d = x.shape
    out_specs = []
    for sh in out_shapes:
        if len(sh.shape) == 3:
            out_specs.append(pl.BlockSpec((1, tm, sh.shape[2]), lambda i, j: (i, j, 0)))
        else:
            out_specs.append(pl.BlockSpec((1, sh.shape[1], tm, sh.shape[3]), lambda i, j: (i, 0, j, 0)))
    return pl.pallas_call(
        functools.partial(_norm_proj_kernel, chunks=chunks),
        grid=(b, s // tm),
        in_specs=[
            pl.BlockSpec((1, tm, d), lambda i, j: (i, j, 0)),
            _resident((1, d)),
            pl.BlockSpec((1, 6, d), lambda i, j: (i, 0, 0)),
            _resident(w.shape),
        ],
        out_specs=out_specs,
        out_shape=out_shapes,
        compiler_params=_cparams(("parallel", "parallel")),
        name="norm_proj",
    )(x, nw.reshape(1, d), mod, w)


def _ssm_in_proj_kernel(x_ref, xh_ref, nw_ref, mod_ref, w_ref, cw_ref, cb_ref,
                        z_ref, xs_ref, b_ref, c_ref, dt_ref, *, chunks):
    assert SSM_D_CONV == 4
    j = pl.program_id(1)
    tm = x_ref.shape[1]
    hr = PROJ_HALO
    m = mod_ref[0]
    xe = jnp.concatenate([xh_ref[0], x_ref[0]], axis=0)
    he = _rms_mod(xe, nw_ref[...], m[0:1], m[1:2]).astype(BF16)
    outs = (z_ref, xs_ref, b_ref, c_ref, dt_ref)
    def project(chunk):
        col, width = chunk[0], chunk[1]
        return jnp.dot(he, w_ref[:, col:col + width], preferred_element_type=F32)

    r_next = project(chunks[0])
    for ci, (col, width, conv_col, stores) in enumerate(chunks):
        r = r_next
        if ci + 1 < len(chunks):
            r_next = project(chunks[ci + 1])
        if conv_col is None:
            res = r[hr:, :]
        else:
            r = jnp.concatenate([jnp.where(j > 0, r[0:hr, :], 0.0), r[hr:, :]], axis=0)
            cw = cw_ref[:, conv_col:conv_col + width]
            u = pltpu.roll(r, 1, 0)
            near = r[hr:, :] * cw[3:4] + u[hr:, :] * cw[2:3]
            far = pltpu.roll(r * cw[1:2] + u * cw[0:1], 2, 0)
            res = _silu(cb_ref[:, conv_col:conv_col + width] + near + far[hr:, :])
        for off, w, oi, g in stores:
            o_ref = outs[oi]
            v = res[:, off:off + w].astype(o_ref.dtype)
            if g is None:
                o_ref[0] = v
            else:
                o_ref[0, g] = v


def _ssm_in_proj_chunks():
    gw, n, ng, cw = SSM_GROUP_WIDTH, SSM_D_STATE, SSM_GROUPS, 512
    chunks = []
    col = 0
    for oi, conv0 in ((0, None), (1, 0)):
        for g in range(0, ng, 2):
            conv_col = None if conv0 is None else conv0 + g * gw
            chunks.append((col, cw, conv_col, ((0, gw, oi, g), (gw, gw, oi, g + 1))))
            col += cw
    for oi, conv0 in ((2, SSM_D_INNER), (3, SSM_D_INNER + ng * n)):
        for g in range(0, ng, 4):
            chunks.append((col, cw, conv0 + g * n, tuple((k * n, n, oi, g + k) for k in range(4))))
            col += cw
    chunks.append((col, LANES, None, ((0, LANES, 4, None),)))
    return tuple(chunks), col + LANES


def ssm_in_proj(x, nw, mod, w, conv_w, conv_b, tm):
    b, s, d = x.shape
    ng, gw, n = SSM_GROUPS, SSM_GROUP_WIDTH, SSM_D_STATE
    chunks, cols = _ssm_in_proj_chunks()
    w = jnp.pad(w, ((0, 0), (0, cols - w.shape[1]))).astype(BF16)
    hb = tm // PROJ_HALO
    tile4 = lambda wd: pl.BlockSpec((1, ng, tm, wd), lambda i, j: (i, 0, j, 0))
    return pl.pallas_call(
        functools.partial(_ssm_in_proj_kernel, chunks=chunks),
        grid=(b, s // tm),
        in_specs=[
            pl.BlockSpec((1, tm, d), lambda i, j: (i, j, 0)),
            pl.BlockSpec((1, PROJ_HALO, d), lambda i, j: (i, jnp.maximum(j * hb - 1, 0), 0)),
            _resident((1, d)),
            pl.BlockSpec((1, 6, d), lambda i, j: (i, 0, 0)),
            _resident(w.shape),
            _resident(conv_w.shape),
            _resident((1, conv_b.shape[0])),
        ],
        out_specs=[tile4(gw), tile4(gw), tile4(n), tile4(n), pl.BlockSpec((1, tm, LANES), lambda i, j: (i, j, 0))],
        out_shape=[
            jax.ShapeDtypeStruct((b, ng, s, gw), BF16),
            jax.ShapeDtypeStruct((b, ng, s, gw), BF16),
            jax.ShapeDtypeStruct((b, ng, s, n), BF16),
            jax.ShapeDtypeStruct((b, ng, s, n), BF16),
            jax.ShapeDtypeStruct((b, s, LANES), F32),
        ],
        compiler_params=_cparams(("parallel", "parallel")),
        name="ssm_in_proj",
    )(x, x, nw.reshape(1, d), mod, w, conv_w, conv_b.reshape(1, -1))


def _even_mixer_kernel(uq_ref, kv2_ref, kv1_ref, kv0_ref, halo_ref, bias_ref, pw_ref, ps_ref, o_ref, ubuf):
    j = pl.program_id(1)
    tq = ATT_TQ

    u = uq_ref[0, :, 0:POOL_WIDTH].astype(F32)
    halo = halo_ref[0].astype(F32)
    ubuf[0:POOL_HALO, :] = jnp.where(j > 0, halo, 0.0)
    ubuf[POOL_HALO:POOL_HALO + tq, :] = u
    pos1 = j * tq + 1 + lax.broadcasted_iota(jnp.int32, (tq, 1), 0)
    for g, win in enumerate(POOL_WINDOWS):
        lo = g * POOL_GROUP
        acc = ubuf[POOL_HALO:POOL_HALO + tq, lo:lo + POOL_GROUP]
        for k in range(1, win):
            acc = acc + ubuf[POOL_HALO - k:POOL_HALO - k + tq, lo:lo + POOL_GROUP]
        cnt = jnp.minimum(pos1, win).astype(F32)
        pooled = acc / cnt - u[:, lo:lo + POOL_GROUP]
        y = jnp.dot(pooled.astype(BF16), pw_ref[g], preferred_element_type=F32)
        o_ref[0, :, lo:lo + POOL_GROUP] = (y * ps_ref[:, lo:lo + POOL_GROUP]).astype(o_ref.dtype)

    kpos = j * tq - ATT_PREV + lax.broadcasted_iota(jnp.int32, (1, ATT_TK), 1)
    kvalid = kpos >= 0
    lane_hi = lax.broadcasted_iota(jnp.int32, (1, LANES), 1) >= ATT_HEAD_DIM
    scale = ATT_HEAD_DIM ** -0.5
    for hp in range(ATT_HEADS // 2):
        c0 = hp * LANES
        q2 = uq_ref[0, :, POOL_WIDTH + c0:POOL_WIDTH + c0 + LANES]
        k2 = jnp.concatenate([r[0, :, c0:c0 + LANES] for r in (kv2_ref, kv1_ref, kv0_ref)], axis=0)
        v2 = jnp.concatenate(
            [r[0, :, ATT_WIDTH + c0:ATT_WIDTH + c0 + LANES] for r in (kv2_ref, kv1_ref, kv0_ref)], axis=0)
        outs = []
        for jj in range(2):
            qm = jnp.where(lane_hi if jj else jnp.logical_not(lane_hi), q2, jnp.zeros_like(q2))
            sc = lax.dot_general(qm, k2, (((1,), (1,)), ((), ())), preferred_element_type=F32)
            sc = sc * scale + bias_ref[2 * hp + jj]
            sc = jnp.where(kvalid, sc, NEG_BIG)
            mx = jnp.max(sc, axis=-1, keepdims=True)
            p = jnp.exp(sc - mx)
            den = jnp.sum(p, axis=-1, keepdims=True)
            o = jnp.dot(p.astype(BF16), v2, preferred_element_type=F32)
            outs.append(o / den)
        o_pair = jnp.where(lane_hi, outs[1], outs[0])
        o_ref[0, :, POOL_WIDTH + c0:POOL_WIDTH + c0 + LANES] = o_pair.astype(o_ref.dtype)


def _attn_bias_tile(rel_bias):
    nh = rel_bias.shape[0]
    i = jnp.arange(ATT_TQ)[:, None]
    s = jnp.arange(ATT_TK)[None, :]
    n_const = ATT_PREV + ATT_TQ - REL_CLIP
    n_tail = ATT_TK + ATT_TQ - 1 - n_const
    v = jnp.concatenate([jnp.broadcast_to(rel_bias[:, 2 * REL_CLIP:], (nh, n_const)),
                         rel_bias[:, 2 * REL_CLIP - n_tail:2 * REL_CLIP][:, ::-1]], axis=1).astype(F32)
    bias = jnp.stack([v[:, ATT_TQ - 1 - r:ATT_TQ - 1 - r + ATT_TK] for r in range(ATT_TQ)], axis=1)
    band = s - (i // ATT_CHUNK) * ATT_CHUNK
    ok = (band >= 0) & (band < (ATT_PREV_CHUNKS + 1) * ATT_CHUNK)
    return jnp.where(ok[None], bias, NEG_BIG)


def even_mixer(proj, bias_tile, pool_w, pool_scale):
    b, s, n = proj.shape
    tq = ATT_TQ
    half = n // 2
    halo_blocks = tq // POOL_HALO
    return pl.pallas_call(
        _even_mixer_kernel,
        grid=(b, s // tq),
        in_specs=[
            pl.BlockSpec((1, tq, half), lambda i, j: (i, j, 0)),
            pl.BlockSpec((1, tq, half), lambda i, j: (i, jnp.maximum(j - 2, 0), 1)),
            pl.BlockSpec((1, tq, half), lambda i, j: (i, jnp.maximum(j - 1, 0), 1)),
            pl.BlockSpec((1, tq, half), lambda i, j: (i, j, 1)),
            pl.BlockSpec((1, POOL_HALO, POOL_WIDTH), lambda i, j: (i, jnp.maximum(j * halo_blocks - 1, 0), 0)),
            _resident(bias_tile.shape),
            _resident(pool_w.shape),
            _resident((1, POOL_WIDTH)),
        ],
        out_specs=pl.BlockSpec((1, tq, POOL_WIDTH + ATT_WIDTH), lambda i, j: (i, j, 0)),
        out_shape=jax.ShapeDtypeStruct((b, s, POOL_WIDTH + ATT_WIDTH), BF16),
        scratch_shapes=[pltpu.VMEM((POOL_HALO + tq, POOL_WIDTH), F32)],
        compiler_params=_cparams(("parallel", "parallel")),
        name="even_mixer",
    )(proj, proj, proj, proj, proj, bias_tile, pool_w, pool_scale.reshape(1, POOL_WIDTH))


def _group_logits(logits):
    lane = lax.broadcasted_iota(jnp.int32, logits.shape, 1)
    is_g = (lane >= MOE_EXPERTS) & (lane < MOE_EXPERTS + MOE_GROUPS)
    gl = jnp.where(is_g, logits, -jnp.inf)
    return lane, gl, jnp.max(gl, axis=-1, keepdims=True)


def _chunk_copy(src, dst, sem):
    return pltpu.make_async_copy(src, dst, sem)


def _out_proj_dispatch_kernel(a_ref, w_ref, x_ref, mod_ref, nw_ref, wr_ref, br_ref, ltri_ref,
                              xo_ref, lp_ref, meta_ref, tot_ref, sorted_ref,
                              sbuf, zbuf, sem, run_s, nis_s, *, grouped, cap):
    i = pl.program_id(0)
    n = pl.num_programs(0)
    slot = i % 2
    tm, ch = MOE_TM, MOE_CH

    @pl.when(i == 0)
    def _():
        for g in range(MOE_GROUPS):
            run_s[g] = 0
        nis_s[0] = 0
        nis_s[1] = 0
        zbuf[...] = jnp.zeros_like(zbuf)

    m = mod_ref[0]
    if grouped:
        ng, kw = a_ref.shape[1], a_ref.shape[3]
        mix = jnp.dot(a_ref[0, 0], w_ref[0:kw, :], preferred_element_type=F32)
        for g in range(1, ng):
            mix = mix + jnp.dot(a_ref[0, g], w_ref[g * kw:(g + 1) * kw, :], preferred_element_type=F32)
    else:
        mix = jnp.dot(a_ref[...], w_ref[...], preferred_element_type=F32)
    xn = x_ref[...] + m[2:3] * mix
    xo_ref[...] = xn
    h2 = _rms_mod(xn, nw_ref[...], m[3:4], m[4:5]).astype(BF16)

    logits = jnp.dot(h2, wr_ref[...], preferred_element_type=F32) + br_ref[...]
    lane, gl, gmax = _group_logits(logits)
    g_lane = jnp.min(jnp.where(gl == gmax, lane, 2 * LANES), axis=-1, keepdims=True)
    onehot = (lane == g_lane).astype(F32)
    rank = jnp.dot(ltri_ref[...], onehot.astype(BF16), preferred_element_type=F32)
    cnt_row = jnp.sum(onehot, axis=0, keepdims=True)
    lane1 = lax.broadcasted_iota(jnp.int32, (1, LANES), 1)
    nch, los = [], []
    lo = jnp.int32(0)
    lo_row = jnp.zeros((1, LANES), F32)
    for g in range(MOE_GROUPS):
        cnt = jnp.sum(jnp.where(lane1 == MOE_EXPERTS + g, cnt_row, 0.0)).astype(jnp.int32)
        nch.append((cnt + (ch - 1)) // ch)
        los.append(lo)
        lo_row = jnp.where(lane1 == MOE_EXPERTS + g, lo.astype(F32), lo_row)
        lo = lo + nch[g] * ch
    lp = jnp.sum(onehot * (rank + lo_row), axis=-1, keepdims=True)
    lp_ref[...] = lp.astype(jnp.int32)
    lpt = jnp.broadcast_to(lp, (tm, LANES)).T[0:1, :]
    rowi = lax.broadcasted_iota(jnp.int32, (MOE_LMAX, tm), 0).astype(F32)
    pm = jnp.where(rowi == lpt, 1.0, 0.0).astype(BF16)

    def _wait(sl, count):
        def body(c, carry):
            _chunk_copy(sbuf.at[sl, pl.ds(0, ch)], sorted_ref.at[pl.ds(0, ch)], sem.at[sl]).wait()
            return carry
        lax.fori_loop(0, count, body, 0)

    _wait(slot, nis_s[slot])
    sbuf[slot] = jnp.dot(pm, h2, preferred_element_type=F32).astype(BF16)
    total = jnp.int32(0)
    for g in range(MOE_GROUPS):
        dst0 = g * cap + run_s[g]

        def issue(c, carry, g=g, dst0=dst0):
            off = pl.multiple_of(c * ch, ch)
            _chunk_copy(sbuf.at[slot, pl.ds(pl.multiple_of(los[g] + off, ch), ch)],
                        sorted_ref.at[pl.ds(pl.multiple_of(dst0 + off, ch), ch)], sem.at[slot]).start()
            return carry

        lax.fori_loop(0, nch[g], issue, 0)
        meta_ref[i * MOE_META + 3 * g] = dst0
        meta_ref[i * MOE_META + 3 * g + 1] = nch[g]
        meta_ref[i * MOE_META + 3 * g + 2] = los[g]
        run_s[g] = run_s[g] + nch[g] * ch
        total = total + nch[g]
    nis_s[slot] = total

    @pl.when(i == n - 1)
    def _():
        for g in range(MOE_GROUPS):
            r = run_s[g]
            full = (r + (MOE_RT - 1)) // MOE_RT * MOE_RT
            nfill = (full - r) // ch
            base = g * cap + r

            def fill(c, carry, base=base):
                off = pl.multiple_of(c * ch, ch)
                _chunk_copy(zbuf, sorted_ref.at[pl.ds(pl.multiple_of(base + off, ch), ch)], sem.at[2]).start()
                return carry

            def fill_wait(c, carry):
                _chunk_copy(zbuf, sorted_ref.at[pl.ds(0, ch)], sem.at[2]).wait()
                return carry

            lax.fori_loop(0, nfill, fill, 0)
            lax.fori_loop(0, nfill, fill_wait, 0)
            tot_ref[g] = full
            tot_ref[MOE_GROUPS + g] = 0
        _wait(0, nis_s[0])
        _wait(1, nis_s[1])


def out_proj_dispatch(a, w, x, mod, nw2, wr, br, cap):
    b, s, d = x.shape
    t = b * s
    tm = MOE_TM
    tpb = s // tm
    nt = t // tm
    grouped = a.ndim == 4
    if grouped:
        a_spec = pl.BlockSpec((1, a.shape[1], tm, a.shape[3]), lambda i: (i // tpb, 0, i % tpb, 0))
    else:
        a = a.reshape(t, a.shape[2])
        a_spec = pl.BlockSpec((tm, a.shape[1]), lambda i: (i, 0))
    ltri = (jnp.arange(tm)[:, None] > jnp.arange(tm)[None, :]).astype(BF16)
    smem = pl.BlockSpec(memory_space=pltpu.SMEM)
    xo, lp, meta, tot, srt = pl.pallas_call(
        functools.partial(_out_proj_dispatch_kernel, grouped=grouped, cap=cap),
        grid=(nt,),
        in_specs=[
            a_spec,
            _resident(w.shape),
            pl.BlockSpec((tm, d), lambda i: (i, 0)),
            pl.BlockSpec((1, 6, d), lambda i: (i // tpb, 0, 0)),
            _resident((1, d)),
            _resident(wr.shape),
            _resident(br.shape),
            _resident(ltri.shape),
        ],
        out_specs=[
            pl.BlockSpec((tm, d), lambda i: (i, 0)),
            pl.BlockSpec((tm, 1), lambda i: (i, 0)),
            smem,
            smem,
            pl.BlockSpec(memory_space=pl.ANY),
        ],
        out_shape=[
            jax.ShapeDtypeStruct((t, d), F32),
            jax.ShapeDtypeStruct((t, 1), jnp.int32),
            jax.ShapeDtypeStruct((nt * MOE_META,), jnp.int32),
            jax.ShapeDtypeStruct((2 * MOE_GROUPS,), jnp.int32),
            jax.ShapeDtypeStruct((MOE_GROUPS * cap, d), BF16),
        ],
        scratch_shapes=[
            pltpu.VMEM((2, MOE_LMAX, d), BF16),
            pltpu.VMEM((MOE_CH, d), BF16),
            pltpu.SemaphoreType.DMA((3,)),
            pltpu.SMEM((MOE_GROUPS,), jnp.int32),
            pltpu.SMEM((2,), jnp.int32),
        ],
        compiler_params=_cparams(("arbitrary",)),
        name="out_proj_dispatch",
    )(a, w, x.reshape(t, d), mod, nw2.reshape(1, d), wr, br, ltri)
    return xo, lp, meta, tot, srt


def _ssd_kernel(z_ref, x_ref, b_ref, c_ref, dt_ref, dtb_ref, alog_ref, dskip_ref, nw_ref, o_ref,
                state, acsb, aspt, wt):
    j = pl.program_id(1)
    L = SSD_L

    @pl.when(j == 0)
    def _():
        state[...] = jnp.zeros_like(state)

    dt = jax.nn.softplus(dt_ref[0] + dtb_ref[...])
    a = dt * (-jnp.exp(alog_ref[...]) * LOG2E)
    row = lax.broadcasted_iota(jnp.int32, (L, L), 0)
    col = lax.broadcasted_iota(jnp.int32, (L, L), 1)
    tril = row >= col
    a_cs = jnp.dot(tril.astype(F32), a, preferred_element_type=F32, precision=lax.Precision.HIGHEST)
    a_last = a_cs[L - 1:L, :]
    aspt[...] = (a_cs - jnp.log2(dt)).T
    wt[...] = (dt * jnp.exp2(a_last - a_cs)).T
    for h in range(SSM_HEADS):
        acsb[h] = jnp.broadcast_to(a_cs[:, h:h + 1], (L, LANES))

    lane_hi = lax.broadcasted_iota(jnp.int32, (1, LANES), 1) >= SSM_HEAD_DIM

    def group_body(g, carry):
        xsb = x_ref[0, g]
        bmb = b_ref[0, g]
        cmb = c_ref[0, g]
        cb = lax.dot_general(cmb, bmb, (((1,), (1,)), ((), ())), preferred_element_type=F32)
        bmt = bmb.astype(F32).T
        st = state[g]
        yoff = jnp.dot(cmb, st.astype(BF16), preferred_element_type=F32)
        zz = z_ref[0, g].astype(F32)
        dsk = dskip_ref[g]
        gated = []
        for q in range(2):
            c0 = q * LANES
            xs2b = xsb[:, c0:c0 + LANES]
            ys, sns, cds = [], [], []
            for jj in range(2):
                h = g * SSM_HEADS_PER_GROUP + 2 * q + jj
                alb = acsb[h]
                asr = aspt[pl.ds(h, 1), :]
                wr = wt[pl.ds(h, 1), :]
                al2 = jnp.concatenate([alb] * (L // LANES), axis=1)
                seg = jnp.exp2(jnp.where(tril, al2 - asr, -jnp.inf))
                mr = (cb * seg).astype(BF16)
                yd = jnp.dot(mr, xs2b, preferred_element_type=F32)
                ys.append(yd + jnp.exp2(alb) * yoff[:, c0:c0 + LANES])
                lhs = (bmt * wr).astype(BF16)
                sns.append(jnp.dot(lhs, xs2b, preferred_element_type=F32))
                cds.append(jnp.exp2(alb[L - 1:L, :]))
            y = jnp.where(lane_hi, ys[1], ys[0]) + xs2b.astype(F32) * dsk[:, c0:c0 + LANES]
            sn = jnp.where(lane_hi, sns[1], sns[0])
            cd = jnp.where(lane_hi, cds[1], cds[0])
            state[g, :, c0:c0 + LANES] = st[:, c0:c0 + LANES] * cd + sn
            gated.append(y * _silu(zz[:, c0:c0 + LANES]))
        ss = jnp.sum(gated[0] * gated[0], axis=-1, keepdims=True) + jnp.sum(gated[1] * gated[1], axis=-1, keepdims=True)
        rs = lax.rsqrt(ss / SSM_GROUP_WIDTH + EPS)
        nw = nw_ref[g]
        for q in range(2):
            c0 = q * LANES
            o_ref[0, g, :, c0:c0 + LANES] = (gated[q] * rs * nw[:, c0:c0 + LANES]).astype(o_ref.dtype)
        return carry

    lax.fori_loop(0, SSM_GROUPS, group_body, 0)


def ssd(z, xs, bm, cm, dt, dt_bias, a_log, d_skip, norm_w):
    b, ng, s, gw = z.shape
    L = SSD_L
    n = SSM_D_STATE
    pad = LANES - SSM_HEADS
    dtb = jnp.pad(dt_bias, (0, pad)).reshape(1, LANES)
    alog = jnp.pad(a_log, (0, pad)).reshape(1, LANES)
    dsk = jnp.repeat(d_skip, SSM_HEAD_DIM).reshape(ng, 1, gw)
    nw = norm_w.reshape(ng, 1, gw)
    tile4 = lambda w: pl.BlockSpec((1, ng, L, w), lambda i, j: (i, 0, j, 0))
    return pl.pallas_call(
        _ssd_kernel,
        grid=(b, s // L),
        in_specs=[
            tile4(gw), tile4(gw), tile4(n), tile4(n),
            pl.BlockSpec((1, L, LANES), lambda i, j: (i, j, 0)),
            _resident(dtb.shape), _resident(alog.shape), _resident(dsk.shape), _resident(nw.shape),
        ],
        out_specs=tile4(gw),
        out_shape=jax.ShapeDtypeStruct((b, ng, s, gw), BF16),
        scratch_shapes=[
            pltpu.VMEM((ng, n, gw), F32),
            pltpu.VMEM((SSM_HEADS, L, LANES), F32),
            pltpu.VMEM((LANES, L), F32),
            pltpu.VMEM((LANES, L), F32),
        ],
        compiler_params=_cparams(("parallel", "arbitrary")),
        name="ssd",
    )(z, xs, bm, cm, dt, dtb, alog, dsk, nw)


def _route_in_group(logits, g):
    lane, gl, gmax = _group_logits(logits)
    gsum = jnp.sum(jnp.exp(gl - gmax), axis=-1, keepdims=True)
    lg = jnp.sum(jnp.where(lane == MOE_EXPERTS + g, logits, 0.0), axis=-1, keepdims=True)
    g_val = jnp.exp(lg - gmax) / gsum
    in_grp = (lane >= g * MOE_EPG) & (lane < (g + 1) * MOE_EPG)
    el = jnp.where(in_grp, logits, -jnp.inf)
    emax = jnp.max(el, axis=-1, keepdims=True)
    ee = jnp.exp(el - emax)
    p = ee / jnp.sum(ee, axis=-1, keepdims=True)
    p = jnp.where(in_grp, p, -1.0)
    m1 = jnp.max(p, axis=-1, keepdims=True)
    i1 = jnp.min(jnp.where(p == m1, lane, 2 * LANES), axis=-1, keepdims=True)
    p2 = jnp.where(lane == i1, -1.0, p)
    m2 = jnp.max(p2, axis=-1, keepdims=True)
    i2 = jnp.min(jnp.where(p2 == m2, lane, 2 * LANES), axis=-1, keepdims=True)
    tot = m1 + m2
    w1 = g_val * (m1 / tot)
    w2 = g_val * (m2 / tot)
    return lane, jnp.where(lane == i1, w1, 0.0) + jnp.where(lane == i2, w2, 0.0)


def _moe_expert_kernel(tg_ref, tr_ref, nt_ref, rows_ref, wr_ref, br_ref, w1_ref, w3_ref, w2_ref, y_ref):
    i = pl.program_id(0)

    @pl.when(i < nt_ref[0])
    def _():
        g = tg_ref[i]
        rows = rows_ref[...]
        logits = jnp.dot(rows, wr_ref[...], preferred_element_type=F32) + br_ref[...]
        lane, comb = _route_in_group(logits, g)
        a = jnp.dot(rows, w1_ref[0], preferred_element_type=F32)
        b = jnp.dot(rows, w3_ref[0], preferred_element_type=F32)
        hid = _silu(a) * b
        parts = []
        for q in range(MOE_EPG):
            cw = jnp.sum(jnp.where(lane == g * MOE_EPG + q, comb, 0.0), axis=-1, keepdims=True)
            parts.append((hid[:, q * MOE_D_FF:(q + 1) * MOE_D_FF] * cw).astype(BF16))
        hb = jnp.concatenate(parts, axis=1)
        y_ref[...] = jnp.dot(hb, w2_ref[0], preferred_element_type=F32).astype(y_ref.dtype)


def moe_experts(tg, tr, nt, srt, wr, br, w1, w3, w2, nt_max):
    rows, d = srt.shape
    gw = MOE_EPG * MOE_D_FF
    grid_spec = pltpu.PrefetchScalarGridSpec(
        num_scalar_prefetch=3,
        grid=(nt_max,),
        in_specs=[
            pl.BlockSpec((MOE_RT, d), lambda i, tg, tr, nt: (tr[i], 0)),
            _resident(wr.shape),
            _resident(br.shape),
            pl.BlockSpec((1, d, gw), lambda i, tg, tr, nt: (tg[i], 0, 0)),
            pl.BlockSpec((1, d, gw), lambda i, tg, tr, nt: (tg[i], 0, 0)),
            pl.BlockSpec((1, gw, d), lambda i, tg, tr, nt: (tg[i], 0, 0)),
        ],
        out_specs=pl.BlockSpec((MOE_RT, d), lambda i, tg, tr, nt: (tr[i], 0)),
    )
    return pl.pallas_call(
        _moe_expert_kernel,
        grid_spec=grid_spec,
        out_shape=jax.ShapeDtypeStruct((rows, d), BF16),
        compiler_params=_cparams(("arbitrary",)),
        name="moe_experts",
    )(tg, tr, nt, srt, wr, br, w1, w3, w2)


def _moe_combine_kernel(meta_ref, x_ref, lp_ref, mod_ref, nw_ref, y_ref, o_ref, ybuf, sem, *, final):
    i = pl.program_id(0)
    n = pl.num_programs(0)
    ch = MOE_CH

    def run_copies(tile, sl, start):
        for g in range(MOE_GROUPS):
            dst0 = meta_ref[tile * MOE_META + 3 * g]
            nch = meta_ref[tile * MOE_META + 3 * g + 1]
            lo = meta_ref[tile * MOE_META + 3 * g + 2]

            def body(c, carry, dst0=dst0, lo=lo):
                off = pl.multiple_of(c * ch, ch)
                cp = _chunk_copy(y_ref.at[pl.ds(pl.multiple_of(dst0 + off, ch), ch)],
                                 ybuf.at[sl, pl.ds(pl.multiple_of(lo + off, ch), ch)], sem.at[sl])
                if start:
                    cp.start()
                else:
                    cp.wait()
                return carry

            lax.fori_loop(0, nch, body, 0)

    @pl.when(i == 0)
    def _():
        ybuf[...] = jnp.zeros_like(ybuf)
        run_copies(0, 0, True)

    @pl.when(i + 1 < n)
    def _():
        run_copies(i + 1, (i + 1) % 2, True)

    run_copies(i, i % 2, False)
    lane = lax.broadcasted_iota(jnp.int32, (MOE_TM, MOE_LMAX), 1)
    pmt = jnp.where(lane == lp_ref[...], 1.0, 0.0).astype(BF16)
    ffn = jnp.dot(pmt, ybuf[i % 2], preferred_element_type=F32)
    xn = x_ref[...] + mod_ref[0][5:6] * ffn
    if final:
        ms = jnp.mean(xn * xn, axis=-1, keepdims=True)
        xn = xn * lax.rsqrt(ms + EPS) * nw_ref[...]
    o_ref[...] = xn


def moe_combine(meta, x, lp, mod, y, nw, final, tpb):
    t, d = x.shape
    tm = MOE_TM
    grid_spec = pltpu.PrefetchScalarGridSpec(
        num_scalar_prefetch=1,
        grid=(t // tm,),
        in_specs=[
            pl.BlockSpec((tm, d), lambda i, m: (i, 0)),
            pl.BlockSpec((tm, 1), lambda i, m: (i, 0)),
            pl.BlockSpec((1, 6, d), lambda i, m: (i // tpb, 0, 0)),
            _resident((1, d)),
            pl.BlockSpec(memory_space=pl.ANY),
        ],
        out_specs=pl.BlockSpec((tm, d), lambda i, m: (i, 0)),
        scratch_shapes=[pltpu.VMEM((2, MOE_LMAX, d), BF16), pltpu.SemaphoreType.DMA((2,))],
    )
    return pl.pallas_call(
        functools.partial(_moe_combine_kernel, final=final),
        grid_spec=grid_spec,
        out_shape=jax.ShapeDtypeStruct((t, d), F32),
        compiler_params=_cparams(("arbitrary",)),
        name="moe_combine",
    )(meta, x, lp, mod, nw.reshape(1, d), y)


def _expert_schedule(tot, cap, nt_max):
    nt_g = tot[:MOE_GROUPS] // MOE_RT
    ends = jnp.cumsum(nt_g)
    starts = ends - nt_g
    nt = ends[MOE_GROUPS - 1]
    ic = jnp.minimum(jnp.arange(nt_max, dtype=jnp.int32), nt - 1)
    tg = jnp.sum((ic[:, None] >= ends[None, :]).astype(jnp.int32), axis=1)
    tr = tg * (cap // MOE_RT) + ic - starts[tg]
    return tg.astype(jnp.int32), tr.astype(jnp.int32), nt.reshape(1).astype(jnp.int32)


def _even_chunks(n):
    cw = 512
    return tuple((c, cw, ((0, cw, 0, None, c),)) for c in range(0, n, cw))


def _router_weights(w_group, b_group, w_expert, b_expert):
    d = w_group.shape[0]
    pad = LANES - MOE_EXPERTS - MOE_GROUPS
    wr = jnp.concatenate([w_expert.reshape(d, MOE_EXPERTS), w_group, jnp.zeros((d, pad), F32)], axis=1)
    br = jnp.concatenate([b_expert.reshape(MOE_EXPERTS), b_group, jnp.zeros((pad,), F32)]).reshape(1, LANES)
    return wr.astype(BF16), br


def kernel(x, c, ada_w, ada_b, norm1_w, norm2_w, mix_w_in, pool_w, pool_scale, rel_bias, mix_w_out, ssm_w_in, ssm_conv_w, ssm_conv_b, ssm_dt_bias, ssm_A_log, ssm_D, ssm_norm_w, ssm_w_out, moe_w_group, moe_b_group, moe_w_expert, moe_b_expert, moe_w1, moe_w3, moe_w2, final_norm_w):
    b, s, d = x.shape
    depth = ada_w.shape[0]
    tm = 512
    mod_all = adaln(c, ada_w, ada_b)
    n_tiles = (b * s) // MOE_TM
    cap = -(-(b * s + n_tiles * MOE_CH) // MOE_RT) * MOE_RT
    nt_max = (b * s + n_tiles * MOE_GROUPS * MOE_CH) // MOE_RT + MOE_GROUPS
    for layer in range(depth):
        mod = mod_all[layer].reshape(b, 6, d)
        i = layer // 2
        if layer % 2 == 0:
            w_in = mix_w_in[i].astype(BF16)
            (proj,) = norm_proj(x, norm1_w[layer], mod, w_in,
                                [jax.ShapeDtypeStruct((b, s, w_in.shape[1]), BF16)], _even_chunks(w_in.shape[1]), tm)
            a = even_mixer(proj, _attn_bias_tile(rel_bias[i]), pool_w[i].astype(BF16), pool_scale[i])
            mixed, w_out = a, mix_w_out[i].astype(BF16)
        else:
            z, xs, bm, cm, dt = ssm_in_proj(x, norm1_w[layer], mod, ssm_w_in[i], ssm_conv_w[i], ssm_conv_b[i], tm)
            g = ssd(z, xs, bm, cm, dt, ssm_dt_bias[i], ssm_A_log[i], ssm_D[i], ssm_norm_w[i])
            mixed, w_out = g, ssm_w_out[i].astype(BF16)
        wr, br = _router_weights(moe_w_group[layer], moe_b_group[layer], moe_w_expert[layer], moe_b_expert[layer])
        gw = MOE_EPG * MOE_D_FF
        grp = lambda w: w.reshape(MOE_GROUPS, MOE_EPG, d, MOE_D_FF).transpose(0, 2, 1, 3).reshape(MOE_GROUPS, d, gw)
        w1 = grp(moe_w1[layer]).astype(BF16)
        w3 = grp(moe_w3[layer]).astype(BF16)
        w2 = moe_w2[layer].reshape(MOE_GROUPS, gw, d).astype(BF16)
        xk, lp, meta, tot, srt = out_proj_dispatch(mixed, w_out, x, mod, norm2_w[layer], wr, br, cap)
        tg, tr, nt = _expert_schedule(tot, cap, nt_max)
        y = moe_experts(tg, tr, nt, srt, wr, br, w1, w3, w2, nt_max)
        x = moe_combine(meta, xk, lp, mod, y, final_norm_w, layer == depth - 1, s // MOE_TM).reshape(b, s, d)
    return x
```

```python
import functools
import math

import jax
import jax.numpy as jnp
from jax import lax
from jax.experimental import pallas as pl
from jax.experimental.pallas import tpu as pltpu

F32 = jnp.float32
BF16 = jnp.bfloat16

EPS = 1e-6
LANES = 128

POOL_WINDOWS = (2, 4, 8, 16)
POOL_GROUP = 128
POOL_WIDTH = POOL_GROUP * len(POOL_WINDOWS)
POOL_HALO = 16
ATT_HEAD_DIM = 64
ATT_HEADS = 8
ATT_WIDTH = ATT_HEADS * ATT_HEAD_DIM
ATT_CHUNK = 64
ATT_PREV_CHUNKS = 8
ATT_TQ = 256
ATT_PREV = ATT_PREV_CHUNKS * ATT_CHUNK
ATT_TK = ATT_PREV + ATT_TQ
REL_CLIP = 256
NEG_BIG = -1e30

SSM_GROUPS = 8
SSM_HEADS = 32
SSM_HEADS_PER_GROUP = 4
SSM_HEAD_DIM = 64
SSM_D_STATE = 128
SSM_D_INNER = SSM_HEADS * SSM_HEAD_DIM
SSM_GROUP_WIDTH = SSM_HEADS_PER_GROUP * SSM_HEAD_DIM
SSM_D_CONV = 4
PROJ_HALO = 16
SSD_L = 256
LOG2E = math.log2(math.e)

MOE_GROUPS = 4
MOE_EPG = 8
MOE_EXPERTS = 32
MOE_D_FF = 128
MOE_TM = 512
MOE_CH = 32
MOE_LMAX = MOE_TM + MOE_GROUPS * MOE_CH
MOE_RT = 256
MOE_META = 3 * MOE_GROUPS

VMEM_LIMIT = 56 * 1024 * 1024


def _cparams(sem):
    return pltpu.CompilerParams(dimension_semantics=sem, vmem_limit_bytes=VMEM_LIMIT)


def _resident(shape):
    nd = len(shape)
    return pl.BlockSpec(shape, lambda *_: (0,) * nd, pipeline_mode=pl.Buffered(1))


def _silu(v):
    return v * jax.nn.sigmoid(v)


def _rms_mod(xv, nw, shift, scale):
    ms = jnp.mean(xv * xv, axis=-1, keepdims=True)
    y = xv * lax.rsqrt(ms + EPS) * nw
    return y * (1.0 + scale) + shift


def _adaln_kernel(c_ref, w_ref, b_ref, o_ref):
    c = c_ref[...]
    ca = _silu(c).astype(BF16)
    o_ref[0] = jnp.dot(ca, w_ref[0].astype(BF16), preferred_element_type=F32) + b_ref[0]


def adaln(c, ada_w, ada_b):
    depth, d, n = ada_w.shape
    b = c.shape[0]
    tn = 1536
    return pl.pallas_call(
        _adaln_kernel,
        grid=(depth, n // tn),
        in_specs=[
            pl.BlockSpec((b, d), lambda l, j: (0, 0)),
            pl.BlockSpec((1, d, tn), lambda l, j: (l, 0, j)),
            pl.BlockSpec((1, 1, tn), lambda l, j: (l, 0, j)),
        ],
        out_specs=pl.BlockSpec((1, b, tn), lambda l, j: (l, 0, j)),
        out_shape=jax.ShapeDtypeStruct((depth, b, n), F32),
        compiler_params=_cparams(("arbitrary", "arbitrary")),
        name="adaln",
    )(c, ada_w, ada_b.reshape(depth, 1, n))


def _norm_proj_kernel(x_ref, nw_ref, mod_ref, w_ref, *o_refs, chunks):
    m = mod_ref[0]
    h = _rms_mod(x_ref[0], nw_ref[...], m[0:1], m[1:2]).astype(BF16)
    for col, width, stores in chunks:
        r = jnp.dot(h, w_ref[:, col:col + width], preferred_element_type=F32)
        for off, w, oi, g, dst in stores:
            o_ref = o_refs[oi]
            v = r[:, off:off + w].astype(o_ref.dtype)
            if g is None:
                o_ref[0, :, dst:dst + w] = v
            else:
                o_ref[0, g, :, dst:dst + w] = v


def norm_proj(x, nw, mod, w, out_shapes, chunks, tm):
    b, s, d = x.shape
    out_specs = []
    for sh in out_shapes:
        if len(sh.shape) == 3:
            out_specs.append(pl.BlockSpec((1, tm, sh.shape[2]), lambda i, j: (i, j, 0)))
        else:
            out_specs.append(pl.BlockSpec((1, sh.shape[1], tm, sh.shape[3]), lambda i, j: (i, 0, j, 0)))
    return pl.pallas_call(
        functools.partial(_norm_proj_kernel, chunks=chunks),
        grid=(b, s // tm),
        in_specs=[
            pl.BlockSpec((1, tm, d), lambda i, j: (i, j, 0)),
            _resident((1, d)),
            pl.BlockSpec((1, 6, d), lambda i, j: (i, 0, 0)),
            _resident(w.shape),
        ],
        out_specs=out_specs,
        out_shape=out_shapes,
        compiler_params=_cparams(("parallel", "parallel")),
        name="norm_proj",
    )(x, nw.reshape(1, d), mod, w)


def _ssm_in_proj_kernel(x_ref, nw_ref, mod_ref, w_ref, cw_ref, cb_ref,
                        z_ref, xs_ref, b_ref, c_ref, dt_ref, halo, *, chunks):
    assert SSM_D_CONV == 4
    j = pl.program_id(1)
    tm = x_ref.shape[1]
    hr = PROJ_HALO
    m = mod_ref[0]
    h = _rms_mod(x_ref[0], nw_ref[...], m[0:1], m[1:2]).astype(BF16)
    outs = (z_ref, xs_ref, b_ref, c_ref, dt_ref)
    for col, width, conv_col, stores in chunks:
        r = jnp.dot(h, w_ref[:, col:col + width], preferred_element_type=F32)
        if conv_col is None:
            res = r
        else:
            prev = jnp.where(j > 0, halo[:, conv_col:conv_col + width], 0.0)
            halo[:, conv_col:conv_col + width] = r[tm - hr:, :]
            re = jnp.concatenate([prev, r], axis=0)
            cw = cw_ref[:, conv_col:conv_col + width]
            u = pltpu.roll(re, 1, 0)
            near = r * cw[3:4] + u[hr:, :] * cw[2:3]
            far = pltpu.roll(re * cw[1:2] + u * cw[0:1], 2, 0)
            res = _silu(cb_ref[:, conv_col:conv_col + width] + near + far[hr:, :])
        for off, w, oi, g in stores:
            o_ref = outs[oi]
            v = res[:, off:off + w].astype(o_ref.dtype)
            if g is None:
                o_ref[0] = v
            else:
                o_ref[0, g] = v


def _ssm_in_proj_chunks():
    gw, n, ng, cw = SSM_GROUP_WIDTH, SSM_D_STATE, SSM_GROUPS, 512
    chunks = []
    col = 0
    for oi, conv0 in ((0, None), (1, 0)):
        for g in range(0, ng, 2):
            conv_col = None if conv0 is None else conv0 + g * gw
            chunks.append((col, cw, conv_col, ((0, gw, oi, g), (gw, gw, oi, g + 1))))
            col += cw
    for oi, conv0 in ((2, SSM_D_INNER), (3, SSM_D_INNER + ng * n)):
        for g in range(0, ng, 4):
            chunks.append((col, cw, conv0 + g * n, tuple((k * n, n, oi, g + k) for k in range(4))))
            col += cw
    chunks.append((col, LANES, None, ((0, LANES, 4, None),)))
    return tuple(chunks), col + LANES


def ssm_in_proj(x, nw, mod, w, conv_w, conv_b, tm):
    b, s, d = x.shape
    ng, gw, n = SSM_GROUPS, SSM_GROUP_WIDTH, SSM_D_STATE
    chunks, cols = _ssm_in_proj_chunks()
    w = jnp.pad(w, ((0, 0), (0, cols - w.shape[1]))).astype(BF16)
    tile4 = lambda wd: pl.BlockSpec((1, ng, tm, wd), lambda i, j: (i, 0, j, 0))
    return pl.pallas_call(
        functools.partial(_ssm_in_proj_kernel, chunks=chunks),
        grid=(b, s // tm),
        in_specs=[
            pl.BlockSpec((1, tm, d), lambda i, j: (i, j, 0)),
            _resident((1, d)),
            pl.BlockSpec((1, 6, d), lambda i, j: (i, 0, 0)),
            _resident(w.shape),
            _resident(conv_w.shape),
            _resident((1, conv_b.shape[0])),
        ],
        out_specs=[tile4(gw), tile4(gw), tile4(n), tile4(n), pl.BlockSpec((1, tm, LANES), lambda i, j: (i, j, 0))],
        out_shape=[
            jax.ShapeDtypeStruct((b, ng, s, gw), BF16),
            jax.ShapeDtypeStruct((b, ng, s, gw), BF16),
            jax.ShapeDtypeStruct((b, ng, s, n), BF16),
            jax.ShapeDtypeStruct((b, ng, s, n), BF16),
            jax.ShapeDtypeStruct((b, s, LANES), F32),
        ],
        scratch_shapes=[pltpu.VMEM((PROJ_HALO, conv_w.shape[1]), F32)],
        compiler_params=_cparams(("parallel", "arbitrary")),
        name="ssm_in_proj",
    )(x, nw.reshape(1, d), mod, w, conv_w, conv_b.reshape(1, -1))


def _even_mixer_kernel(uq_ref, kv2_ref, kv1_ref, kv0_ref, halo_ref, bias_ref, pw_ref, ps_ref, o_ref, ubuf):
    j = pl.program_id(1)
    tq = ATT_TQ

    u = uq_ref[0, :, 0:POOL_WIDTH].astype(F32)
    halo = halo_ref[0].astype(F32)
    ubuf[0:POOL_HALO, :] = jnp.where(j > 0, halo, 0.0)
    ubuf[POOL_HALO:POOL_HALO + tq, :] = u
    pos1 = j * tq + 1 + lax.broadcasted_iota(jnp.int32, (tq, 1), 0)
    for g, win in enumerate(POOL_WINDOWS):
        lo = g * POOL_GROUP
        acc = ubuf[POOL_HALO:POOL_HALO + tq, lo:lo + POOL_GROUP]
        for k in range(1, win):
            acc = acc + ubuf[POOL_HALO - k:POOL_HALO - k + tq, lo:lo + POOL_GROUP]
        cnt = jnp.minimum(pos1, win).astype(F32)
        pooled = acc / cnt - u[:, lo:lo + POOL_GROUP]
        y = jnp.dot(pooled.astype(BF16), pw_ref[g], preferred_element_type=F32)
        o_ref[0, :, lo:lo + POOL_GROUP] = (y * ps_ref[:, lo:lo + POOL_GROUP]).astype(o_ref.dtype)

    kpos = j * tq - ATT_PREV + lax.broadcasted_iota(jnp.int32, (1, ATT_TK), 1)
    kvalid = kpos >= 0
    lane_hi = lax.broadcasted_iota(jnp.int32, (1, LANES), 1) >= ATT_HEAD_DIM
    for hp in range(ATT_HEADS // 2):
        c0 = hp * LANES
        q2 = uq_ref[0, :, POOL_WIDTH + c0:POOL_WIDTH + c0 + LANES]
        k2 = jnp.concatenate([r[0, :, c0:c0 + LANES] for r in (kv2_ref, kv1_ref, kv0_ref)], axis=0)
        v2 = jnp.concatenate(
            [r[0, :, ATT_WIDTH + c0:ATT_WIDTH + c0 + LANES] for r in (kv2_ref, kv1_ref, kv0_ref)], axis=0)
        outs = []
        for jj in range(2):
            qm = jnp.where(lane_hi if jj else jnp.logical_not(lane_hi), q2, jnp.zeros_like(q2))
            sc = lax.dot_general(qm, k2, (((1,), (1,)), ((), ())), preferred_element_type=F32)
            sc = jnp.where(kvalid, sc + bias_ref[2 * hp + jj], NEG_BIG)
            mx = jnp.max(sc, axis=-1, keepdims=True)
            p = jnp.exp2(sc - mx)
            den = jnp.sum(p, axis=-1, keepdims=True)
            o = jnp.dot(p.astype(BF16), v2, preferred_element_type=F32)
            outs.append(o / den)
        o_pair = jnp.where(lane_hi, outs[1], outs[0])
        o_ref[0, :, POOL_WIDTH + c0:POOL_WIDTH + c0 + LANES] = o_pair.astype(o_ref.dtype)


def _attn_bias_tile(rel_bias):
    nh = rel_bias.shape[0]
    i = jnp.arange(ATT_TQ)[:, None]
    s = jnp.arange(ATT_TK)[None, :]
    n_const = ATT_PREV + ATT_TQ - REL_CLIP
    n_tail = ATT_TK + ATT_TQ - 1 - n_const
    v = jnp.concatenate([jnp.broadcast_to(rel_bias[:, 2 * REL_CLIP:], (nh, n_const)),
                         rel_bias[:, 2 * REL_CLIP - n_tail:2 * REL_CLIP][:, ::-1]], axis=1).astype(F32)
    starts = ATT_TQ - 1 - jnp.arange(ATT_TQ)
    rows = jax.vmap(lambda st: lax.dynamic_slice_in_dim(v, st, ATT_TK, axis=1))(starts)
    bias = rows.transpose(1, 0, 2) * LOG2E
    band = s - (i // ATT_CHUNK) * ATT_CHUNK
    ok = (band >= 0) & (band < (ATT_PREV_CHUNKS + 1) * ATT_CHUNK)
    return jnp.where(ok[None], bias, NEG_BIG)


def even_mixer(proj, bias_tile, pool_w, pool_scale):
    b, s, n = proj.shape
    tq = ATT_TQ
    half = n // 2
    halo_blocks = tq // POOL_HALO
    return pl.pallas_call(
        _even_mixer_kernel,
        grid=(b, s // tq),
        in_specs=[
            pl.BlockSpec((1, tq, half), lambda i, j: (i, j, 0)),
            pl.BlockSpec((1, tq, half), lambda i, j: (i, jnp.maximum(j - 2, 0), 1)),
            pl.BlockSpec((1, tq, half), lambda i, j: (i, jnp.maximum(j - 1, 0), 1)),
            pl.BlockSpec((1, tq, half), lambda i, j: (i, j, 1)),
            pl.BlockSpec((1, POOL_HALO, POOL_WIDTH), lambda i, j: (i, jnp.maximum(j * halo_blocks - 1, 0), 0)),
            _resident(bias_tile.shape),
            _resident(pool_w.shape),
            _resident((1, POOL_WIDTH)),
        ],
        out_specs=pl.BlockSpec((1, tq, POOL_WIDTH + ATT_WIDTH), lambda i, j: (i, j, 0)),
        out_shape=jax.ShapeDtypeStruct((b, s, POOL_WIDTH + ATT_WIDTH), BF16),
        scratch_shapes=[pltpu.VMEM((POOL_HALO + tq, POOL_WIDTH), F32)],
        compiler_params=_cparams(("parallel", "parallel")),
        name="even_mixer",
    )(proj, proj, proj, proj, proj, bias_tile, pool_w, pool_scale.reshape(1, POOL_WIDTH))


def _group_logits(logits):
    lane = lax.broadcasted_iota(jnp.int32, logits.shape, 1)
    is_g = (lane >= MOE_EXPERTS) & (lane < MOE_EXPERTS + MOE_GROUPS)
    gl = jnp.where(is_g, logits, -jnp.inf)
    return lane, gl, jnp.max(gl, axis=-1, keepdims=True)


def _chunk_copy(src, dst, sem):
    return pltpu.make_async_copy(src, dst, sem)


def _out_proj_dispatch_kernel(a_ref, w_ref, x_ref, mod_ref, nw_ref, wr_ref, br_ref, ltri_ref,
                              xo_ref, lp_ref, meta_ref, tot_ref, sorted_ref,
                              sbuf, zbuf, sem, run_s, nis_s, *, grouped, cap):
    i = pl.program_id(0)
    n = pl.num_programs(0)
    slot = i % 2
    tm, ch = MOE_TM, MOE_CH

    @pl.when(i == 0)
    def _():
        for g in range(MOE_GROUPS):
            run_s[g] = 0
        nis_s[0] = 0
        nis_s[1] = 0
        zbuf[...] = jnp.zeros_like(zbuf)

    m = mod_ref[0]
    if grouped:
        ng, kw = a_ref.shape[1], a_ref.shape[3]
        mix = jnp.dot(a_ref[0, 0], w_ref[0:kw, :], preferred_element_type=F32)
        for g in range(1, ng):
            mix = mix + jnp.dot(a_ref[0, g], w_ref[g * kw:(g + 1) * kw, :], preferred_element_type=F32)
    else:
        mix = jnp.dot(a_ref[...], w_ref[...], preferred_element_type=F32)
    xn = x_ref[...] + m[2:3] * mix
    xo_ref[...] = xn
    h2 = _rms_mod(xn, nw_ref[...], m[3:4], m[4:5]).astype(BF16)

    logits = jnp.dot(h2, wr_ref[...], preferred_element_type=F32) + br_ref[...]
    lane, gl, gmax = _group_logits(logits)
    g_lane = jnp.min(jnp.where(gl == gmax, lane, 2 * LANES), axis=-1, keepdims=True)
    onehot = (lane == g_lane).astype(F32)
    rank = jnp.dot(ltri_ref[...], onehot.astype(BF16), preferred_element_type=F32)
    cnt_row = jnp.sum(onehot, axis=0, keepdims=True)
    lane1 = lax.broadcasted_iota(jnp.int32, (1, LANES), 1)
    nch, los = [], []
    lo = jnp.int32(0)
    lo_row = jnp.zeros((1, LANES), F32)
    for g in range(MOE_GROUPS):
        cnt = jnp.sum(jnp.where(lane1 == MOE_EXPERTS + g, cnt_row, 0.0)).astype(jnp.int32)
        nch.append((cnt + (ch - 1)) // ch)
        los.append(lo)
        lo_row = jnp.where(lane1 == MOE_EXPERTS + g, lo.astype(F32), lo_row)
        lo = lo + nch[g] * ch
    lp = jnp.sum(onehot * (rank + lo_row), axis=-1, keepdims=True)
    lp_ref[...] = lp.astype(jnp.int32)
    lpt = jnp.broadcast_to(lp, (tm, LANES)).T[0:1, :]
    rowi = lax.broadcasted_iota(jnp.int32, (MOE_LMAX, tm), 0).astype(F32)
    pm = jnp.where(rowi == lpt, 1.0, 0.0).astype(BF16)

    def _wait(sl, count):
        def body(c, carry):
            _chunk_copy(sbuf.at[sl, pl.ds(0, ch)], sorted_ref.at[pl.ds(0, ch)], sem.at[sl]).wait()
            return carry
        lax.fori_loop(0, count, body, 0)

    _wait(slot, nis_s[slot])
    sbuf[slot] = jnp.dot(pm, h2, preferred_element_type=F32).astype(BF16)
    total = jnp.int32(0)
    for g in range(MOE_GROUPS):
        dst0 = g * cap + run_s[g]

        def issue(c, carry, g=g, dst0=dst0):
            off = pl.multiple_of(c * ch, ch)
            _chunk_copy(sbuf.at[slot, pl.ds(pl.multiple_of(los[g] + off, ch), ch)],
                        sorted_ref.at[pl.ds(pl.multiple_of(dst0 + off, ch), ch)], sem.at[slot]).start()
            return carry

        lax.fori_loop(0, nch[g], issue, 0)
        meta_ref[i * MOE_META + 3 * g] = dst0
        meta_ref[i * MOE_META + 3 * g + 1] = nch[g]
        meta_ref[i * MOE_META + 3 * g + 2] = los[g]
        run_s[g] = run_s[g] + nch[g] * ch
        total = total + nch[g]
    nis_s[slot] = total

    @pl.when(i == n - 1)
    def _():
        for g in range(MOE_GROUPS):
            r = run_s[g]
            full = (r + (MOE_RT - 1)) // MOE_RT * MOE_RT
            nfill = (full - r) // ch
            base = g * cap + r

            def fill(c, carry, base=base):
                off = pl.multiple_of(c * ch, ch)
                _chunk_copy(zbuf, sorted_ref.at[pl.ds(pl.multiple_of(base + off, ch), ch)], sem.at[2]).start()
                return carry

            def fill_wait(c, carry):
                _chunk_copy(zbuf, sorted_ref.at[pl.ds(0, ch)], sem.at[2]).wait()
                return carry

            lax.fori_loop(0, nfill, fill, 0)
            lax.fori_loop(0, nfill, fill_wait, 0)
            tot_ref[g] = full
            tot_ref[MOE_GROUPS + g] = 0
        _wait(0, nis_s[0])
        _wait(1, nis_s[1])


def out_proj_dispatch(a, w, x, mod, nw2, wr, br, cap):
    b, s, d = x.shape
    t = b * s
    tm = MOE_TM
    tpb = s // tm
    nt = t // tm
    grouped = a.ndim == 4
    if grouped:
        a_spec = pl.BlockSpec((1, a.shape[1], tm, a.shape[3]), lambda i: (i // tpb, 0, i % tpb, 0))
    else:
        a = a.reshape(t, a.shape[2])
        a_spec = pl.BlockSpec((tm, a.shape[1]), lambda i: (i, 0))
    ltri = (jnp.arange(tm)[:, None] > jnp.arange(tm)[None, :]).astype(BF16)
    smem = pl.BlockSpec(memory_space=pltpu.SMEM)
    xo, lp, meta, tot, srt = pl.pallas_call(
        functools.partial(_out_proj_dispatch_kernel, grouped=grouped, cap=cap),
        grid=(nt,),
        in_specs=[
            a_spec,
            _resident(w.shape),
            pl.BlockSpec((tm, d), lambda i: (i, 0)),
            pl.BlockSpec((1, 6, d), lambda i: (i // tpb, 0, 0)),
            _resident((1, d)),
            _resident(wr.shape),
            _resident(br.shape),
            _resident(ltri.shape),
        ],
        out_specs=[
            pl.BlockSpec((tm, d), lambda i: (i, 0)),
            pl.BlockSpec((tm, 1), lambda i: (i, 0)),
            smem,
            smem,
            pl.BlockSpec(memory_space=pl.ANY),
        ],
        out_shape=[
            jax.ShapeDtypeStruct((t, d), F32),
            jax.ShapeDtypeStruct((t, 1), jnp.int32),
            jax.ShapeDtypeStruct((nt * MOE_META,), jnp.int32),
            jax.ShapeDtypeStruct((2 * MOE_GROUPS,), jnp.int32),
            jax.ShapeDtypeStruct((MOE_GROUPS * cap, d), BF16),
        ],
        scratch_shapes=[
            pltpu.VMEM((2, MOE_LMAX, d), BF16),
            pltpu.VMEM((MOE_CH, d), BF16),
            pltpu.SemaphoreType.DMA((3,)),
            pltpu.SMEM((MOE_GROUPS,), jnp.int32),
            pltpu.SMEM((2,), jnp.int32),
        ],
        compiler_params=_cparams(("arbitrary",)),
        name="out_proj_dispatch",
    )(a, w, x.reshape(t, d), mod, nw2.reshape(1, d), wr, br, ltri)
    return xo, lp, meta, tot, srt


def _ssd_kernel(z_ref, x_ref, b_ref, c_ref, dt_ref, dtb_ref, alog_ref, dskip_ref, nw_ref, o_ref,
                state, acsg, aspt, wt):
    j = pl.program_id(1)
    L = SSD_L

    @pl.when(j == 0)
    def _():
        state[...] = jnp.zeros_like(state)

    dt = jax.nn.softplus(dt_ref[0] + dtb_ref[...])
    a = dt * (-jnp.exp(alog_ref[...]) * LOG2E)
    row = lax.broadcasted_iota(jnp.int32, (L, L), 0)
    col = lax.broadcasted_iota(jnp.int32, (L, L), 1)
    tril = row >= col
    a_cs = jnp.dot(tril.astype(F32), a, preferred_element_type=F32, precision=lax.Precision.HIGHEST)
    a_last = a_cs[L - 1:L, :]
    aspt[...] = (a_cs - jnp.log2(dt)).T
    wt[...] = (dt * jnp.exp2(a_last - a_cs)).T
    for gi in range(SSM_GROUPS):
        shift = (LANES - gi * SSM_HEADS_PER_GROUP) % LANES
        acsg[gi] = pltpu.roll(a_cs, shift, 1) if shift else a_cs

    lane_hi = lax.broadcasted_iota(jnp.int32, (1, LANES), 1) >= SSM_HEAD_DIM

    def group_body(g, carry):
        xsb = x_ref[0, g]
        bmb = b_ref[0, g]
        cmb = c_ref[0, g]
        cb = lax.dot_general(cmb, bmb, (((1,), (1,)), ((), ())), preferred_element_type=F32)
        bmt = bmb.astype(F32).T
        st = state[g]
        yoff = jnp.dot(cmb, st.astype(BF16), preferred_element_type=F32)
        zz = z_ref[0, g].astype(F32)
        dsk = dskip_ref[g]
        acs_g = acsg[g]
        gated = []
        for q in range(2):
            c0 = q * LANES
            xs2b = xsb[:, c0:c0 + LANES]
            ys, sns, cds = [], [], []
            for jj in range(2):
                h = g * SSM_HEADS_PER_GROUP + 2 * q + jj
                r = 2 * q + jj
                alb = jnp.broadcast_to(acs_g[:, r:r + 1], (L, LANES))
                asr = aspt[pl.ds(h, 1), :]
                wr = wt[pl.ds(h, 1), :]
                al2 = jnp.concatenate([alb] * (L // LANES), axis=1)
                seg = jnp.exp2(jnp.where(tril, al2 - asr, -jnp.inf))
                mr = (cb * seg).astype(BF16)
                yd = jnp.dot(mr, xs2b, preferred_element_type=F32)
                ys.append(yd + jnp.exp2(alb) * yoff[:, c0:c0 + LANES])
                lhs = (bmt * wr).astype(BF16)
                sns.append(jnp.dot(lhs, xs2b, preferred_element_type=F32))
                cds.append(jnp.exp2(alb[L - 1:L, :]))
            y = jnp.where(lane_hi, ys[1], ys[0]) + xs2b.astype(F32) * dsk[:, c0:c0 + LANES]
            sn = jnp.where(lane_hi, sns[1], sns[0])
            cd = jnp.where(lane_hi, cds[1], cds[0])
            state[g, :, c0:c0 + LANES] = st[:, c0:c0 + LANES] * cd + sn
            gated.append(y * _silu(zz[:, c0:c0 + LANES]))
        ss = jnp.sum(gated[0] * gated[0], axis=-1, keepdims=True) + jnp.sum(gated[1] * gated[1], axis=-1, keepdims=True)
        rs = lax.rsqrt(ss / SSM_GROUP_WIDTH + EPS)
        nw = nw_ref[g]
        for q in range(2):
            c0 = q * LANES
            o_ref[0, g, :, c0:c0 + LANES] = (gated[q] * rs * nw[:, c0:c0 + LANES]).astype(o_ref.dtype)
        return carry

    lax.fori_loop(0, SSM_GROUPS, group_body, 0)


def ssd(z, xs, bm, cm, dt, dt_bias, a_log, d_skip, norm_w):
    b, ng, s, gw = z.shape
    L = SSD_L
    n = SSM_D_STATE
    pad = LANES - SSM_HEADS
    dtb = jnp.pad(dt_bias, (0, pad)).reshape(1, LANES)
    alog = jnp.pad(a_log, (0, pad)).reshape(1, LANES)
    dsk = jnp.repeat(d_skip, SSM_HEAD_DIM).reshape(ng, 1, gw)
    nw = norm_w.reshape(ng, 1, gw)
    tile4 = lambda w: pl.BlockSpec((1, ng, L, w), lambda i, j: (i, 0, j, 0))
    return pl.pallas_call(
        _ssd_kernel,
        grid=(b, s // L),
        in_specs=[
            tile4(gw), tile4(gw), tile4(n), tile4(n),
            pl.BlockSpec((1, L, LANES), lambda i, j: (i, j, 0)),
            _resident(dtb.shape), _resident(alog.shape), _resident(dsk.shape), _resident(nw.shape),
        ],
        out_specs=tile4(gw),
        out_shape=jax.ShapeDtypeStruct((b, ng, s, gw), BF16),
        scratch_shapes=[
            pltpu.VMEM((ng, n, gw), F32),
            pltpu.VMEM((ng, L, LANES), F32),
            pltpu.VMEM((LANES, L), F32),
            pltpu.VMEM((LANES, L), F32),
        ],
        compiler_params=_cparams(("parallel", "arbitrary")),
        name="ssd",
    )(z, xs, bm, cm, dt, dtb, alog, dsk, nw)


def _route_in_group(logits, g):
    lane, gl, gmax = _group_logits(logits)
    gsum = jnp.sum(jnp.exp(gl - gmax), axis=-1, keepdims=True)
    lg = jnp.sum(jnp.where(lane == MOE_EXPERTS + g, logits, 0.0), axis=-1, keepdims=True)
    g_val = jnp.exp(lg - gmax) / gsum
    in_grp = (lane >= g * MOE_EPG) & (lane < (g + 1) * MOE_EPG)
    el = jnp.where(in_grp, logits, -jnp.inf)
    emax = jnp.max(el, axis=-1, keepdims=True)
    ee = jnp.exp(el - emax)
    p = ee / jnp.sum(ee, axis=-1, keepdims=True)
    p = jnp.where(in_grp, p, -1.0)
    m1 = jnp.max(p, axis=-1, keepdims=True)
    i1 = jnp.min(jnp.where(p == m1, lane, 2 * LANES), axis=-1, keepdims=True)
    p2 = jnp.where(lane == i1, -1.0, p)
    m2 = jnp.max(p2, axis=-1, keepdims=True)
    i2 = jnp.min(jnp.where(p2 == m2, lane, 2 * LANES), axis=-1, keepdims=True)
    tot = m1 + m2
    w1 = g_val * (m1 / tot)
    w2 = g_val * (m2 / tot)
    return lane, jnp.where(lane == i1, w1, 0.0) + jnp.where(lane == i2, w2, 0.0)


def _moe_expert_kernel(tg_ref, tr_ref, nt_ref, rows_ref, wr_ref, br_ref, w1_ref, w3_ref, w2_ref, y_ref):
    i = pl.program_id(0)

    @pl.when(i < nt_ref[0])
    def _():
        g = tg_ref[i]
        rows = rows_ref[...]
        logits = jnp.dot(rows, wr_ref[...], preferred_element_type=F32) + br_ref[...]
        lane, comb = _route_in_group(logits, g)
        a = jnp.dot(rows, w1_ref[0], preferred_element_type=F32)
        b = jnp.dot(rows, w3_ref[0], preferred_element_type=F32)
        hid = _silu(a) * b
        parts = []
        for q in range(MOE_EPG):
            cw = jnp.sum(jnp.where(lane == g * MOE_EPG + q, comb, 0.0), axis=-1, keepdims=True)
            parts.append((hid[:, q * MOE_D_FF:(q + 1) * MOE_D_FF] * cw).astype(BF16))
        hb = jnp.concatenate(parts, axis=1)
        y_ref[...] = jnp.dot(hb, w2_ref[0], preferred_element_type=F32).astype(y_ref.dtype)


def moe_experts(tg, tr, nt, srt, wr, br, w1, w3, w2, nt_max):
    rows, d = srt.shape
    gw = MOE_EPG * MOE_D_FF
    grid_spec = pltpu.PrefetchScalarGridSpec(
        num_scalar_prefetch=3,
        grid=(nt_max,),
        in_specs=[
            pl.BlockSpec((MOE_RT, d), lambda i, tg, tr, nt: (tr[i], 0)),
            _resident(wr.shape),
            _resident(br.shape),
            pl.BlockSpec((1, d, gw), lambda i, tg, tr, nt: (tg[i], 0, 0)),
            pl.BlockSpec((1, d, gw), lambda i, tg, tr, nt: (tg[i], 0, 0)),
            pl.BlockSpec((1, gw, d), lambda i, tg, tr, nt: (tg[i], 0, 0)),
        ],
        out_specs=pl.BlockSpec((MOE_RT, d), lambda i, tg, tr, nt: (tr[i], 0)),
    )
    return pl.pallas_call(
        _moe_expert_kernel,
        grid_spec=grid_spec,
        out_shape=jax.ShapeDtypeStruct((rows, d), BF16),
        compiler_params=_cparams(("arbitrary",)),
        name="moe_experts",
    )(tg, tr, nt, srt, wr, br, w1, w3, w2)


def _moe_combine_kernel(meta_ref, x_ref, lp_ref, mod_ref, nw_ref, y_ref, o_ref, ybuf, sem, *, final):
    i = pl.program_id(0)
    n = pl.num_programs(0)
    ch = MOE_CH

    def run_copies(tile, sl, start):
        for g in range(MOE_GROUPS):
            dst0 = meta_ref[tile * MOE_META + 3 * g]
            nch = meta_ref[tile * MOE_META + 3 * g + 1]
            lo = meta_ref[tile * MOE_META + 3 * g + 2]

            def body(c, carry, dst0=dst0, lo=lo):
                off = pl.multiple_of(c * ch, ch)
                cp = _chunk_copy(y_ref.at[pl.ds(pl.multiple_of(dst0 + off, ch), ch)],
                                 ybuf.at[sl, pl.ds(pl.multiple_of(lo + off, ch), ch)], sem.at[sl])
                if start:
                    cp.start()
                else:
                    cp.wait()
                return carry

            lax.fori_loop(0, nch, body, 0)

    @pl.when(i == 0)
    def _():
        ybuf[...] = jnp.zeros_like(ybuf)
        run_copies(0, 0, True)

    @pl.when(i + 1 < n)
    def _():
        run_copies(i + 1, (i + 1) % 2, True)

    run_copies(i, i % 2, False)
    lane = lax.broadcasted_iota(jnp.int32, (MOE_TM, MOE_LMAX), 1)
    pmt = jnp.where(lane == lp_ref[...], 1.0, 0.0).astype(BF16)
    ffn = jnp.dot(pmt, ybuf[i % 2], preferred_element_type=F32)
    xn = x_ref[...] + mod_ref[0][5:6] * ffn
    if final:
        ms = jnp.mean(xn * xn, axis=-1, keepdims=True)
        xn = xn * lax.rsqrt(ms + EPS) * nw_ref[...]
    o_ref[...] = xn


def moe_combine(meta, x, lp, mod, y, nw, final, tpb):
    t, d = x.shape
    tm = MOE_TM
    grid_spec = pltpu.PrefetchScalarGridSpec(
        num_scalar_prefetch=1,
        grid=(t // tm,),
        in_specs=[
            pl.BlockSpec((tm, d), lambda i, m: (i, 0)),
            pl.BlockSpec((tm, 1), lambda i, m: (i, 0)),
            pl.BlockSpec((1, 6, d), lambda i, m: (i // tpb, 0, 0)),
            _resident((1, d)),
            pl.BlockSpec(memory_space=pl.ANY),
        ],
        out_specs=pl.BlockSpec((tm, d), lambda i, m: (i, 0)),
        scratch_shapes=[pltpu.VMEM((2, MOE_LMAX, d), BF16), pltpu.SemaphoreType.DMA((2,))],
    )
    return pl.pallas_call(
        functools.partial(_moe_combine_kernel, final=final),
        grid_spec=grid_spec,
        out_shape=jax.ShapeDtypeStruct((t, d), F32),
        compiler_params=_cparams(("arbitrary",)),
        name="moe_combine",
    )(meta, x, lp, mod, nw.reshape(1, d), y)


def _expert_schedule(tot, cap, nt_max):
    nt_g = tot[:MOE_GROUPS] // MOE_RT
    ends = jnp.cumsum(nt_g)
    starts = ends - nt_g
    nt = ends[MOE_GROUPS - 1]
    ic = jnp.minimum(jnp.arange(nt_max, dtype=jnp.int32), nt - 1)
    tg = jnp.sum((ic[:, None] >= ends[None, :]).astype(jnp.int32), axis=1)
    tr = tg * (cap // MOE_RT) + ic - starts[tg]
    return tg.astype(jnp.int32), tr.astype(jnp.int32), nt.reshape(1).astype(jnp.int32)


def _even_chunks(n):
    cw = 512
    return tuple((c, cw, ((0, cw, 0, None, c),)) for c in range(0, n, cw))


def _router_weights(w_group, b_group, w_expert, b_expert):
    d = w_group.shape[0]
    pad = LANES - MOE_EXPERTS - MOE_GROUPS
    wr = jnp.concatenate([w_expert.reshape(d, MOE_EXPERTS), w_group, jnp.zeros((d, pad), F32)], axis=1)
    br = jnp.concatenate([b_expert.reshape(MOE_EXPERTS), b_group, jnp.zeros((pad,), F32)]).reshape(1, LANES)
    return wr.astype(BF16), br


def kernel(x, c, ada_w, ada_b, norm1_w, norm2_w, mix_w_in, pool_w, pool_scale, rel_bias, mix_w_out, ssm_w_in, ssm_conv_w, ssm_conv_b, ssm_dt_bias, ssm_A_log, ssm_D, ssm_norm_w, ssm_w_out, moe_w_group, moe_b_group, moe_w_expert, moe_b_expert, moe_w1, moe_w3, moe_w2, final_norm_w):
    b, s, d = x.shape
    depth = ada_w.shape[0]
    tm = 512
    mod_all = adaln(c, ada_w, ada_b)
    n_tiles = (b * s) // MOE_TM
    cap = -(-(b * s + n_tiles * MOE_CH) // MOE_RT) * MOE_RT
    nt_max = (b * s + n_tiles * MOE_GROUPS * MOE_CH) // MOE_RT + MOE_GROUPS
    for layer in range(depth):
        mod = mod_all[layer].reshape(b, 6, d)
        i = layer // 2
        if layer % 2 == 0:
            col = jnp.arange(mix_w_in.shape[2])
            q_cols = (col >= POOL_WIDTH) & (col < POOL_WIDTH + ATT_WIDTH)
            w_in = (mix_w_in[i] * jnp.where(q_cols, ATT_HEAD_DIM ** -0.5 * LOG2E, 1.0)).astype(BF16)
            (proj,) = norm_proj(x, norm1_w[layer], mod, w_in,
                                [jax.ShapeDtypeStruct((b, s, w_in.shape[1]), BF16)], _even_chunks(w_in.shape[1]), tm)
            a = even_mixer(proj, _attn_bias_tile(rel_bias[i]), pool_w[i].astype(BF16), pool_scale[i])
            mixed, w_out = a, mix_w_out[i].astype(BF16)
        else:
            z, xs, bm, cm, dt = ssm_in_proj(x, norm1_w[layer], mod, ssm_w_in[i], ssm_conv_w[i], ssm_conv_b[i], tm)
            g = ssd(z, xs, bm, cm, dt, ssm_dt_bias[i], ssm_A_log[i], ssm_D[i], ssm_norm_w[i])
            mixed, w_out = g, ssm_w_out[i].astype(BF16)
        wr, br = _router_weights(moe_w_group[layer], moe_b_group[layer], moe_w_expert[layer], moe_b_expert[layer])
        gw = MOE_EPG * MOE_D_FF
        grp = lambda w: w.reshape(MOE_GROUPS, MOE_EPG, d, MOE_D_FF).transpose(0, 2, 1, 3).reshape(MOE_GROUPS, d, gw)
        w1 = grp(moe_w1[layer]).astype(BF16)
        w3 = grp(moe_w3[layer]).astype(BF16)
        w2 = moe_w2[layer].reshape(MOE_GROUPS, gw, d).astype(BF16)
        xk, lp, meta, tot, srt = out_proj_dispatch(mixed, w_out, x, mod, norm2_w[layer], wr, br, cap)
        tg, tr, nt = _expert_schedule(tot, cap, nt_max)
        y = moe_experts(tg, tr, nt, srt, wr, br, w1, w3, w2, nt_max)
        x = moe_combine(meta, xk, lp, mod, y, final_norm_w, layer == depth - 1, s // MOE_TM).reshape(b, s, d)
    return x
```

```python
import functools
import math

import jax
import jax.numpy as jnp
from jax import lax
from jax.experimental import pallas as pl
from jax.experimental.pallas import tpu as pltpu

F32 = jnp.float32
BF16 = jnp.bfloat16

EPS = 1e-6
LANES = 128

POOL_WINDOWS = (2, 4, 8, 16)
POOL_GROUP = 128
POOL_WIDTH = POOL_GROUP * len(POOL_WINDOWS)
POOL_HALO = 16
ATT_HEAD_DIM = 64
ATT_HEADS = 8
ATT_WIDTH = ATT_HEADS * ATT_HEAD_DIM
ATT_CHUNK = 64
ATT_PREV_CHUNKS = 8
ATT_TQ = 256
ATT_PREV = ATT_PREV_CHUNKS * ATT_CHUNK
ATT_TK = ATT_PREV + ATT_TQ
REL_CLIP = 256
NEG_BIG = -1e30

SSM_GROUPS = 8
SSM_HEADS = 32
SSM_HEADS_PER_GROUP = 4
SSM_HEAD_DIM = 64
SSM_D_STATE = 128
SSM_D_INNER = SSM_HEADS * SSM_HEAD_DIM
SSM_GROUP_WIDTH = SSM_HEADS_PER_GROUP * SSM_HEAD_DIM
SSM_D_CONV = 4
PROJ_HALO = 16
SSD_L = 256
LOG2E = math.log2(math.e)

MOE_GROUPS = 4
MOE_EPG = 8
MOE_EXPERTS = 32
MOE_D_FF = 128
MOE_TM = 512
MOE_CH = 32
MOE_LMAX = MOE_TM + MOE_GROUPS * MOE_CH
MOE_RT = 256
MOE_META = 3 * MOE_GROUPS

VMEM_LIMIT = 56 * 1024 * 1024


def _cparams(sem):
    return pltpu.CompilerParams(dimension_semantics=sem, vmem_limit_bytes=VMEM_LIMIT)


def _resident(shape):
    nd = len(shape)
    return pl.BlockSpec(shape, lambda *_: (0,) * nd, pipeline_mode=pl.Buffered(1))


def _silu(v):
    return v * jax.nn.sigmoid(v)


def _rms_mod(xv, nw, shift, scale):
    ms = jnp.mean(xv * xv, axis=-1, keepdims=True)
    y = xv * lax.rsqrt(ms + EPS) * nw
    return y * (1.0 + scale) + shift


def _adaln_kernel(c_ref, w_ref, b_ref, o_ref):
    c = c_ref[...]
    ca = _silu(c).astype(BF16)
    o_ref[0] = jnp.dot(ca, w_ref[0].astype(BF16), preferred_element_type=F32) + b_ref[0]


def adaln(c, ada_w, ada_b):
    depth, d, n = ada_w.shape
    b = c.shape[0]
    tn = 1536
    return pl.pallas_call(
        _adaln_kernel,
        grid=(depth, n // tn),
        in_specs=[
            pl.BlockSpec((b, d), lambda l, j: (0, 0)),
            pl.BlockSpec((1, d, tn), lambda l, j: (l, 0, j)),
            pl.BlockSpec((1, 1, tn), lambda l, j: (l, 0, j)),
        ],
        out_specs=pl.BlockSpec((1, b, tn), lambda l, j: (l, 0, j)),
        out_shape=jax.ShapeDtypeStruct((depth, b, n), F32),
        compiler_params=_cparams(("arbitrary", "arbitrary")),
        name="adaln",
    )(c, ada_w, ada_b.reshape(depth, 1, n))


def _norm_proj_kernel(x_ref, nw_ref, mod_ref, w_ref, *o_refs, chunks):
    m = mod_ref[0]
    h = _rms_mod(x_ref[0], nw_ref[...], m[0:1], m[1:2]).astype(BF16)
    for col, width, stores in chunks:
        r = jnp.dot(h, w_ref[:, col:col + width], preferred_element_type=F32)
        for off, w, oi, g, dst in stores:
            o_ref = o_refs[oi]
            v = r[:, off:off + w].astype(o_ref.dtype)
            if g is None:
                o_ref[0, :, dst:dst + w] = v
            else:
                o_ref[0, g, :, dst:dst + w] = v


def norm_proj(x, nw, mod, w, out_shapes, chunks, tm):
    b, s, d = x.shape
    out_specs = []
    for sh in out_shapes:
        if len(sh.shape) == 3:
            out_specs.append(pl.BlockSpec((1, tm, sh.shape[2]), lambda i, j: (i, j, 0)))
        else:
            out_specs.append(pl.BlockSpec((1, sh.shape[1], tm, sh.shape[3]), lambda i, j: (i, 0, j, 0)))
    return pl.pallas_call(
        functools.partial(_norm_proj_kernel, chunks=chunks),
        grid=(b, s // tm),
        in_specs=[
            pl.BlockSpec((1, tm, d), lambda i, j: (i, j, 0)),
            _resident((1, d)),
            pl.BlockSpec((1, 6, d), lambda i, j: (i, 0, 0)),
            _resident(w.shape),
        ],
        out_specs=out_specs,
        out_shape=out_shapes,
        compiler_params=_cparams(("parallel", "parallel")),
        name="norm_proj",
    )(x, nw.reshape(1, d), mod, w)


def _ssm_in_proj_kernel(x_ref, nw_ref, mod_ref, w_ref, cw_ref, cb_ref,
                        z_ref, xs_ref, b_ref, c_ref, dt_ref, halo, *, chunks):
    assert SSM_D_CONV == 4
    j = pl.program_id(1)
    tm = x_ref.shape[1]
    hr = PROJ_HALO
    m = mod_ref[0]
    h = _rms_mod(x_ref[0], nw_ref[...], m[0:1], m[1:2]).astype(BF16)
    outs = (z_ref, xs_ref, b_ref, c_ref, dt_ref)
    for col, width, conv_col, stores in chunks:
        r = jnp.dot(h, w_ref[:, col:col + width], preferred_element_type=F32)
        if conv_col is None:
            res = r
        else:
            prev = jnp.where(j > 0, halo[:, conv_col:conv_col + width], 0.0)
            halo[:, conv_col:conv_col + width] = r[tm - hr:, :]
            re = jnp.concatenate([prev, r], axis=0)
            cw = cw_ref[:, conv_col:conv_col + width]
            u = pltpu.roll(re, 1, 0)
            near = r * cw[3:4] + u[hr:, :] * cw[2:3]
            far = pltpu.roll(re * cw[1:2] + u * cw[0:1], 2, 0)
            res = _silu(cb_ref[:, conv_col:conv_col + width] + near + far[hr:, :])
        for off, w, oi, g in stores:
            o_ref = outs[oi]
            v = res[:, off:off + w].astype(o_ref.dtype)
            if g is None:
                o_ref[0] = v
            else:
                o_ref[0, g] = v


def _ssm_in_proj_chunks():
    gw, n, ng, cw = SSM_GROUP_WIDTH, SSM_D_STATE, SSM_GROUPS, 512
    chunks = []
    col = 0
    for oi, conv0 in ((0, None), (1, 0)):
        for g in range(0, ng, 2):
            conv_col = None if conv0 is None else conv0 + g * gw
            chunks.append((col, cw, conv_col, ((0, gw, oi, g), (gw, gw, oi, g + 1))))
            col += cw
    for oi, conv0 in ((2, SSM_D_INNER), (3, SSM_D_INNER + ng * n)):
        for g in range(0, ng, 4):
            chunks.append((col, cw, conv0 + g * n, tuple((k * n, n, oi, g + k) for k in range(4))))
            col += cw
    chunks.append((col, LANES, None, ((0, LANES, 4, None),)))
    return tuple(chunks), col + LANES


def ssm_in_proj(x, nw, mod, w, conv_w, conv_b, tm):
    b, s, d = x.shape
    ng, gw, n = SSM_GROUPS, SSM_GROUP_WIDTH, SSM_D_STATE
    chunks, cols = _ssm_in_proj_chunks()
    w = jnp.pad(w, ((0, 0), (0, cols - w.shape[1]))).astype(BF16)
    tile4 = lambda wd: pl.BlockSpec((1, ng, tm, wd), lambda i, j: (i, 0, j, 0))
    return pl.pallas_call(
        functools.partial(_ssm_in_proj_kernel, chunks=chunks),
        grid=(b, s // tm),
        in_specs=[
            pl.BlockSpec((1, tm, d), lambda i, j: (i, j, 0)),
            _resident((1, d)),
            pl.BlockSpec((1, 6, d), lambda i, j: (i, 0, 0)),
            _resident(w.shape),
            _resident(conv_w.shape),
            _resident((1, conv_b.shape[0])),
        ],
        out_specs=[tile4(gw), tile4(gw), tile4(n), tile4(n), pl.BlockSpec((1, tm, LANES), lambda i, j: (i, j, 0))],
        out_shape=[
            jax.ShapeDtypeStruct((b, ng, s, gw), BF16),
            jax.ShapeDtypeStruct((b, ng, s, gw), BF16),
            jax.ShapeDtypeStruct((b, ng, s, n), BF16),
            jax.ShapeDtypeStruct((b, ng, s, n), BF16),
            jax.ShapeDtypeStruct((b, s, LANES), F32),
        ],
        scratch_shapes=[pltpu.VMEM((PROJ_HALO, conv_w.shape[1]), F32)],
        compiler_params=_cparams(("parallel", "arbitrary")),
        name="ssm_in_proj",
    )(x, nw.reshape(1, d), mod, w, conv_w, conv_b.reshape(1, -1))


def _even_mixer_kernel(uq_ref, kv2_ref, kv1_ref, kv0_ref, halo_ref, bias_ref, pw_ref, ps_ref, o_ref, ubuf):
    j = pl.program_id(1)
    tq = ATT_TQ

    u = uq_ref[0, :, 0:POOL_WIDTH].astype(F32)
    halo = halo_ref[0].astype(F32)
    ubuf[0:POOL_HALO, :] = jnp.where(j > 0, halo, 0.0)
    ubuf[POOL_HALO:POOL_HALO + tq, :] = u
    pos1 = j * tq + 1 + lax.broadcasted_iota(jnp.int32, (tq, 1), 0)
    for g, win in enumerate(POOL_WINDOWS):
        lo = g * POOL_GROUP
        assert win & (win - 1) == 0 and win <= POOL_HALO
        acc = ubuf[:, lo:lo + POOL_GROUP]
        k = 1
        while k < win:
            acc = acc + pltpu.roll(acc, k, 0)
            k *= 2
        acc = acc[POOL_HALO:, :]
        cnt = jnp.minimum(pos1, win).astype(F32)
        pooled = acc / cnt - u[:, lo:lo + POOL_GROUP]
        y = jnp.dot(pooled.astype(BF16), pw_ref[g], preferred_element_type=F32)
        o_ref[0, :, lo:lo + POOL_GROUP] = (y * ps_ref[:, lo:lo + POOL_GROUP]).astype(o_ref.dtype)

    kpos = j * tq - ATT_PREV + lax.broadcasted_iota(jnp.int32, (1, ATT_TK), 1)
    kvalid = kpos >= 0
    lane_hi = lax.broadcasted_iota(jnp.int32, (1, LANES), 1) >= ATT_HEAD_DIM
    for hp in range(ATT_HEADS // 2):
        c0 = hp * LANES
        q2 = uq_ref[0, :, POOL_WIDTH + c0:POOL_WIDTH + c0 + LANES]
        k2 = jnp.concatenate([r[0, :, c0:c0 + LANES] for r in (kv2_ref, kv1_ref, kv0_ref)], axis=0)
        v2 = jnp.concatenate(
            [r[0, :, ATT_WIDTH + c0:ATT_WIDTH + c0 + LANES] for r in (kv2_ref, kv1_ref, kv0_ref)], axis=0)
        outs = []
        for jj in range(2):
            qm = jnp.where(lane_hi if jj else jnp.logical_not(lane_hi), q2, jnp.zeros_like(q2))
            sc = lax.dot_general(qm, k2, (((1,), (1,)), ((), ())), preferred_element_type=F32)
            sc = jnp.where(kvalid, sc + bias_ref[2 * hp + jj], NEG_BIG)
            mx = jnp.max(sc, axis=-1, keepdims=True)
            p = jnp.exp2(sc - mx)
            den = jnp.sum(p, axis=-1, keepdims=True)
            o = jnp.dot(p.astype(BF16), v2, preferred_element_type=F32)
            outs.append(o / den)
        o_pair = jnp.where(lane_hi, outs[1], outs[0])
        o_ref[0, :, POOL_WIDTH + c0:POOL_WIDTH + c0 + LANES] = o_pair.astype(o_ref.dtype)


def _attn_bias_tile(rel_bias):
    nh = rel_bias.shape[0]
    i = jnp.arange(ATT_TQ)[:, None]
    s = jnp.arange(ATT_TK)[None, :]
    n_const = ATT_PREV + ATT_TQ - REL_CLIP
    n_tail = ATT_TK + ATT_TQ - 1 - n_const
    v = jnp.concatenate([jnp.broadcast_to(rel_bias[:, 2 * REL_CLIP:], (nh, n_const)),
                         rel_bias[:, 2 * REL_CLIP - n_tail:2 * REL_CLIP][:, ::-1]], axis=1).astype(F32)
    nv = n_const + n_tail
    stream = jnp.broadcast_to(jnp.pad(v, ((0, 0), (0, 1)))[:, None, :], (nh, ATT_TQ, nv + 1)).reshape(nh, -1)
    skew = stream[:, :ATT_TQ * nv].reshape(nh, ATT_TQ, nv)
    bias = skew[:, :, ATT_TQ - 1:ATT_TQ - 1 + ATT_TK] * LOG2E
    band = s - (i // ATT_CHUNK) * ATT_CHUNK
    ok = (band >= 0) & (band < (ATT_PREV_CHUNKS + 1) * ATT_CHUNK)
    return jnp.where(ok[None], bias, NEG_BIG)


def even_mixer(proj, bias_tile, pool_w, pool_scale):
    b, s, n = proj.shape
    tq = ATT_TQ
    half = n // 2
    halo_blocks = tq // POOL_HALO
    return pl.pallas_call(
        _even_mixer_kernel,
        grid=(b, s // tq),
        in_specs=[
            pl.BlockSpec((1, tq, half), lambda i, j: (i, j, 0)),
            pl.BlockSpec((1, tq, half), lambda i, j: (i, jnp.maximum(j - 2, 0), 1)),
            pl.BlockSpec((1, tq, half), lambda i, j: (i, jnp.maximum(j - 1, 0), 1)),
            pl.BlockSpec((1, tq, half), lambda i, j: (i, j, 1)),
            pl.BlockSpec((1, POOL_HALO, POOL_WIDTH), lambda i, j: (i, jnp.maximum(j * halo_blocks - 1, 0), 0)),
            _resident(bias_tile.shape),
            _resident(pool_w.shape),
            _resident((1, POOL_WIDTH)),
        ],
        out_specs=pl.BlockSpec((1, tq, POOL_WIDTH + ATT_WIDTH), lambda i, j: (i, j, 0)),
        out_shape=jax.ShapeDtypeStruct((b, s, POOL_WIDTH + ATT_WIDTH), BF16),
        scratch_shapes=[pltpu.VMEM((POOL_HALO + tq, POOL_WIDTH), F32)],
        compiler_params=_cparams(("parallel", "parallel")),
        name="even_mixer",
    )(proj, proj, proj, proj, proj, bias_tile, pool_w, pool_scale.reshape(1, POOL_WIDTH))


def _group_logits(logits):
    lane = lax.broadcasted_iota(jnp.int32, logits.shape, 1)
    is_g = (lane >= MOE_EXPERTS) & (lane < MOE_EXPERTS + MOE_GROUPS)
    gl = jnp.where(is_g, logits, -jnp.inf)
    return lane, gl, jnp.max(gl, axis=-1, keepdims=True)


def _chunk_copy(src, dst, sem):
    return pltpu.make_async_copy(src, dst, sem)


def _out_proj_dispatch_kernel(a_ref, w_ref, x_ref, mod_ref, nw_ref, wr_ref, br_ref, ltri_ref,
                              xo_ref, lp_ref, meta_ref, tot_ref, sorted_ref,
                              sbuf, zbuf, sem, run_s, nis_s, *, grouped, cap):
    i = pl.program_id(0)
    n = pl.num_programs(0)
    slot = i % 2
    tm, ch = MOE_TM, MOE_CH

    @pl.when(i == 0)
    def _():
        for g in range(MOE_GROUPS):
            run_s[g] = 0
        nis_s[0] = 0
        nis_s[1] = 0
        zbuf[...] = jnp.zeros_like(zbuf)

    m = mod_ref[0]
    if grouped:
        ng, kw = a_ref.shape[1], a_ref.shape[3]
        mix = jnp.dot(a_ref[0, 0], w_ref[0:kw, :], preferred_element_type=F32)
        for g in range(1, ng):
            mix = mix + jnp.dot(a_ref[0, g], w_ref[g * kw:(g + 1) * kw, :], preferred_element_type=F32)
    else:
        mix = jnp.dot(a_ref[...], w_ref[...], preferred_element_type=F32)
    xn = x_ref[...] + m[2:3] * mix
    xo_ref[...] = xn
    h2 = _rms_mod(xn, nw_ref[...], m[3:4], m[4:5]).astype(BF16)

    logits = jnp.dot(h2, wr_ref[...], preferred_element_type=F32) + br_ref[...]
    lane, gl, gmax = _group_logits(logits)
    g_lane = jnp.min(jnp.where(gl == gmax, lane, 2 * LANES), axis=-1, keepdims=True)
    onehot = (lane == g_lane).astype(F32)
    rank = jnp.dot(ltri_ref[...], onehot.astype(BF16), preferred_element_type=F32)
    cnt_row = jnp.sum(onehot, axis=0, keepdims=True)
    lane1 = lax.broadcasted_iota(jnp.int32, (1, LANES), 1)
    nch, los = [], []
    lo = jnp.int32(0)
    lo_row = jnp.zeros((1, LANES), F32)
    for g in range(MOE_GROUPS):
        cnt = jnp.sum(jnp.where(lane1 == MOE_EXPERTS + g, cnt_row, 0.0)).astype(jnp.int32)
        nch.append((cnt + (ch - 1)) // ch)
        los.append(lo)
        lo_row = jnp.where(lane1 == MOE_EXPERTS + g, lo.astype(F32), lo_row)
        lo = lo + nch[g] * ch
    lp = jnp.sum(onehot * (rank + lo_row), axis=-1, keepdims=True)
    lp_ref[...] = lp.astype(jnp.int32)
    lpt = jnp.broadcast_to(lp, (tm, LANES)).T[0:1, :]
    rowi = lax.broadcasted_iota(jnp.int32, (MOE_LMAX, tm), 0).astype(F32)
    pm = jnp.where(rowi == lpt, 1.0, 0.0).astype(BF16)

    def _wait(sl, count):
        def body(c, carry):
            _chunk_copy(sbuf.at[sl, pl.ds(0, ch)], sorted_ref.at[pl.ds(0, ch)], sem.at[sl]).wait()
            return carry
        lax.fori_loop(0, count, body, 0)

    _wait(slot, nis_s[slot])
    sbuf[slot] = jnp.dot(pm, h2, preferred_element_type=F32).astype(BF16)
    total = jnp.int32(0)
    for g in range(MOE_GROUPS):
        dst0 = g * cap + run_s[g]

        def issue(c, carry, g=g, dst0=dst0):
            off = pl.multiple_of(c * ch, ch)
            _chunk_copy(sbuf.at[slot, pl.ds(pl.multiple_of(los[g] + off, ch), ch)],
                        sorted_ref.at[pl.ds(pl.multiple_of(dst0 + off, ch), ch)], sem.at[slot]).start()
            return carry

        lax.fori_loop(0, nch[g], issue, 0)
        meta_ref[i * MOE_META + 3 * g] = dst0
        meta_ref[i * MOE_META + 3 * g + 1] = nch[g]
        meta_ref[i * MOE_META + 3 * g + 2] = los[g]
        run_s[g] = run_s[g] + nch[g] * ch
        total = total + nch[g]
    nis_s[slot] = total

    @pl.when(i == n - 1)
    def _():
        for g in range(MOE_GROUPS):
            r = run_s[g]
            full = (r + (MOE_RT - 1)) // MOE_RT * MOE_RT
            nfill = (full - r) // ch
            base = g * cap + r

            def fill(c, carry, base=base):
                off = pl.multiple_of(c * ch, ch)
                _chunk_copy(zbuf, sorted_ref.at[pl.ds(pl.multiple_of(base + off, ch), ch)], sem.at[2]).start()
                return carry

            def fill_wait(c, carry):
                _chunk_copy(zbuf, sorted_ref.at[pl.ds(0, ch)], sem.at[2]).wait()
                return carry

            lax.fori_loop(0, nfill, fill, 0)
            lax.fori_loop(0, nfill, fill_wait, 0)
            tot_ref[g] = full
            tot_ref[MOE_GROUPS + g] = 0
        _wait(0, nis_s[0])
        _wait(1, nis_s[1])


def out_proj_dispatch(a, w, x, mod, nw2, wr, br, cap):
    b, s, d = x.shape
    t = b * s
    tm = MOE_TM
    tpb = s // tm
    nt = t // tm
    grouped = a.ndim == 4
    if grouped:
        a_spec = pl.BlockSpec((1, a.shape[1], tm, a.shape[3]), lambda i: (i // tpb, 0, i % tpb, 0))
    else:
        a = a.reshape(t, a.shape[2])
        a_spec = pl.BlockSpec((tm, a.shape[1]), lambda i: (i, 0))
    ltri = (jnp.arange(tm)[:, None] > jnp.arange(tm)[None, :]).astype(BF16)
    smem = pl.BlockSpec(memory_space=pltpu.SMEM)
    xo, lp, meta, tot, srt = pl.pallas_call(
        functools.partial(_out_proj_dispatch_kernel, grouped=grouped, cap=cap),
        grid=(nt,),
        in_specs=[
            a_spec,
            _resident(w.shape),
            pl.BlockSpec((tm, d), lambda i: (i, 0)),
            pl.BlockSpec((1, 6, d), lambda i: (i // tpb, 0, 0)),
            _resident((1, d)),
            _resident(wr.shape),
            _resident(br.shape),
            _resident(ltri.shape),
        ],
        out_specs=[
            pl.BlockSpec((tm, d), lambda i: (i, 0)),
            pl.BlockSpec((tm, 1), lambda i: (i, 0)),
            smem,
            smem,
            pl.BlockSpec(memory_space=pl.ANY),
        ],
        out_shape=[
            jax.ShapeDtypeStruct((t, d), F32),
            jax.ShapeDtypeStruct((t, 1), jnp.int32),
            jax.ShapeDtypeStruct((nt * MOE_META,), jnp.int32),
            jax.ShapeDtypeStruct((2 * MOE_GROUPS,), jnp.int32),
            jax.ShapeDtypeStruct((MOE_GROUPS * cap, d), BF16),
        ],
        scratch_shapes=[
            pltpu.VMEM((2, MOE_LMAX, d), BF16),
            pltpu.VMEM((MOE_CH, d), BF16),
            pltpu.SemaphoreType.DMA((3,)),
            pltpu.SMEM((MOE_GROUPS,), jnp.int32),
            pltpu.SMEM((2,), jnp.int32),
        ],
        compiler_params=_cparams(("arbitrary",)),
        name="out_proj_dispatch",
    )(a, w, x.reshape(t, d), mod, nw2.reshape(1, d), wr, br, ltri)
    return xo, lp, meta, tot, srt


def _ssd_kernel(z_ref, x_ref, b_ref, c_ref, dt_ref, dtb_ref, alog_ref, dskip_ref, nw_ref, o_ref,
                state, acsg, aspt, wt):
    j = pl.program_id(1)
    L = SSD_L

    @pl.when(j == 0)
    def _():
        state[...] = jnp.zeros_like(state)

    dt = jax.nn.softplus(dt_ref[0] + dtb_ref[...])
    a = dt * (-jnp.exp(alog_ref[...]) * LOG2E)
    row = lax.broadcasted_iota(jnp.int32, (L, L), 0)
    col = lax.broadcasted_iota(jnp.int32, (L, L), 1)
    tril = row >= col
    a_cs = jnp.dot(tril.astype(F32), a, preferred_element_type=F32, precision=lax.Precision.HIGHEST)
    a_last = a_cs[L - 1:L, :]
    aspt[...] = (a_cs - jnp.log2(dt)).T
    wt[...] = (dt * jnp.exp2(a_last - a_cs)).T
    for gi in range(SSM_GROUPS):
        shift = (LANES - gi * SSM_HEADS_PER_GROUP) % LANES
        acsg[gi] = pltpu.roll(a_cs, shift, 1) if shift else a_cs

    lane_hi = lax.broadcasted_iota(jnp.int32, (1, LANES), 1) >= SSM_HEAD_DIM

    def group_body(g, carry):
        xsb = x_ref[0, g]
        bmb = b_ref[0, g]
        cmb = c_ref[0, g]
        cb = lax.dot_general(cmb, bmb, (((1,), (1,)), ((), ())), preferred_element_type=F32)
        bmt = bmb.astype(F32).T
        st = state[g]
        yoff = jnp.dot(cmb, st.astype(BF16), preferred_element_type=F32)
        zz = z_ref[0, g].astype(F32)
        dsk = dskip_ref[g]
        acs_g = acsg[g]
        gated = []
        for q in range(2):
            c0 = q * LANES
            xs2b = xsb[:, c0:c0 + LANES]
            ys, sns, cds = [], [], []
            for jj in range(2):
                h = g * SSM_HEADS_PER_GROUP + 2 * q + jj
                r = 2 * q + jj
                alb = jnp.broadcast_to(acs_g[:, r:r + 1], (L, LANES))
                asr = aspt[pl.ds(h, 1), :]
                wr = wt[pl.ds(h, 1), :]
                al2 = jnp.concatenate([alb] * (L // LANES), axis=1)
                seg = jnp.exp2(jnp.where(tril, al2 - asr, -jnp.inf))
                mr = (cb * seg).astype(BF16)
                yd = jnp.dot(mr, xs2b, preferred_element_type=F32)
                ys.append(yd + jnp.exp2(alb) * yoff[:, c0:c0 + LANES])
                lhs = (bmt * wr).astype(BF16)
                sns.append(jnp.dot(lhs, xs2b, preferred_element_type=F32))
                cds.append(jnp.exp2(alb[L - 1:L, :]))
            y = jnp.where(lane_hi, ys[1], ys[0]) + xs2b.astype(F32) * dsk[:, c0:c0 + LANES]
            sn = jnp.where(lane_hi, sns[1], sns[0])
            cd = jnp.where(lane_hi, cds[1], cds[0])
            state[g, :, c0:c0 + LANES] = st[:, c0:c0 + LANES] * cd + sn
            gated.append(y * _silu(zz[:, c0:c0 + LANES]))
        ss = jnp.sum(gated[0] * gated[0], axis=-1, keepdims=True) + jnp.sum(gated[1] * gated[1], axis=-1, keepdims=True)
        rs = lax.rsqrt(ss / SSM_GROUP_WIDTH + EPS)
        nw = nw_ref[g]
        for q in range(2):
            c0 = q * LANES
            o_ref[0, g, :, c0:c0 + LANES] = (gated[q] * rs * nw[:, c0:c0 + LANES]).astype(o_ref.dtype)
        return carry

    lax.fori_loop(0, SSM_GROUPS, group_body, 0)


def ssd(z, xs, bm, cm, dt, dt_bias, a_log, d_skip, norm_w):
    b, ng, s, gw = z.shape
    L = SSD_L
    n = SSM_D_STATE
    pad = LANES - SSM_HEADS
    dtb = jnp.pad(dt_bias, (0, pad)).reshape(1, LANES)
    alog = jnp.pad(a_log, (0, pad)).reshape(1, LANES)
    dsk = jnp.repeat(d_skip, SSM_HEAD_DIM).reshape(ng, 1, gw)
    nw = norm_w.reshape(ng, 1, gw)
    tile4 = lambda w: pl.BlockSpec((1, ng, L, w), lambda i, j: (i, 0, j, 0))
    return pl.pallas_call(
        _ssd_kernel,
        grid=(b, s // L),
        in_specs=[
            tile4(gw), tile4(gw), tile4(n), tile4(n),
            pl.BlockSpec((1, L, LANES), lambda i, j: (i, j, 0)),
            _resident(dtb.shape), _resident(alog.shape), _resident(dsk.shape), _resident(nw.shape),
        ],
        out_specs=tile4(gw),
        out_shape=jax.ShapeDtypeStruct((b, ng, s, gw), BF16),
        scratch_shapes=[
            pltpu.VMEM((ng, n, gw), F32),
            pltpu.VMEM((ng, L, LANES), F32),
            pltpu.VMEM((LANES, L), F32),
            pltpu.VMEM((LANES, L), F32),
        ],
        compiler_params=_cparams(("parallel", "arbitrary")),
        name="ssd",
    )(z, xs, bm, cm, dt, dtb, alog, dsk, nw)


def _route_in_group(logits, g):
    lane, gl, gmax = _group_logits(logits)
    gsum = jnp.sum(jnp.exp(gl - gmax), axis=-1, keepdims=True)
    lg = jnp.sum(jnp.where(lane == MOE_EXPERTS + g, logits, 0.0), axis=-1, keepdims=True)
    g_val = jnp.exp(lg - gmax) / gsum
    in_grp = (lane >= g * MOE_EPG) & (lane < (g + 1) * MOE_EPG)
    el = jnp.where(in_grp, logits, -jnp.inf)
    emax = jnp.max(el, axis=-1, keepdims=True)
    ee = jnp.exp(el - emax)
    p = ee / jnp.sum(ee, axis=-1, keepdims=True)
    p = jnp.where(in_grp, p, -1.0)
    m1 = jnp.max(p, axis=-1, keepdims=True)
    i1 = jnp.min(jnp.where(p == m1, lane, 2 * LANES), axis=-1, keepdims=True)
    p2 = jnp.where(lane == i1, -1.0, p)
    m2 = jnp.max(p2, axis=-1, keepdims=True)
    i2 = jnp.min(jnp.where(p2 == m2, lane, 2 * LANES), axis=-1, keepdims=True)
    tot = m1 + m2
    w1 = g_val * (m1 / tot)
    w2 = g_val * (m2 / tot)
    return lane, jnp.where(lane == i1, w1, 0.0) + jnp.where(lane == i2, w2, 0.0)


def _moe_expert_kernel(tg_ref, tr_ref, nt_ref, rows_ref, wr_ref, br_ref, w1_ref, w3_ref, w2_ref, y_ref,
                       w1s, w3s, w2s):
    i = pl.program_id(0)
    g = tg_ref[i]

    @pl.when((i == 0) | (g != tg_ref[jnp.maximum(i - 1, 0)]))
    def _():
        for q in range(MOE_EPG):
            w1s[:, q * MOE_D_FF:(q + 1) * MOE_D_FF] = w1_ref[q].astype(BF16)
            w3s[:, q * MOE_D_FF:(q + 1) * MOE_D_FF] = w3_ref[q].astype(BF16)
            w2s[q * MOE_D_FF:(q + 1) * MOE_D_FF, :] = w2_ref[q].astype(BF16)

    @pl.when(i < nt_ref[0])
    def _():
        rows = rows_ref[...]
        logits = jnp.dot(rows, wr_ref[...], preferred_element_type=F32) + br_ref[...]
        lane, comb = _route_in_group(logits, g)
        a = jnp.dot(rows, w1s[...], preferred_element_type=F32)
        b = jnp.dot(rows, w3s[...], preferred_element_type=F32)
        hid = _silu(a) * b
        parts = []
        for q in range(MOE_EPG):
            cw = jnp.sum(jnp.where(lane == g * MOE_EPG + q, comb, 0.0), axis=-1, keepdims=True)
            parts.append((hid[:, q * MOE_D_FF:(q + 1) * MOE_D_FF] * cw).astype(BF16))
        hb = jnp.concatenate(parts, axis=1)
        y_ref[...] = jnp.dot(hb, w2s[...], preferred_element_type=F32).astype(y_ref.dtype)


def moe_experts(tg, tr, nt, srt, wr, br, w1, w3, w2, nt_max):
    rows, d = srt.shape
    gw = MOE_EPG * MOE_D_FF
    grid_spec = pltpu.PrefetchScalarGridSpec(
        num_scalar_prefetch=3,
        grid=(nt_max,),
        in_specs=[
            pl.BlockSpec((MOE_RT, d), lambda i, tg, tr, nt: (tr[i], 0)),
            _resident(wr.shape),
            _resident(br.shape),
            pl.BlockSpec((MOE_EPG, d, MOE_D_FF), lambda i, tg, tr, nt: (tg[i], 0, 0)),
            pl.BlockSpec((MOE_EPG, d, MOE_D_FF), lambda i, tg, tr, nt: (tg[i], 0, 0)),
            pl.BlockSpec((MOE_EPG, MOE_D_FF, d), lambda i, tg, tr, nt: (tg[i], 0, 0)),
        ],
        out_specs=pl.BlockSpec((MOE_RT, d), lambda i, tg, tr, nt: (tr[i], 0)),
        scratch_shapes=[pltpu.VMEM((d, gw), BF16), pltpu.VMEM((d, gw), BF16), pltpu.VMEM((gw, d), BF16)],
    )
    return pl.pallas_call(
        _moe_expert_kernel,
        grid_spec=grid_spec,
        out_shape=jax.ShapeDtypeStruct((rows, d), BF16),
        compiler_params=_cparams(("arbitrary",)),
        name="moe_experts",
    )(tg, tr, nt, srt, wr, br, w1, w3, w2)


def _moe_combine_kernel(meta_ref, x_ref, lp_ref, mod_ref, nw_ref, y_ref, o_ref, ybuf, sem, *, final):
    i = pl.program_id(0)
    n = pl.num_programs(0)
    ch = MOE_CH

    def run_copies(tile, sl, start):
        for g in range(MOE_GROUPS):
            dst0 = meta_ref[tile * MOE_META + 3 * g]
            nch = meta_ref[tile * MOE_META + 3 * g + 1]
            lo = meta_ref[tile * MOE_META + 3 * g + 2]

            def body(c, carry, dst0=dst0, lo=lo):
                off = pl.multiple_of(c * ch, ch)
                cp = _chunk_copy(y_ref.at[pl.ds(pl.multiple_of(dst0 + off, ch), ch)],
                                 ybuf.at[sl, pl.ds(pl.multiple_of(lo + off, ch), ch)], sem.at[sl])
                if start:
                    cp.start()
                else:
                    cp.wait()
                return carry

            lax.fori_loop(0, nch, body, 0)

    @pl.when(i == 0)
    def _():
        ybuf[...] = jnp.zeros_like(ybuf)
        run_copies(0, 0, True)

    @pl.when(i + 1 < n)
    def _():
        run_copies(i + 1, (i + 1) % 2, True)

    run_copies(i, i % 2, False)
    lane = lax.broadcasted_iota(jnp.int32, (MOE_TM, MOE_LMAX), 1)
    pmt = jnp.where(lane == lp_ref[...], 1.0, 0.0).astype(BF16)
    ffn = jnp.dot(pmt, ybuf[i % 2], preferred_element_type=F32)
    xn = x_ref[...] + mod_ref[0][5:6] * ffn
    if final:
        ms = jnp.mean(xn * xn, axis=-1, keepdims=True)
        xn = xn * lax.rsqrt(ms + EPS) * nw_ref[...]
    o_ref[...] = xn


def moe_combine(meta, x, lp, mod, y, nw, final, tpb):
    t, d = x.shape
    tm = MOE_TM
    grid_spec = pltpu.PrefetchScalarGridSpec(
        num_scalar_prefetch=1,
        grid=(t // tm,),
        in_specs=[
            pl.BlockSpec((tm, d), lambda i, m: (i, 0)),
            pl.BlockSpec((tm, 1), lambda i, m: (i, 0)),
            pl.BlockSpec((1, 6, d), lambda i, m: (i // tpb, 0, 0)),
            _resident((1, d)),
            pl.BlockSpec(memory_space=pl.ANY),
        ],
        out_specs=pl.BlockSpec((tm, d), lambda i, m: (i, 0)),
        scratch_shapes=[pltpu.VMEM((2, MOE_LMAX, d), BF16), pltpu.SemaphoreType.DMA((2,))],
    )
    return pl.pallas_call(
        functools.partial(_moe_combine_kernel, final=final),
        grid_spec=grid_spec,
        out_shape=jax.ShapeDtypeStruct((t, d), F32),
        compiler_params=_cparams(("arbitrary",)),
        name="moe_combine",
    )(meta, x, lp, mod, nw.reshape(1, d), y)


def _expert_schedule(tot, cap, nt_max):
    nt_g = tot[:MOE_GROUPS] // MOE_RT
    ends = jnp.cumsum(nt_g)
    starts = ends - nt_g
    nt = ends[MOE_GROUPS - 1]
    ic = jnp.minimum(jnp.arange(nt_max, dtype=jnp.int32), nt - 1)
    tg = jnp.sum((ic[:, None] >= ends[None, :]).astype(jnp.int32), axis=1)
    tr = tg * (cap // MOE_RT) + ic - starts[tg]
    return tg.astype(jnp.int32), tr.astype(jnp.int32), nt.reshape(1).astype(jnp.int32)


def _even_chunks(n):
    cw = 512
    return tuple((c, cw, ((0, cw, 0, None, c),)) for c in range(0, n, cw))


def _router_weights(w_group, b_group, w_expert, b_expert):
    d = w_group.shape[0]
    pad = LANES - MOE_EXPERTS - MOE_GROUPS
    wr = jnp.concatenate([w_expert.reshape(d, MOE_EXPERTS), w_group, jnp.zeros((d, pad), F32)], axis=1)
    br = jnp.concatenate([b_expert.reshape(MOE_EXPERTS), b_group, jnp.zeros((pad,), F32)]).reshape(1, LANES)
    return wr.astype(BF16), br


def kernel(x, c, ada_w, ada_b, norm1_w, norm2_w, mix_w_in, pool_w, pool_scale, rel_bias, mix_w_out, ssm_w_in, ssm_conv_w, ssm_conv_b, ssm_dt_bias, ssm_A_log, ssm_D, ssm_norm_w, ssm_w_out, moe_w_group, moe_b_group, moe_w_expert, moe_b_expert, moe_w1, moe_w3, moe_w2, final_norm_w):
    b, s, d = x.shape
    depth = ada_w.shape[0]
    tm = 512
    mod_all = adaln(c, ada_w, ada_b)
    n_tiles = (b * s) // MOE_TM
    cap = -(-(b * s + n_tiles * MOE_CH) // MOE_RT) * MOE_RT
    nt_max = (b * s + n_tiles * MOE_GROUPS * MOE_CH) // MOE_RT + MOE_GROUPS
    for layer in range(depth):
        mod = mod_all[layer].reshape(b, 6, d)
        i = layer // 2
        if layer % 2 == 0:
            col = jnp.arange(mix_w_in.shape[2])
            q_cols = (col >= POOL_WIDTH) & (col < POOL_WIDTH + ATT_WIDTH)
            w_in = (mix_w_in[i] * jnp.where(q_cols, ATT_HEAD_DIM ** -0.5 * LOG2E, 1.0)).astype(BF16)
            (proj,) = norm_proj(x, norm1_w[layer], mod, w_in,
                                [jax.ShapeDtypeStruct((b, s, w_in.shape[1]), BF16)], _even_chunks(w_in.shape[1]), tm)
            a = even_mixer(proj, _attn_bias_tile(rel_bias[i]), pool_w[i].astype(BF16), pool_scale[i])
            mixed, w_out = a, mix_w_out[i].astype(BF16)
        else:
            z, xs, bm, cm, dt = ssm_in_proj(x, norm1_w[layer], mod, ssm_w_in[i], ssm_conv_w[i], ssm_conv_b[i], tm)
            g = ssd(z, xs, bm, cm, dt, ssm_dt_bias[i], ssm_A_log[i], ssm_D[i], ssm_norm_w[i])
            mixed, w_out = g, ssm_w_out[i].astype(BF16)
        wr, br = _router_weights(moe_w_group[layer], moe_b_group[layer], moe_w_expert[layer], moe_b_expert[layer])
        xk, lp, meta, tot, srt = out_proj_dispatch(mixed, w_out, x, mod, norm2_w[layer], wr, br, cap)
        tg, tr, nt = _expert_schedule(tot, cap, nt_max)
        y = moe_experts(tg, tr, nt, srt, wr, br, moe_w1[layer], moe_w3[layer], moe_w2[layer], nt_max)
        x = moe_combine(meta, xk, lp, mod, y, final_norm_w, layer == depth - 1, s // MOE_TM).reshape(b, s, d)
    return x
```

```python
import functools
import math

import jax
import jax.numpy as jnp
from jax import lax
from jax.experimental import pallas as pl
from jax.experimental.pallas import tpu as pltpu

F32 = jnp.float32
BF16 = jnp.bfloat16

EPS = 1e-6
LANES = 128

POOL_WINDOWS = (2, 4, 8, 16)
POOL_GROUP = 128
POOL_WIDTH = POOL_GROUP * len(POOL_WINDOWS)
POOL_HALO = 16
ATT_HEAD_DIM = 64
ATT_HEADS = 8
ATT_WIDTH = ATT_HEADS * ATT_HEAD_DIM
ATT_CHUNK = 64
ATT_PREV_CHUNKS = 8
ATT_TQ = 256
ATT_PREV = ATT_PREV_CHUNKS * ATT_CHUNK
ATT_TK = ATT_PREV + ATT_TQ
REL_CLIP = 256
NEG_BIG = -1e30

SSM_GROUPS = 8
SSM_HEADS = 32
SSM_HEADS_PER_GROUP = 4
SSM_HEAD_DIM = 64
SSM_D_STATE = 128
SSM_D_INNER = SSM_HEADS * SSM_HEAD_DIM
SSM_GROUP_WIDTH = SSM_HEADS_PER_GROUP * SSM_HEAD_DIM
SSM_D_CONV = 4
PROJ_HALO = 16
SSD_L = 256
LOG2E = math.log2(math.e)

MOE_GROUPS = 4
MOE_EPG = 8
MOE_EXPERTS = 32
MOE_D_FF = 128
MOE_TM = 512
MOE_CH = 32
MOE_LMAX = MOE_TM + MOE_GROUPS * MOE_CH
MOE_RT = 256
MOE_META = 3 * MOE_GROUPS

VMEM_LIMIT = 56 * 1024 * 1024


def _cparams(sem):
    return pltpu.CompilerParams(dimension_semantics=sem, vmem_limit_bytes=VMEM_LIMIT)


def _resident(shape):
    nd = len(shape)
    return pl.BlockSpec(shape, lambda *_: (0,) * nd, pipeline_mode=pl.Buffered(1))


def _silu(v):
    return v * jax.nn.sigmoid(v)


def _rms_mod(xv, nw, shift, scale):
    ms = jnp.mean(xv * xv, axis=-1, keepdims=True)
    y = xv * lax.rsqrt(ms + EPS) * nw
    return y * (1.0 + scale) + shift


def _adaln_kernel(c_ref, w_ref, b_ref, o_ref):
    c = c_ref[...]
    ca = _silu(c).astype(BF16)
    o_ref[0] = jnp.dot(ca, w_ref[0].astype(BF16), preferred_element_type=F32) + b_ref[0]


def adaln(c, ada_w, ada_b):
    depth, d, n = ada_w.shape
    b = c.shape[0]
    tn = 1536
    return pl.pallas_call(
        _adaln_kernel,
        grid=(depth, n // tn),
        in_specs=[
            pl.BlockSpec((b, d), lambda l, j: (0, 0)),
            pl.BlockSpec((1, d, tn), lambda l, j: (l, 0, j)),
            pl.BlockSpec((1, 1, tn), lambda l, j: (l, 0, j)),
        ],
        out_specs=pl.BlockSpec((1, b, tn), lambda l, j: (l, 0, j)),
        out_shape=jax.ShapeDtypeStruct((depth, b, n), F32),
        compiler_params=_cparams(("arbitrary", "arbitrary")),
        name="adaln",
    )(c, ada_w, ada_b.reshape(depth, 1, n))


def _norm_proj_kernel(x_ref, nw_ref, mod_ref, w_ref, *o_refs, chunks):
    m = mod_ref[0]
    h = _rms_mod(x_ref[0], nw_ref[...], m[0:1], m[1:2]).astype(BF16)
    for col, width, stores in chunks:
        r = jnp.dot(h, w_ref[:, col:col + width], preferred_element_type=F32)
        for off, w, oi, g, dst in stores:
            o_ref = o_refs[oi]
            v = r[:, off:off + w].astype(o_ref.dtype)
            if g is None:
                o_ref[0, :, dst:dst + w] = v
            else:
                o_ref[0, g, :, dst:dst + w] = v


def norm_proj(x, nw, mod, w, out_shapes, chunks, tm):
    b, s, d = x.shape
    out_specs = []
    for sh in out_shapes:
        if len(sh.shape) == 3:
            out_specs.append(pl.BlockSpec((1, tm, sh.shape[2]), lambda i, j: (i, j, 0)))
        else:
            out_specs.append(pl.BlockSpec((1, sh.shape[1], tm, sh.shape[3]), lambda i, j: (i, 0, j, 0)))
    return pl.pallas_call(
        functools.partial(_norm_proj_kernel, chunks=chunks),
        grid=(b, s // tm),
        in_specs=[
            pl.BlockSpec((1, tm, d), lambda i, j: (i, j, 0)),
            _resident((1, d)),
            pl.BlockSpec((1, 6, d), lambda i, j: (i, 0, 0)),
            _resident(w.shape),
        ],
        out_specs=out_specs,
        out_shape=out_shapes,
        compiler_params=_cparams(("parallel", "parallel")),
        name="norm_proj",
    )(x, nw.reshape(1, d), mod, w)


def _ssm_in_proj_kernel(x_ref, nw_ref, mod_ref, w_ref, cw_ref, cb_ref,
                        z_ref, xs_ref, b_ref, c_ref, dt_ref, halo, *, chunks):
    assert SSM_D_CONV == 4
    j = pl.program_id(1)
    tm = x_ref.shape[1]
    hr = PROJ_HALO
    m = mod_ref[0]
    h = _rms_mod(x_ref[0], nw_ref[...], m[0:1], m[1:2]).astype(BF16)
    outs = (z_ref, xs_ref, b_ref, c_ref, dt_ref)
    for col, width, conv_col, stores in chunks:
        r = jnp.dot(h, w_ref[:, col:col + width], preferred_element_type=F32)
        if conv_col is None:
            res = r
        else:
            prev = jnp.where(j > 0, halo[:, conv_col:conv_col + width], 0.0)
            halo[:, conv_col:conv_col + width] = r[tm - hr:, :]
            re = jnp.concatenate([prev, r], axis=0)
            cw = cw_ref[:, conv_col:conv_col + width]
            u = pltpu.roll(re, 1, 0)
            near = r * cw[3:4] + u[hr:, :] * cw[2:3]
            far = pltpu.roll(re * cw[1:2] + u * cw[0:1], 2, 0)
            res = _silu(cb_ref[:, conv_col:conv_col + width] + near + far[hr:, :])
        for off, w, oi, g in stores:
            o_ref = outs[oi]
            v = res[:, off:off + w].astype(o_ref.dtype)
            if g is None:
                o_ref[0] = v
            else:
                o_ref[0, g] = v


def _ssm_in_proj_chunks():
    gw, n, ng, cw = SSM_GROUP_WIDTH, SSM_D_STATE, SSM_GROUPS, 512
    chunks = []
    col = 0
    for oi, conv0 in ((0, None), (1, 0)):
        for g in range(0, ng, 2):
            conv_col = None if conv0 is None else conv0 + g * gw
            chunks.append((col, cw, conv_col, ((0, gw, oi, g), (gw, gw, oi, g + 1))))
            col += cw
    for oi, conv0 in ((2, SSM_D_INNER), (3, SSM_D_INNER + ng * n)):
        for g in range(0, ng, 4):
            chunks.append((col, cw, conv0 + g * n, tuple((k * n, n, oi, g + k) for k in range(4))))
            col += cw
    chunks.append((col, LANES, None, ((0, LANES, 4, None),)))
    return tuple(chunks), col + LANES


def ssm_in_proj(x, nw, mod, w, conv_w, conv_b, tm):
    b, s, d = x.shape
    ng, gw, n = SSM_GROUPS, SSM_GROUP_WIDTH, SSM_D_STATE
    chunks, cols = _ssm_in_proj_chunks()
    w = jnp.pad(w, ((0, 0), (0, cols - w.shape[1]))).astype(BF16)
    tile4 = lambda wd: pl.BlockSpec((1, ng, tm, wd), lambda i, j: (i, 0, j, 0))
    return pl.pallas_call(
        functools.partial(_ssm_in_proj_kernel, chunks=chunks),
        grid=(b, s // tm),
        in_specs=[
            pl.BlockSpec((1, tm, d), lambda i, j: (i, j, 0)),
            _resident((1, d)),
            pl.BlockSpec((1, 6, d), lambda i, j: (i, 0, 0)),
            _resident(w.shape),
            _resident(conv_w.shape),
            _resident((1, conv_b.shape[0])),
        ],
        out_specs=[tile4(gw), tile4(gw), tile4(n), tile4(n), pl.BlockSpec((1, tm, LANES), lambda i, j: (i, j, 0))],
        out_shape=[
            jax.ShapeDtypeStruct((b, ng, s, gw), BF16),
            jax.ShapeDtypeStruct((b, ng, s, gw), BF16),
            jax.ShapeDtypeStruct((b, ng, s, n), BF16),
            jax.ShapeDtypeStruct((b, ng, s, n), BF16),
            jax.ShapeDtypeStruct((b, s, LANES), F32),
        ],
        scratch_shapes=[pltpu.VMEM((PROJ_HALO, conv_w.shape[1]), F32)],
        compiler_params=_cparams(("parallel", "arbitrary")),
        name="ssm_in_proj",
    )(x, nw.reshape(1, d), mod, w, conv_w, conv_b.reshape(1, -1))


def _even_mixer_kernel(uq_ref, kv2_ref, kv1_ref, kv0_ref, halo_ref, bias_ref, pw_ref, ps_ref, o_ref, ubuf):
    j = pl.program_id(1)
    tq = ATT_TQ

    u = uq_ref[0, :, 0:POOL_WIDTH].astype(F32)
    halo = halo_ref[0].astype(F32)
    ubuf[0:POOL_HALO, :] = jnp.where(j > 0, halo, 0.0)
    ubuf[POOL_HALO:POOL_HALO + tq, :] = u
    pos1 = j * tq + 1 + lax.broadcasted_iota(jnp.int32, (tq, 1), 0)
    for g, win in enumerate(POOL_WINDOWS):
        lo = g * POOL_GROUP
        assert win & (win - 1) == 0 and win <= POOL_HALO
        acc = ubuf[:, lo:lo + POOL_GROUP]
        k = 1
        while k < win:
            acc = acc + pltpu.roll(acc, k, 0)
            k *= 2
        acc = acc[POOL_HALO:, :]
        cnt = jnp.minimum(pos1, win).astype(F32)
        pooled = acc * (1.0 / cnt) - u[:, lo:lo + POOL_GROUP]
        y = jnp.dot(pooled.astype(BF16), pw_ref[g], preferred_element_type=F32)
        o_ref[0, :, lo:lo + POOL_GROUP] = (y * ps_ref[:, lo:lo + POOL_GROUP]).astype(o_ref.dtype)

    kpos = j * tq - ATT_PREV + lax.broadcasted_iota(jnp.int32, (1, ATT_TK), 1)
    kvalid = kpos >= 0
    lane_hi = lax.broadcasted_iota(jnp.int32, (1, LANES), 1) >= ATT_HEAD_DIM
    for hp in range(ATT_HEADS // 2):
        c0 = hp * LANES
        q2 = uq_ref[0, :, POOL_WIDTH + c0:POOL_WIDTH + c0 + LANES]
        k2 = jnp.concatenate([r[0, :, c0:c0 + LANES] for r in (kv2_ref, kv1_ref, kv0_ref)], axis=0)
        v2 = jnp.concatenate(
            [r[0, :, ATT_WIDTH + c0:ATT_WIDTH + c0 + LANES] for r in (kv2_ref, kv1_ref, kv0_ref)], axis=0)
        zero = jnp.zeros_like(q2)
        qm = jnp.concatenate([jnp.where(lane_hi, zero, q2), jnp.where(lane_hi, q2, zero)], axis=0)
        sc = lax.dot_general(qm, k2, (((1,), (1,)), ((), ())), preferred_element_type=F32)
        bias2 = bias_ref[2 * hp:2 * hp + 2].reshape(2 * tq, ATT_TK)
        sc = jnp.where(kvalid, sc + bias2, NEG_BIG)
        mx = jnp.max(sc, axis=-1, keepdims=True)
        p = jnp.exp2(sc - mx)
        den = jnp.sum(p, axis=-1, keepdims=True)
        o = jnp.dot(p.astype(BF16), v2, preferred_element_type=F32) * (1.0 / den)
        o_pair = jnp.where(lane_hi, o[tq:, :], o[0:tq, :])
        o_ref[0, :, POOL_WIDTH + c0:POOL_WIDTH + c0 + LANES] = o_pair.astype(o_ref.dtype)


def _attn_bias_tile(rel_bias):
    nh = rel_bias.shape[0]
    i = jnp.arange(ATT_TQ)[:, None]
    s = jnp.arange(ATT_TK)[None, :]
    n_const = ATT_PREV + ATT_TQ - REL_CLIP
    n_tail = ATT_TK + ATT_TQ - 1 - n_const
    v = jnp.concatenate([jnp.broadcast_to(rel_bias[:, 2 * REL_CLIP:], (nh, n_const)),
                         rel_bias[:, 2 * REL_CLIP - n_tail:2 * REL_CLIP][:, ::-1]], axis=1).astype(F32)
    nv = n_const + n_tail
    stream = jnp.broadcast_to(jnp.pad(v, ((0, 0), (0, 1)))[:, None, :], (nh, ATT_TQ, nv + 1)).reshape(nh, -1)
    skew = stream[:, :ATT_TQ * nv].reshape(nh, ATT_TQ, nv)
    bias = skew[:, :, ATT_TQ - 1:ATT_TQ - 1 + ATT_TK] * LOG2E
    band = s - (i // ATT_CHUNK) * ATT_CHUNK
    ok = (band >= 0) & (band < (ATT_PREV_CHUNKS + 1) * ATT_CHUNK)
    return jnp.where(ok[None], bias, NEG_BIG)


def even_mixer(proj, bias_tile, pool_w, pool_scale):
    b, s, n = proj.shape
    tq = ATT_TQ
    half = n // 2
    halo_blocks = tq // POOL_HALO
    return pl.pallas_call(
        _even_mixer_kernel,
        grid=(b, s // tq),
        in_specs=[
            pl.BlockSpec((1, tq, half), lambda i, j: (i, j, 0)),
            pl.BlockSpec((1, tq, half), lambda i, j: (i, jnp.maximum(j - 2, 0), 1)),
            pl.BlockSpec((1, tq, half), lambda i, j: (i, jnp.maximum(j - 1, 0), 1)),
            pl.BlockSpec((1, tq, half), lambda i, j: (i, j, 1)),
            pl.BlockSpec((1, POOL_HALO, POOL_WIDTH), lambda i, j: (i, jnp.maximum(j * halo_blocks - 1, 0), 0)),
            _resident(bias_tile.shape),
            _resident(pool_w.shape),
            _resident((1, POOL_WIDTH)),
        ],
        out_specs=pl.BlockSpec((1, tq, POOL_WIDTH + ATT_WIDTH), lambda i, j: (i, j, 0)),
        out_shape=jax.ShapeDtypeStruct((b, s, POOL_WIDTH + ATT_WIDTH), BF16),
        scratch_shapes=[pltpu.VMEM((POOL_HALO + tq, POOL_WIDTH), F32)],
        compiler_params=_cparams(("parallel", "parallel")),
        name="even_mixer",
    )(proj, proj, proj, proj, proj, bias_tile, pool_w, pool_scale.reshape(1, POOL_WIDTH))


def _group_logits(logits):
    lane = lax.broadcasted_iota(jnp.int32, logits.shape, 1)
    is_g = (lane >= MOE_EXPERTS) & (lane < MOE_EXPERTS + MOE_GROUPS)
    gl = jnp.where(is_g, logits, -jnp.inf)
    return lane, gl, jnp.max(gl, axis=-1, keepdims=True)


def _chunk_copy(src, dst, sem):
    return pltpu.make_async_copy(src, dst, sem)


def _out_proj_dispatch_kernel(a_ref, w_ref, x_ref, mod_ref, nw_ref, wr_ref, br_ref, ltri_ref,
                              xo_ref, lp_ref, meta_ref, tot_ref, sorted_ref,
                              sbuf, zbuf, sem, run_s, nis_s, *, grouped, cap):
    i = pl.program_id(0)
    n = pl.num_programs(0)
    slot = i % 2
    tm, ch = MOE_TM, MOE_CH

    @pl.when(i == 0)
    def _():
        for g in range(MOE_GROUPS):
            run_s[g] = 0
        nis_s[0] = 0
        nis_s[1] = 0
        zbuf[...] = jnp.zeros_like(zbuf)

    m = mod_ref[0]
    if grouped:
        ng, kw = a_ref.shape[1], a_ref.shape[3]
        mix = jnp.dot(a_ref[0, 0], w_ref[0:kw, :], preferred_element_type=F32)
        for g in range(1, ng):
            mix = mix + jnp.dot(a_ref[0, g], w_ref[g * kw:(g + 1) * kw, :], preferred_element_type=F32)
    else:
        mix = jnp.dot(a_ref[...], w_ref[...], preferred_element_type=F32)
    xn = x_ref[...] + m[2:3] * mix
    xo_ref[...] = xn
    h2 = _rms_mod(xn, nw_ref[...], m[3:4], m[4:5]).astype(BF16)

    logits = jnp.dot(h2, wr_ref[...], preferred_element_type=F32) + br_ref[...]
    lane, gl, gmax = _group_logits(logits)
    g_lane = jnp.min(jnp.where(gl == gmax, lane, 2 * LANES), axis=-1, keepdims=True)
    onehot = (lane == g_lane).astype(F32)
    rank = jnp.dot(ltri_ref[...], onehot.astype(BF16), preferred_element_type=F32)
    cnt_row = jnp.sum(onehot, axis=0, keepdims=True)
    lane1 = lax.broadcasted_iota(jnp.int32, (1, LANES), 1)
    nch, los = [], []
    lo = jnp.int32(0)
    lo_row = jnp.zeros((1, LANES), F32)
    for g in range(MOE_GROUPS):
        cnt = jnp.sum(jnp.where(lane1 == MOE_EXPERTS + g, cnt_row, 0.0)).astype(jnp.int32)
        nch.append((cnt + (ch - 1)) // ch)
        los.append(lo)
        lo_row = jnp.where(lane1 == MOE_EXPERTS + g, lo.astype(F32), lo_row)
        lo = lo + nch[g] * ch
    lp = jnp.sum(onehot * (rank + lo_row), axis=-1, keepdims=True)
    lp_ref[...] = lp.astype(jnp.int32)
    lpt = jnp.broadcast_to(lp, (tm, LANES)).T[0:1, :]
    rowi = lax.broadcasted_iota(jnp.int32, (MOE_LMAX, tm), 0).astype(F32)
    pm = jnp.where(rowi == lpt, 1.0, 0.0).astype(BF16)

    def _wait(sl, count):
        def body(c, carry):
            _chunk_copy(sbuf.at[sl, pl.ds(0, ch)], sorted_ref.at[pl.ds(0, ch)], sem.at[sl]).wait()
            return carry
        lax.fori_loop(0, count, body, 0)

    _wait(slot, nis_s[slot])
    sbuf[slot] = jnp.dot(pm, h2, preferred_element_type=F32).astype(BF16)
    total = jnp.int32(0)
    for g in range(MOE_GROUPS):
        dst0 = g * cap + run_s[g]

        def issue(c, carry, g=g, dst0=dst0):
            off = pl.multiple_of(c * ch, ch)
            _chunk_copy(sbuf.at[slot, pl.ds(pl.multiple_of(los[g] + off, ch), ch)],
                        sorted_ref.at[pl.ds(pl.multiple_of(dst0 + off, ch), ch)], sem.at[slot]).start()
            return carry

        lax.fori_loop(0, nch[g], issue, 0)
        meta_ref[i * MOE_META + 3 * g] = dst0
        meta_ref[i * MOE_META + 3 * g + 1] = nch[g]
        meta_ref[i * MOE_META + 3 * g + 2] = los[g]
        run_s[g] = run_s[g] + nch[g] * ch
        total = total + nch[g]
    nis_s[slot] = total

    @pl.when(i == n - 1)
    def _():
        for g in range(MOE_GROUPS):
            r = run_s[g]
            full = (r + (MOE_RT - 1)) // MOE_RT * MOE_RT
            nfill = (full - r) // ch
            base = g * cap + r

            def fill(c, carry, base=base):
                off = pl.multiple_of(c * ch, ch)
                _chunk_copy(zbuf, sorted_ref.at[pl.ds(pl.multiple_of(base + off, ch), ch)], sem.at[2]).start()
                return carry

            def fill_wait(c, carry):
                _chunk_copy(zbuf, sorted_ref.at[pl.ds(0, ch)], sem.at[2]).wait()
                return carry

            lax.fori_loop(0, nfill, fill, 0)
            lax.fori_loop(0, nfill, fill_wait, 0)
            tot_ref[g] = full
            tot_ref[MOE_GROUPS + g] = 0
        _wait(0, nis_s[0])
        _wait(1, nis_s[1])


def out_proj_dispatch(a, w, x, mod, nw2, wr, br, cap):
    b, s, d = x.shape
    t = b * s
    tm = MOE_TM
    tpb = s // tm
    nt = t // tm
    grouped = a.ndim == 4
    if grouped:
        a_spec = pl.BlockSpec((1, a.shape[1], tm, a.shape[3]), lambda i: (i // tpb, 0, i % tpb, 0))
    else:
        a = a.reshape(t, a.shape[2])
        a_spec = pl.BlockSpec((tm, a.shape[1]), lambda i: (i, 0))
    ltri = (jnp.arange(tm)[:, None] > jnp.arange(tm)[None, :]).astype(BF16)
    smem = pl.BlockSpec(memory_space=pltpu.SMEM)
    xo, lp, meta, tot, srt = pl.pallas_call(
        functools.partial(_out_proj_dispatch_kernel, grouped=grouped, cap=cap),
        grid=(nt,),
        in_specs=[
            a_spec,
            _resident(w.shape),
            pl.BlockSpec((tm, d), lambda i: (i, 0)),
            pl.BlockSpec((1, 6, d), lambda i: (i // tpb, 0, 0)),
            _resident((1, d)),
            _resident(wr.shape),
            _resident(br.shape),
            _resident(ltri.shape),
        ],
        out_specs=[
            pl.BlockSpec((tm, d), lambda i: (i, 0)),
            pl.BlockSpec((tm, 1), lambda i: (i, 0)),
            smem,
            smem,
            pl.BlockSpec(memory_space=pl.ANY),
        ],
        out_shape=[
            jax.ShapeDtypeStruct((t, d), F32),
            jax.ShapeDtypeStruct((t, 1), jnp.int32),
            jax.ShapeDtypeStruct((nt * MOE_META,), jnp.int32),
            jax.ShapeDtypeStruct((2 * MOE_GROUPS,), jnp.int32),
            jax.ShapeDtypeStruct((MOE_GROUPS * cap, d), BF16),
        ],
        scratch_shapes=[
            pltpu.VMEM((2, MOE_LMAX, d), BF16),
            pltpu.VMEM((MOE_CH, d), BF16),
            pltpu.SemaphoreType.DMA((3,)),
            pltpu.SMEM((MOE_GROUPS,), jnp.int32),
            pltpu.SMEM((2,), jnp.int32),
        ],
        compiler_params=_cparams(("arbitrary",)),
        name="out_proj_dispatch",
    )(a, w, x.reshape(t, d), mod, nw2.reshape(1, d), wr, br, ltri)
    return xo, lp, meta, tot, srt


def _ssd_kernel(z_ref, x_ref, b_ref, c_ref, dt_ref, dtb_ref, alog_ref, dskip_ref, nw_ref, o_ref,
                state, acsg, aspt, wt):
    j = pl.program_id(1)
    L = SSD_L

    @pl.when(j == 0)
    def _():
        state[...] = jnp.zeros_like(state)

    dt = jax.nn.softplus(dt_ref[0] + dtb_ref[...])
    a = dt * (-jnp.exp(alog_ref[...]) * LOG2E)
    row = lax.broadcasted_iota(jnp.int32, (L, L), 0)
    col = lax.broadcasted_iota(jnp.int32, (L, L), 1)
    tril = row >= col
    a_hi = a.astype(BF16)
    r1 = a - a_hi.astype(F32)
    a_mid = r1.astype(BF16)
    a_lo = (r1 - a_mid.astype(F32)).astype(BF16)
    parts = jnp.dot(tril.astype(BF16), jnp.concatenate([a_hi, a_mid, a_lo], axis=1), preferred_element_type=F32)
    a_cs = parts[:, 0:LANES] + parts[:, LANES:2 * LANES] + parts[:, 2 * LANES:]
    a_last = a_cs[L - 1:L, :]
    aspt[...] = (a_cs - jnp.log2(dt)).T
    wt[...] = (dt * jnp.exp2(a_last - a_cs)).T
    for gi in range(SSM_GROUPS):
        shift = (LANES - gi * SSM_HEADS_PER_GROUP) % LANES
        acsg[gi] = pltpu.roll(a_cs, shift, 1) if shift else a_cs

    lane_hi = lax.broadcasted_iota(jnp.int32, (1, LANES), 1) >= SSM_HEAD_DIM

    def group_body(g, carry):
        xsb = x_ref[0, g]
        bmb = b_ref[0, g]
        cmb = c_ref[0, g]
        cb = lax.dot_general(cmb, bmb, (((1,), (1,)), ((), ())), preferred_element_type=F32)
        bmt = bmb.astype(F32).T
        st = state[g]
        yoff = jnp.dot(cmb, st.astype(BF16), preferred_element_type=F32)
        zz = z_ref[0, g].astype(F32)
        dsk = dskip_ref[g]
        acs_g = acsg[g]
        gated = []
        for q in range(2):
            c0 = q * LANES
            xs2b = xsb[:, c0:c0 + LANES]
            albs, mrs, lhss, cds = [], [], [], []
            for jj in range(2):
                h = g * SSM_HEADS_PER_GROUP + 2 * q + jj
                r = 2 * q + jj
                alb = jnp.broadcast_to(acs_g[:, r:r + 1], (L, LANES))
                asr = aspt[pl.ds(h, 1), :]
                wr = wt[pl.ds(h, 1), :]
                al2 = jnp.concatenate([alb] * (L // LANES), axis=1)
                seg = jnp.exp2(jnp.where(tril, al2 - asr, -jnp.inf))
                albs.append(alb)
                mrs.append((cb * seg).astype(BF16))
                lhss.append((bmt * wr).astype(BF16))
                cds.append(jnp.exp2(alb[L - 1:L, :]))
            yd = jnp.dot(jnp.concatenate(mrs, axis=0), xs2b, preferred_element_type=F32)
            sn2 = jnp.dot(jnp.concatenate(lhss, axis=0), xs2b, preferred_element_type=F32)
            yo = yoff[:, c0:c0 + LANES]
            ys = [yd[jj * L:(jj + 1) * L, :] + jnp.exp2(albs[jj]) * yo for jj in range(2)]
            y = jnp.where(lane_hi, ys[1], ys[0]) + xs2b.astype(F32) * dsk[:, c0:c0 + LANES]
            sn = jnp.where(lane_hi, sn2[SSM_D_STATE:, :], sn2[0:SSM_D_STATE, :])
            cd = jnp.where(lane_hi, cds[1], cds[0])
            state[g, :, c0:c0 + LANES] = st[:, c0:c0 + LANES] * cd + sn
            gated.append(y * _silu(zz[:, c0:c0 + LANES]))
        ss = jnp.sum(gated[0] * gated[0], axis=-1, keepdims=True) + jnp.sum(gated[1] * gated[1], axis=-1, keepdims=True)
        rs = lax.rsqrt(ss / SSM_GROUP_WIDTH + EPS)
        nw = nw_ref[g]
        for q in range(2):
            c0 = q * LANES
            o_ref[0, g, :, c0:c0 + LANES] = (gated[q] * rs * nw[:, c0:c0 + LANES]).astype(o_ref.dtype)
        return carry

    lax.fori_loop(0, SSM_GROUPS, group_body, 0)


def ssd(z, xs, bm, cm, dt, dt_bias, a_log, d_skip, norm_w):
    b, ng, s, gw = z.shape
    L = SSD_L
    n = SSM_D_STATE
    pad = LANES - SSM_HEADS
    dtb = jnp.pad(dt_bias, (0, pad)).reshape(1, LANES)
    alog = jnp.pad(a_log, (0, pad)).reshape(1, LANES)
    dsk = jnp.repeat(d_skip, SSM_HEAD_DIM).reshape(ng, 1, gw)
    nw = norm_w.reshape(ng, 1, gw)
    tile4 = lambda w: pl.BlockSpec((1, ng, L, w), lambda i, j: (i, 0, j, 0))
    return pl.pallas_call(
        _ssd_kernel,
        grid=(b, s // L),
        in_specs=[
            tile4(gw), tile4(gw), tile4(n), tile4(n),
            pl.BlockSpec((1, L, LANES), lambda i, j: (i, j, 0)),
            _resident(dtb.shape), _resident(alog.shape), _resident(dsk.shape), _resident(nw.shape),
        ],
        out_specs=tile4(gw),
        out_shape=jax.ShapeDtypeStruct((b, ng, s, gw), BF16),
        scratch_shapes=[
            pltpu.VMEM((ng, n, gw), F32),
            pltpu.VMEM((ng, L, LANES), F32),
            pltpu.VMEM((LANES, L), F32),
            pltpu.VMEM((LANES, L), F32),
        ],
        compiler_params=_cparams(("parallel", "arbitrary")),
        name="ssd",
    )(z, xs, bm, cm, dt, dtb, alog, dsk, nw)


def _route_in_group(logits, g):
    lane, gl, gmax = _group_logits(logits)
    gsum = jnp.sum(jnp.exp(gl - gmax), axis=-1, keepdims=True)
    lg = jnp.sum(jnp.where(lane == MOE_EXPERTS + g, logits, 0.0), axis=-1, keepdims=True)
    g_val = jnp.exp(lg - gmax) / gsum
    in_grp = (lane >= g * MOE_EPG) & (lane < (g + 1) * MOE_EPG)
    el = jnp.where(in_grp, logits, -jnp.inf)
    emax = jnp.max(el, axis=-1, keepdims=True)
    ee = jnp.exp(el - emax)
    p = ee * (1.0 / jnp.sum(ee, axis=-1, keepdims=True))
    p = jnp.where(in_grp, p, -1.0)
    m1 = jnp.max(p, axis=-1, keepdims=True)
    i1 = jnp.min(jnp.where(p == m1, lane, 2 * LANES), axis=-1, keepdims=True)
    p2 = jnp.where(lane == i1, -1.0, p)
    m2 = jnp.max(p2, axis=-1, keepdims=True)
    i2 = jnp.min(jnp.where(p2 == m2, lane, 2 * LANES), axis=-1, keepdims=True)
    tot = m1 + m2
    w1 = g_val * (m1 / tot)
    w2 = g_val * (m2 / tot)
    return lane, jnp.where(lane == i1, w1, 0.0) + jnp.where(lane == i2, w2, 0.0)


def _moe_expert_kernel(tg_ref, tr_ref, nt_ref, rows_ref, wr_ref, br_ref, w1_ref, w3_ref, w2_ref, y_ref,
                       w1s, w3s, w2s):
    i = pl.program_id(0)
    g = tg_ref[i]

    @pl.when((i == 0) | (g != tg_ref[jnp.maximum(i - 1, 0)]))
    def _():
        for q in range(MOE_EPG):
            w1s[:, q * MOE_D_FF:(q + 1) * MOE_D_FF] = w1_ref[q].astype(BF16)
            w3s[:, q * MOE_D_FF:(q + 1) * MOE_D_FF] = w3_ref[q].astype(BF16)
            w2s[q * MOE_D_FF:(q + 1) * MOE_D_FF, :] = w2_ref[q].astype(BF16)

    @pl.when(i < nt_ref[0])
    def _():
        rows = rows_ref[...]
        logits = jnp.dot(rows, wr_ref[...], preferred_element_type=F32) + br_ref[...]
        lane, comb = _route_in_group(logits, g)
        a = jnp.dot(rows, w1s[...], preferred_element_type=F32)
        b = jnp.dot(rows, w3s[...], preferred_element_type=F32)
        hid = _silu(a) * b
        parts = []
        for q in range(MOE_EPG):
            cw = jnp.sum(jnp.where(lane == g * MOE_EPG + q, comb, 0.0), axis=-1, keepdims=True)
            parts.append((hid[:, q * MOE_D_FF:(q + 1) * MOE_D_FF] * cw).astype(BF16))
        hb = jnp.concatenate(parts, axis=1)
        y_ref[...] = jnp.dot(hb, w2s[...], preferred_element_type=F32).astype(y_ref.dtype)


def moe_experts(tg, tr, nt, srt, wr, br, w1, w3, w2, nt_max):
    rows, d = srt.shape
    gw = MOE_EPG * MOE_D_FF
    grid_spec = pltpu.PrefetchScalarGridSpec(
        num_scalar_prefetch=3,
        grid=(nt_max,),
        in_specs=[
            pl.BlockSpec((MOE_RT, d), lambda i, tg, tr, nt: (tr[i], 0)),
            _resident(wr.shape),
            _resident(br.shape),
            pl.BlockSpec((MOE_EPG, d, MOE_D_FF), lambda i, tg, tr, nt: (tg[i], 0, 0)),
            pl.BlockSpec((MOE_EPG, d, MOE_D_FF), lambda i, tg, tr, nt: (tg[i], 0, 0)),
            pl.BlockSpec((MOE_EPG, MOE_D_FF, d), lambda i, tg, tr, nt: (tg[i], 0, 0)),
        ],
        out_specs=pl.BlockSpec((MOE_RT, d), lambda i, tg, tr, nt: (tr[i], 0)),
        scratch_shapes=[pltpu.VMEM((d, gw), BF16), pltpu.VMEM((d, gw), BF16), pltpu.VMEM((gw, d), BF16)],
    )
    return pl.pallas_call(
        _moe_expert_kernel,
        grid_spec=grid_spec,
        out_shape=jax.ShapeDtypeStruct((rows, d), BF16),
        compiler_params=_cparams(("arbitrary",)),
        name="moe_experts",
    )(tg, tr, nt, srt, wr, br, w1, w3, w2)


def _moe_combine_kernel(meta_ref, x_ref, lp_ref, mod_ref, nw_ref, y_ref, o_ref, ybuf, sem, *, final):
    i = pl.program_id(0)
    n = pl.num_programs(0)
    ch = MOE_CH

    def run_copies(tile, sl, start):
        for g in range(MOE_GROUPS):
            dst0 = meta_ref[tile * MOE_META + 3 * g]
            nch = meta_ref[tile * MOE_META + 3 * g + 1]
            lo = meta_ref[tile * MOE_META + 3 * g + 2]

            def body(c, carry, dst0=dst0, lo=lo):
                off = pl.multiple_of(c * ch, ch)
                cp = _chunk_copy(y_ref.at[pl.ds(pl.multiple_of(dst0 + off, ch), ch)],
                                 ybuf.at[sl, pl.ds(pl.multiple_of(lo + off, ch), ch)], sem.at[sl])
                if start:
                    cp.start()
                else:
                    cp.wait()
                return carry

            lax.fori_loop(0, nch, body, 0)

    @pl.when(i == 0)
    def _():
        ybuf[...] = jnp.zeros_like(ybuf)
        run_copies(0, 0, True)

    @pl.when(i + 1 < n)
    def _():
        run_copies(i + 1, (i + 1) % 2, True)

    run_copies(i, i % 2, False)
    lane = lax.broadcasted_iota(jnp.int32, (MOE_TM, MOE_LMAX), 1)
    pmt = jnp.where(lane == lp_ref[...], 1.0, 0.0).astype(BF16)
    ffn = jnp.dot(pmt, ybuf[i % 2], preferred_element_type=F32)
    xn = x_ref[...] + mod_ref[0][5:6] * ffn
    if final:
        ms = jnp.mean(xn * xn, axis=-1, keepdims=True)
        xn = xn * lax.rsqrt(ms + EPS) * nw_ref[...]
    o_ref[...] = xn


def moe_combine(meta, x, lp, mod, y, nw, final, tpb):
    t, d = x.shape
    tm = MOE_TM
    grid_spec = pltpu.PrefetchScalarGridSpec(
        num_scalar_prefetch=1,
        grid=(t // tm,),
        in_specs=[
            pl.BlockSpec((tm, d), lambda i, m: (i, 0)),
            pl.BlockSpec((tm, 1), lambda i, m: (i, 0)),
            pl.BlockSpec((1, 6, d), lambda i, m: (i // tpb, 0, 0)),
            _resident((1, d)),
            pl.BlockSpec(memory_space=pl.ANY),
        ],
        out_specs=pl.BlockSpec((tm, d), lambda i, m: (i, 0)),
        scratch_shapes=[pltpu.VMEM((2, MOE_LMAX, d), BF16), pltpu.SemaphoreType.DMA((2,))],
    )
    return pl.pallas_call(
        functools.partial(_moe_combine_kernel, final=final),
        grid_spec=grid_spec,
        out_shape=jax.ShapeDtypeStruct((t, d), F32),
        compiler_params=_cparams(("arbitrary",)),
        name="moe_combine",
    )(meta, x, lp, mod, nw.reshape(1, d), y)


def _expert_schedule(tot, cap, nt_max):
    nt_g = tot[:MOE_GROUPS] // MOE_RT
    ends = jnp.cumsum(nt_g)
    starts = ends - nt_g
    nt = ends[MOE_GROUPS - 1]
    ic = jnp.minimum(jnp.arange(nt_max, dtype=jnp.int32), nt - 1)
    tg = jnp.sum((ic[:, None] >= ends[None, :]).astype(jnp.int32), axis=1)
    tr = tg * (cap // MOE_RT) + ic - starts[tg]
    return tg.astype(jnp.int32), tr.astype(jnp.int32), nt.reshape(1).astype(jnp.int32)


def _even_chunks(n):
    cw = 512
    return tuple((c, cw, ((0, cw, 0, None, c),)) for c in range(0, n, cw))


def _router_weights(w_group, b_group, w_expert, b_expert):
    d = w_group.shape[0]
    pad = LANES - MOE_EXPERTS - MOE_GROUPS
    wr = jnp.concatenate([w_expert.reshape(d, MOE_EXPERTS), w_group, jnp.zeros((d, pad), F32)], axis=1)
    br = jnp.concatenate([b_expert.reshape(MOE_EXPERTS), b_group, jnp.zeros((pad,), F32)]).reshape(1, LANES)
    return wr.astype(BF16), br


def kernel(x, c, ada_w, ada_b, norm1_w, norm2_w, mix_w_in, pool_w, pool_scale, rel_bias, mix_w_out, ssm_w_in, ssm_conv_w, ssm_conv_b, ssm_dt_bias, ssm_A_log, ssm_D, ssm_norm_w, ssm_w_out, moe_w_group, moe_b_group, moe_w_expert, moe_b_expert, moe_w1, moe_w3, moe_w2, final_norm_w):
    b, s, d = x.shape
    depth = ada_w.shape[0]
    tm = 512
    mod_all = adaln(c, ada_w, ada_b)
    n_tiles = (b * s) // MOE_TM
    cap = -(-(b * s + n_tiles * MOE_CH) // MOE_RT) * MOE_RT
    nt_max = (b * s + n_tiles * MOE_GROUPS * MOE_CH) // MOE_RT + MOE_GROUPS
    for layer in range(depth):
        mod = mod_all[layer].reshape(b, 6, d)
        i = layer // 2
        if layer % 2 == 0:
            col = jnp.arange(mix_w_in.shape[2])
            q_cols = (col >= POOL_WIDTH) & (col < POOL_WIDTH + ATT_WIDTH)
            w_in = (mix_w_in[i] * jnp.where(q_cols, ATT_HEAD_DIM ** -0.5 * LOG2E, 1.0)).astype(BF16)
            (proj,) = norm_proj(x, norm1_w[layer], mod, w_in,
                                [jax.ShapeDtypeStruct((b, s, w_in.shape[1]), BF16)], _even_chunks(w_in.shape[1]), tm)
            a = even_mixer(proj, _attn_bias_tile(rel_bias[i]), pool_w[i].astype(BF16), pool_scale[i])
            mixed, w_out = a, mix_w_out[i].astype(BF16)
        else:
            z, xs, bm, cm, dt = ssm_in_proj(x, norm1_w[layer], mod, ssm_w_in[i], ssm_conv_w[i], ssm_conv_b[i], tm)
            g = ssd(z, xs, bm, cm, dt, ssm_dt_bias[i], ssm_A_log[i], ssm_D[i], ssm_norm_w[i])
            mixed, w_out = g, ssm_w_out[i].astype(BF16)
        wr, br = _router_weights(moe_w_group[layer], moe_b_group[layer], moe_w_expert[layer], moe_b_expert[layer])
        xk, lp, meta, tot, srt = out_proj_dispatch(mixed, w_out, x, mod, norm2_w[layer], wr, br, cap)
        tg, tr, nt = _expert_schedule(tot, cap, nt_max)
        y = moe_experts(tg, tr, nt, srt, wr, br, moe_w1[layer], moe_w3[layer], moe_w2[layer], nt_max)
        x = moe_combine(meta, xk, lp, mod, y, final_norm_w, layer == depth - 1, s // MOE_TM).reshape(b, s, d)
    return x
```

```python
import functools
import math

import jax
import jax.numpy as jnp
from jax import lax
from jax.experimental import pallas as pl
from jax.experimental.pallas import tpu as pltpu

F32 = jnp.float32
BF16 = jnp.bfloat16

EPS = 1e-6
LANES = 128

POOL_WINDOWS = (2, 4, 8, 16)
POOL_GROUP = 128
POOL_WIDTH = POOL_GROUP * len(POOL_WINDOWS)
POOL_HALO = 16
ATT_HEAD_DIM = 64
ATT_HEADS = 8
ATT_WIDTH = ATT_HEADS * ATT_HEAD_DIM
ATT_CHUNK = 64
ATT_PREV_CHUNKS = 8
ATT_TQ = 256
ATT_PREV = ATT_PREV_CHUNKS * ATT_CHUNK
ATT_TK = ATT_PREV + ATT_TQ
REL_CLIP = 256
NEG_BIG = -1e30

SSM_GROUPS = 8
SSM_HEADS = 32
SSM_HEADS_PER_GROUP = 4
SSM_HEAD_DIM = 64
SSM_D_STATE = 128
SSM_D_INNER = SSM_HEADS * SSM_HEAD_DIM
SSM_GROUP_WIDTH = SSM_HEADS_PER_GROUP * SSM_HEAD_DIM
SSM_D_CONV = 4
PROJ_HALO = 16
SSD_L = 256
LOG2E = math.log2(math.e)

MOE_GROUPS = 4
MOE_EPG = 8
MOE_EXPERTS = 32
MOE_D_FF = 128
MOE_TM = 512
MOE_CH = 32
MOE_LMAX = MOE_TM + MOE_GROUPS * MOE_CH
MOE_RT = 256
MOE_META = 3 * MOE_GROUPS

VMEM_LIMIT = 56 * 1024 * 1024


def _cparams(sem):
    return pltpu.CompilerParams(dimension_semantics=sem, vmem_limit_bytes=VMEM_LIMIT)


def _resident(shape):
    nd = len(shape)
    return pl.BlockSpec(shape, lambda *_: (0,) * nd, pipeline_mode=pl.Buffered(1))


def _silu(v):
    return v * jax.nn.sigmoid(v)


def _rms_mod(xv, nw, shift, scale):
    ms = jnp.mean(xv * xv, axis=-1, keepdims=True)
    y = xv * lax.rsqrt(ms + EPS) * nw
    return y * (1.0 + scale) + shift


def _adaln_kernel(c_ref, w_ref, b_ref, o_ref):
    c = c_ref[...]
    ca = _silu(c).astype(BF16)
    o_ref[0] = jnp.dot(ca, w_ref[0].astype(BF16), preferred_element_type=F32) + b_ref[0]


def adaln(c, ada_w, ada_b):
    depth, d, n = ada_w.shape
    b = c.shape[0]
    tn = 1536
    return pl.pallas_call(
        _adaln_kernel,
        grid=(depth, n // tn),
        in_specs=[
            pl.BlockSpec((b, d), lambda l, j: (0, 0)),
            pl.BlockSpec((1, d, tn), lambda l, j: (l, 0, j)),
            pl.BlockSpec((1, 1, tn), lambda l, j: (l, 0, j)),
        ],
        out_specs=pl.BlockSpec((1, b, tn), lambda l, j: (l, 0, j)),
        out_shape=jax.ShapeDtypeStruct((depth, b, n), F32),
        compiler_params=_cparams(("arbitrary", "arbitrary")),
        name="adaln",
    )(c, ada_w, ada_b.reshape(depth, 1, n))


STAGE_COLS = 512


def _stage_weight(w_hbm, li, w_bf16, stage, sem, scale_ref=None):
    n = w_bf16.shape[1]
    cols = list(range(0, n, STAGE_COLS))

    def copy(ci):
        return pltpu.make_async_copy(w_hbm.at[li, :, pl.ds(cols[ci], STAGE_COLS)], stage.at[ci % 2], sem.at[ci % 2])

    copy(0).start()
    for ci, c in enumerate(cols):
        if ci + 1 < len(cols):
            copy(ci + 1).start()
        copy(ci).wait()
        v = stage[ci % 2]
        if scale_ref is not None:
            v = v * scale_ref[:, c:c + STAGE_COLS]
        w_bf16[:, c:c + STAGE_COLS] = v.astype(BF16)


def _stage_scratch(k, n):
    return [pltpu.VMEM((k, n), BF16), pltpu.VMEM((2, k, STAGE_COLS), F32), pltpu.SemaphoreType.DMA((2,))]


def _norm_proj_kernel(x_ref, nw_ref, mod_ref, wsc_ref, w_hbm, *refs, chunks, li):
    o_refs, (w_ref, stage, sem) = refs[:-3], refs[-3:]

    @pl.when((pl.program_id(0) == 0) & (pl.program_id(1) == 0))
    def _():
        _stage_weight(w_hbm, li, w_ref, stage, sem, wsc_ref)

    m = mod_ref[0]
    h = _rms_mod(x_ref[0], nw_ref[...], m[0:1], m[1:2]).astype(BF16)
    for col, width, stores in chunks:
        r = jnp.dot(h, w_ref[:, col:col + width], preferred_element_type=F32)
        for off, w, oi, g, dst in stores:
            o_ref = o_refs[oi]
            v = r[:, off:off + w].astype(o_ref.dtype)
            if g is None:
                o_ref[0, :, dst:dst + w] = v
            else:
                o_ref[0, g, :, dst:dst + w] = v


def norm_proj(x, nw, mod, w, li, w_scale, out_shapes, chunks, tm):
    b, s, d = x.shape
    assert w.shape[2] % STAGE_COLS == 0
    out_specs = []
    for sh in out_shapes:
        if len(sh.shape) == 3:
            out_specs.append(pl.BlockSpec((1, tm, sh.shape[2]), lambda i, j: (i, j, 0)))
        else:
            out_specs.append(pl.BlockSpec((1, sh.shape[1], tm, sh.shape[3]), lambda i, j: (i, 0, j, 0)))
    return pl.pallas_call(
        functools.partial(_norm_proj_kernel, chunks=chunks, li=li),
        grid=(b, s // tm),
        in_specs=[
            pl.BlockSpec((1, tm, d), lambda i, j: (i, j, 0)),
            _resident((1, d)),
            pl.BlockSpec((1, 6, d), lambda i, j: (i, 0, 0)),
            _resident((1, w.shape[2])),
            pl.BlockSpec(memory_space=pl.ANY),
        ],
        out_specs=out_specs,
        out_shape=out_shapes,
        scratch_shapes=_stage_scratch(w.shape[1], w.shape[2]),
        compiler_params=_cparams(("arbitrary", "arbitrary")),
        name="norm_proj",
    )(x, nw.reshape(1, d), mod, w_scale.reshape(1, -1), w)


def _ssm_in_proj_kernel(x_ref, nw_ref, mod_ref, wdt_ref, cw_ref, cb_ref, w_hbm,
                        z_ref, xs_ref, b_ref, c_ref, dt_ref, halo, w_ref, stage, sem, *, chunks, li):
    assert SSM_D_CONV == 4
    j = pl.program_id(1)

    @pl.when((pl.program_id(0) == 0) & (j == 0))
    def _():
        _stage_weight(w_hbm, li, w_ref, stage, sem)

    tm = x_ref.shape[1]
    hr = PROJ_HALO
    m = mod_ref[0]
    h = _rms_mod(x_ref[0], nw_ref[...], m[0:1], m[1:2]).astype(BF16)
    outs = (z_ref, xs_ref, b_ref, c_ref, dt_ref)
    for col, width, conv_col, stores in chunks:
        wc = w_ref[:, col:col + width] if col < w_ref.shape[1] else wdt_ref[...]
        r = jnp.dot(h, wc, preferred_element_type=F32)
        if conv_col is None:
            res = r
        else:
            prev = jnp.where(j > 0, halo[:, conv_col:conv_col + width], 0.0)
            halo[:, conv_col:conv_col + width] = r[tm - hr:, :]
            re = jnp.concatenate([prev, r], axis=0)
            cw = cw_ref[:, conv_col:conv_col + width]
            u = pltpu.roll(re, 1, 0)
            near = r * cw[3:4] + u[hr:, :] * cw[2:3]
            far = pltpu.roll(re * cw[1:2] + u * cw[0:1], 2, 0)
            res = _silu(cb_ref[:, conv_col:conv_col + width] + near + far[hr:, :])
        for off, w, oi, g in stores:
            o_ref = outs[oi]
            v = res[:, off:off + w].astype(o_ref.dtype)
            if g is None:
                o_ref[0] = v
            else:
                o_ref[0, g] = v


def _ssm_in_proj_chunks():
    gw, n, ng, cw = SSM_GROUP_WIDTH, SSM_D_STATE, SSM_GROUPS, 512
    chunks = []
    col = 0
    for oi, conv0 in ((0, None), (1, 0)):
        for g in range(0, ng, 2):
            conv_col = None if conv0 is None else conv0 + g * gw
            chunks.append((col, cw, conv_col, ((0, gw, oi, g), (gw, gw, oi, g + 1))))
            col += cw
    for oi, conv0 in ((2, SSM_D_INNER), (3, SSM_D_INNER + ng * n)):
        for g in range(0, ng, 4):
            chunks.append((col, cw, conv0 + g * n, tuple((k * n, n, oi, g + k) for k in range(4))))
            col += cw
    chunks.append((col, LANES, None, ((0, LANES, 4, None),)))
    return tuple(chunks), col + LANES


def ssm_in_proj(x, nw, mod, w, li, conv_w, conv_b, tm):
    b, s, d = x.shape
    ng, gw, n = SSM_GROUPS, SSM_GROUP_WIDTH, SSM_D_STATE
    chunks, cols = _ssm_in_proj_chunks()
    n_main = cols - LANES
    assert n_main % STAGE_COLS == 0
    w_dt = jnp.pad(w[li, :, n_main:], ((0, 0), (0, cols - w.shape[2]))).astype(BF16)
    tile4 = lambda wd: pl.BlockSpec((1, ng, tm, wd), lambda i, j: (i, 0, j, 0))
    return pl.pallas_call(
        functools.partial(_ssm_in_proj_kernel, chunks=chunks, li=li),
        grid=(b, s // tm),
        in_specs=[
            pl.BlockSpec((1, tm, d), lambda i, j: (i, j, 0)),
            _resident((1, d)),
            pl.BlockSpec((1, 6, d), lambda i, j: (i, 0, 0)),
            _resident(w_dt.shape),
            _resident(conv_w.shape),
            _resident((1, conv_b.shape[0])),
            pl.BlockSpec(memory_space=pl.ANY),
        ],
        out_specs=[tile4(gw), tile4(gw), tile4(n), tile4(n), pl.BlockSpec((1, tm, LANES), lambda i, j: (i, j, 0))],
        out_shape=[
            jax.ShapeDtypeStruct((b, ng, s, gw), BF16),
            jax.ShapeDtypeStruct((b, ng, s, gw), BF16),
            jax.ShapeDtypeStruct((b, ng, s, n), BF16),
            jax.ShapeDtypeStruct((b, ng, s, n), BF16),
            jax.ShapeDtypeStruct((b, s, LANES), F32),
        ],
        scratch_shapes=[pltpu.VMEM((PROJ_HALO, conv_w.shape[1]), F32)] + _stage_scratch(d, n_main),
        compiler_params=_cparams(("arbitrary", "arbitrary")),
        name="ssm_in_proj",
    )(x, nw.reshape(1, d), mod, w_dt, conv_w, conv_b.reshape(1, -1), w)


def _even_mixer_kernel(uq_ref, kv2_ref, kv1_ref, kv0_ref, halo_ref, bias_ref, pw_ref, ps_ref, o_ref, ubuf):
    j = pl.program_id(1)
    tq = ATT_TQ

    u = uq_ref[0, :, 0:POOL_WIDTH].astype(F32)
    halo = halo_ref[0].astype(F32)
    ubuf[0:POOL_HALO, :] = jnp.where(j > 0, halo, 0.0)
    ubuf[POOL_HALO:POOL_HALO + tq, :] = u
    pos1 = j * tq + 1 + lax.broadcasted_iota(jnp.int32, (tq, 1), 0)
    for g, win in enumerate(POOL_WINDOWS):
        lo = g * POOL_GROUP
        assert win & (win - 1) == 0 and win <= POOL_HALO
        acc = ubuf[:, lo:lo + POOL_GROUP]
        k = 1
        while k < win:
            acc = acc + pltpu.roll(acc, k, 0)
            k *= 2
        acc = acc[POOL_HALO:, :]
        cnt = jnp.minimum(pos1, win).astype(F32)
        pooled = acc * (1.0 / cnt) - u[:, lo:lo + POOL_GROUP]
        y = jnp.dot(pooled.astype(BF16), pw_ref[g], preferred_element_type=F32)
        o_ref[0, :, lo:lo + POOL_GROUP] = (y * ps_ref[:, lo:lo + POOL_GROUP]).astype(o_ref.dtype)

    kpos = j * tq - ATT_PREV + lax.broadcasted_iota(jnp.int32, (1, ATT_TK), 1)
    kvalid = kpos >= 0
    lane_hi = lax.broadcasted_iota(jnp.int32, (1, LANES), 1) >= ATT_HEAD_DIM
    for hp in range(ATT_HEADS // 2):
        c0 = hp * LANES
        q2 = uq_ref[0, :, POOL_WIDTH + c0:POOL_WIDTH + c0 + LANES]
        k2 = jnp.concatenate([r[0, :, c0:c0 + LANES] for r in (kv2_ref, kv1_ref, kv0_ref)], axis=0)
        v2 = jnp.concatenate(
            [r[0, :, ATT_WIDTH + c0:ATT_WIDTH + c0 + LANES] for r in (kv2_ref, kv1_ref, kv0_ref)], axis=0)
        zero = jnp.zeros_like(q2)
        qm = jnp.concatenate([jnp.where(lane_hi, zero, q2), jnp.where(lane_hi, q2, zero)], axis=0)
        sc = lax.dot_general(qm, k2, (((1,), (1,)), ((), ())), preferred_element_type=F32)
        bias2 = bias_ref[2 * hp:2 * hp + 2].reshape(2 * tq, ATT_TK)
        sc = jnp.where(kvalid, sc + bias2, NEG_BIG)
        mx = jnp.max(sc, axis=-1, keepdims=True)
        p = jnp.exp2(sc - mx)
        den = jnp.sum(p, axis=-1, keepdims=True)
        o = jnp.dot(p.astype(BF16), v2, preferred_element_type=F32) * (1.0 / den)
        o_pair = jnp.where(lane_hi, o[tq:, :], o[0:tq, :])
        o_ref[0, :, POOL_WIDTH + c0:POOL_WIDTH + c0 + LANES] = o_pair.astype(o_ref.dtype)


def _attn_bias_tile(rel_bias):
    nh = rel_bias.shape[0]
    i = jnp.arange(ATT_TQ)[:, None]
    s = jnp.arange(ATT_TK)[None, :]
    n_const = ATT_PREV + ATT_TQ - REL_CLIP
    n_tail = ATT_TK + ATT_TQ - 1 - n_const
    v = jnp.concatenate([jnp.broadcast_to(rel_bias[:, 2 * REL_CLIP:], (nh, n_const)),
                         rel_bias[:, 2 * REL_CLIP - n_tail:2 * REL_CLIP][:, ::-1]], axis=1).astype(F32)
    nv = n_const + n_tail
    stream = jnp.broadcast_to(jnp.pad(v, ((0, 0), (0, 1)))[:, None, :], (nh, ATT_TQ, nv + 1)).reshape(nh, -1)
    skew = stream[:, :ATT_TQ * nv].reshape(nh, ATT_TQ, nv)
    bias = skew[:, :, ATT_TQ - 1:ATT_TQ - 1 + ATT_TK] * LOG2E
    band = s - (i // ATT_CHUNK) * ATT_CHUNK
    ok = (band >= 0) & (band < (ATT_PREV_CHUNKS + 1) * ATT_CHUNK)
    return jnp.where(ok[None], bias, NEG_BIG)


def even_mixer(proj, bias_tile, pool_w, pool_scale):
    b, s, n = proj.shape
    tq = ATT_TQ
    half = n // 2
    halo_blocks = tq // POOL_HALO
    return pl.pallas_call(
        _even_mixer_kernel,
        grid=(b, s // tq),
        in_specs=[
            pl.BlockSpec((1, tq, half), lambda i, j: (i, j, 0)),
            pl.BlockSpec((1, tq, half), lambda i, j: (i, jnp.maximum(j - 2, 0), 1)),
            pl.BlockSpec((1, tq, half), lambda i, j: (i, jnp.maximum(j - 1, 0), 1)),
            pl.BlockSpec((1, tq, half), lambda i, j: (i, j, 1)),
            pl.BlockSpec((1, POOL_HALO, POOL_WIDTH), lambda i, j: (i, jnp.maximum(j * halo_blocks - 1, 0), 0)),
            _resident(bias_tile.shape),
            _resident(pool_w.shape),
            _resident((1, POOL_WIDTH)),
        ],
        out_specs=pl.BlockSpec((1, tq, POOL_WIDTH + ATT_WIDTH), lambda i, j: (i, j, 0)),
        out_shape=jax.ShapeDtypeStruct((b, s, POOL_WIDTH + ATT_WIDTH), BF16),
        scratch_shapes=[pltpu.VMEM((POOL_HALO + tq, POOL_WIDTH), F32)],
        compiler_params=_cparams(("parallel", "parallel")),
        name="even_mixer",
    )(proj, proj, proj, proj, proj, bias_tile, pool_w, pool_scale.reshape(1, POOL_WIDTH))


def _group_logits(logits):
    lane = lax.broadcasted_iota(jnp.int32, logits.shape, 1)
    is_g = (lane >= MOE_EXPERTS) & (lane < MOE_EXPERTS + MOE_GROUPS)
    gl = jnp.where(is_g, logits, -jnp.inf)
    return lane, gl, jnp.max(gl, axis=-1, keepdims=True)


def _chunk_copy(src, dst, sem):
    return pltpu.make_async_copy(src, dst, sem)


def _out_proj_dispatch_kernel(a_ref, w_hbm, x_ref, mod_ref, nw_ref, wr_ref, br_ref, ltri_ref,
                              xo_ref, lp_ref, meta_ref, tot_ref, sorted_ref,
                              sbuf, zbuf, sem, run_s, nis_s, w_ref, wstage, wsem, *, grouped, cap, li):
    i = pl.program_id(0)
    n = pl.num_programs(0)
    slot = i % 2
    tm, ch = MOE_TM, MOE_CH

    @pl.when(i == 0)
    def _():
        for g in range(MOE_GROUPS):
            run_s[g] = 0
        nis_s[0] = 0
        nis_s[1] = 0
        zbuf[...] = jnp.zeros_like(zbuf)
        _stage_weight(w_hbm, li, w_ref, wstage, wsem)

    m = mod_ref[0]
    if grouped:
        a_cat = jnp.concatenate([a_ref[0, g] for g in range(a_ref.shape[1])], axis=1)
        mix = jnp.dot(a_cat, w_ref[...], preferred_element_type=F32)
    else:
        mix = jnp.dot(a_ref[...], w_ref[...], preferred_element_type=F32)
    xn = x_ref[...] + m[2:3] * mix
    xo_ref[...] = xn
    h2 = _rms_mod(xn, nw_ref[...], m[3:4], m[4:5]).astype(BF16)

    logits = jnp.dot(h2, wr_ref[...], preferred_element_type=F32) + br_ref[...]
    lane, gl, gmax = _group_logits(logits)
    g_lane = jnp.min(jnp.where(gl == gmax, lane, 2 * LANES), axis=-1, keepdims=True)
    onehot = (lane == g_lane).astype(F32)
    rank = jnp.dot(ltri_ref[...], onehot.astype(BF16), preferred_element_type=F32)
    cnt_row = jnp.sum(onehot, axis=0, keepdims=True)
    lane1 = lax.broadcasted_iota(jnp.int32, (1, LANES), 1)
    nch, los = [], []
    lo = jnp.int32(0)
    lo_row = jnp.zeros((1, LANES), F32)
    for g in range(MOE_GROUPS):
        cnt = jnp.sum(jnp.where(lane1 == MOE_EXPERTS + g, cnt_row, 0.0)).astype(jnp.int32)
        nch.append((cnt + (ch - 1)) // ch)
        los.append(lo)
        lo_row = jnp.where(lane1 == MOE_EXPERTS + g, lo.astype(F32), lo_row)
        lo = lo + nch[g] * ch
    lp = jnp.sum(onehot * (rank + lo_row), axis=-1, keepdims=True)
    lp_ref[...] = lp.astype(jnp.int32)
    lpt = jnp.broadcast_to(lp, (tm, LANES)).T[0:1, :]
    rowi = lax.broadcasted_iota(jnp.int32, (MOE_LMAX, tm), 0).astype(F32)
    pm = jnp.where(rowi == lpt, 1.0, 0.0).astype(BF16)

    def _wait(sl, count):
        def body(c, carry):
            _chunk_copy(sbuf.at[sl, pl.ds(0, ch)], sorted_ref.at[pl.ds(0, ch)], sem.at[sl]).wait()
            return carry
        lax.fori_loop(0, count, body, 0)

    _wait(slot, nis_s[slot])
    sbuf[slot] = jnp.dot(pm, h2, preferred_element_type=F32).astype(BF16)
    total = jnp.int32(0)
    for g in range(MOE_GROUPS):
        dst0 = g * cap + run_s[g]

        def issue(c, carry, g=g, dst0=dst0):
            off = pl.multiple_of(c * ch, ch)
            _chunk_copy(sbuf.at[slot, pl.ds(pl.multiple_of(los[g] + off, ch), ch)],
                        sorted_ref.at[pl.ds(pl.multiple_of(dst0 + off, ch), ch)], sem.at[slot]).start()
            return carry

        lax.fori_loop(0, nch[g], issue, 0)
        meta_ref[i * MOE_META + 3 * g] = dst0
        meta_ref[i * MOE_META + 3 * g + 1] = nch[g]
        meta_ref[i * MOE_META + 3 * g + 2] = los[g]
        run_s[g] = run_s[g] + nch[g] * ch
        total = total + nch[g]
    nis_s[slot] = total

    @pl.when(i == n - 1)
    def _():
        for g in range(MOE_GROUPS):
            r = run_s[g]
            full = (r + (MOE_RT - 1)) // MOE_RT * MOE_RT
            nfill = (full - r) // ch
            base = g * cap + r

            def fill(c, carry, base=base):
                off = pl.multiple_of(c * ch, ch)
                _chunk_copy(zbuf, sorted_ref.at[pl.ds(pl.multiple_of(base + off, ch), ch)], sem.at[2]).start()
                return carry

            def fill_wait(c, carry):
                _chunk_copy(zbuf, sorted_ref.at[pl.ds(0, ch)], sem.at[2]).wait()
                return carry

            lax.fori_loop(0, nfill, fill, 0)
            lax.fori_loop(0, nfill, fill_wait, 0)
            tot_ref[g] = full
            tot_ref[MOE_GROUPS + g] = 0
        _wait(0, nis_s[0])
        _wait(1, nis_s[1])


def out_proj_dispatch(a, w, li, x, mod, nw2, wr, br, cap):
    b, s, d = x.shape
    t = b * s
    tm = MOE_TM
    tpb = s // tm
    nt = t // tm
    grouped = a.ndim == 4
    if grouped:
        a_spec = pl.BlockSpec((1, a.shape[1], tm, a.shape[3]), lambda i: (i // tpb, 0, i % tpb, 0))
    else:
        a = a.reshape(t, a.shape[2])
        a_spec = pl.BlockSpec((tm, a.shape[1]), lambda i: (i, 0))
    ltri = (jnp.arange(tm)[:, None] > jnp.arange(tm)[None, :]).astype(BF16)
    smem = pl.BlockSpec(memory_space=pltpu.SMEM)
    xo, lp, meta, tot, srt = pl.pallas_call(
        functools.partial(_out_proj_dispatch_kernel, grouped=grouped, cap=cap, li=li),
        grid=(nt,),
        in_specs=[
            a_spec,
            pl.BlockSpec(memory_space=pl.ANY),
            pl.BlockSpec((tm, d), lambda i: (i, 0)),
            pl.BlockSpec((1, 6, d), lambda i: (i // tpb, 0, 0)),
            _resident((1, d)),
            _resident(wr.shape),
            _resident(br.shape),
            _resident(ltri.shape),
        ],
        out_specs=[
            pl.BlockSpec((tm, d), lambda i: (i, 0)),
            pl.BlockSpec((tm, 1), lambda i: (i, 0)),
            smem,
            smem,
            pl.BlockSpec(memory_space=pl.ANY),
        ],
        out_shape=[
            jax.ShapeDtypeStruct((t, d), F32),
            jax.ShapeDtypeStruct((t, 1), jnp.int32),
            jax.ShapeDtypeStruct((nt * MOE_META,), jnp.int32),
            jax.ShapeDtypeStruct((2 * MOE_GROUPS,), jnp.int32),
            jax.ShapeDtypeStruct((MOE_GROUPS * cap, d), BF16),
        ],
        scratch_shapes=[
            pltpu.VMEM((2, MOE_LMAX, d), BF16),
            pltpu.VMEM((MOE_CH, d), BF16),
            pltpu.SemaphoreType.DMA((3,)),
            pltpu.SMEM((MOE_GROUPS,), jnp.int32),
            pltpu.SMEM((2,), jnp.int32),
        ] + _stage_scratch(w.shape[1], w.shape[2]),
        compiler_params=_cparams(("arbitrary",)),
        name="out_proj_dispatch",
    )(a, w, x.reshape(t, d), mod, nw2.reshape(1, d), wr, br, ltri)
    return xo, lp, meta, tot, srt


def _ssd_kernel(z_ref, x_ref, b_ref, c_ref, dt_ref, dtb_ref, alog_ref, dskip_ref, nw_ref, o_ref,
                state, acsg, aspt, wt):
    j = pl.program_id(1)
    L = SSD_L

    @pl.when(j == 0)
    def _():
        state[...] = jnp.zeros_like(state)

    dt = jax.nn.softplus(dt_ref[0] + dtb_ref[...])
    a = dt * (-jnp.exp(alog_ref[...]) * LOG2E)
    row = lax.broadcasted_iota(jnp.int32, (L, L), 0)
    col = lax.broadcasted_iota(jnp.int32, (L, L), 1)
    tril = row >= col
    a_hi = a.astype(BF16)
    r1 = a - a_hi.astype(F32)
    a_mid = r1.astype(BF16)
    a_lo = (r1 - a_mid.astype(F32)).astype(BF16)
    parts = jnp.dot(tril.astype(BF16), jnp.concatenate([a_hi, a_mid, a_lo], axis=1), preferred_element_type=F32)
    a_cs = parts[:, 0:LANES] + parts[:, LANES:2 * LANES] + parts[:, 2 * LANES:]
    a_last = a_cs[L - 1:L, :]
    aspt[...] = (a_cs - jnp.log2(dt)).T
    wt[...] = (dt * jnp.exp2(a_last - a_cs)).T
    for gi in range(SSM_GROUPS):
        shift = (LANES - gi * SSM_HEADS_PER_GROUP) % LANES
        acsg[gi] = pltpu.roll(a_cs, shift, 1) if shift else a_cs

    lane_hi = lax.broadcasted_iota(jnp.int32, (1, LANES), 1) >= SSM_HEAD_DIM

    def group_body(g, carry):
        xsb = x_ref[0, g]
        bmb = b_ref[0, g]
        cmb = c_ref[0, g]
        cb = lax.dot_general(cmb, bmb, (((1,), (1,)), ((), ())), preferred_element_type=F32)
        bmt = bmb.astype(F32).T
        st = state[g]
        yoff = jnp.dot(cmb, st.astype(BF16), preferred_element_type=F32)
        zz = z_ref[0, g].astype(F32)
        dsk = dskip_ref[g]
        acs_g = acsg[g]
        gated = []
        for q in range(2):
            c0 = q * LANES
            xs2b = xsb[:, c0:c0 + LANES]
            albs, mrs, lhss, cds = [], [], [], []
            for jj in range(2):
                h = g * SSM_HEADS_PER_GROUP + 2 * q + jj
                r = 2 * q + jj
                alb = jnp.broadcast_to(acs_g[:, r:r + 1], (L, LANES))
                asr = aspt[pl.ds(h, 1), :]
                wr = wt[pl.ds(h, 1), :]
                al2 = jnp.concatenate([alb] * (L // LANES), axis=1)
                seg = jnp.exp2(jnp.where(tril, al2 - asr, -jnp.inf))
                albs.append(alb)
                mrs.append((cb * seg).astype(BF16))
                lhss.append((bmt * wr).astype(BF16))
                cds.append(jnp.exp2(alb[L - 1:L, :]))
            yd = jnp.dot(jnp.concatenate(mrs, axis=0), xs2b, preferred_element_type=F32)
            sn2 = jnp.dot(jnp.concatenate(lhss, axis=0), xs2b, preferred_element_type=F32)
            yo = yoff[:, c0:c0 + LANES]
            ys = [yd[jj * L:(jj + 1) * L, :] + jnp.exp2(albs[jj]) * yo for jj in range(2)]
            y = jnp.where(lane_hi, ys[1], ys[0]) + xs2b.astype(F32) * dsk[:, c0:c0 + LANES]
            sn = jnp.where(lane_hi, sn2[SSM_D_STATE:, :], sn2[0:SSM_D_STATE, :])
            cd = jnp.where(lane_hi, cds[1], cds[0])
            state[g, :, c0:c0 + LANES] = st[:, c0:c0 + LANES] * cd + sn
            gated.append(y * _silu(zz[:, c0:c0 + LANES]))
        ss = jnp.sum(gated[0] * gated[0], axis=-1, keepdims=True) + jnp.sum(gated[1] * gated[1], axis=-1, keepdims=True)
        rs = lax.rsqrt(ss / SSM_GROUP_WIDTH + EPS)
        nw = nw_ref[g]
        for q in range(2):
            c0 = q * LANES
            o_ref[0, g, :, c0:c0 + LANES] = (gated[q] * rs * nw[:, c0:c0 + LANES]).astype(o_ref.dtype)
        return carry

    lax.fori_loop(0, SSM_GROUPS, group_body, 0)


def ssd(z, xs, bm, cm, dt, dt_bias, a_log, d_skip, norm_w):
    b, ng, s, gw = z.shape
    L = SSD_L
    n = SSM_D_STATE
    pad = LANES - SSM_HEADS
    dtb = jnp.pad(dt_bias, (0, pad)).reshape(1, LANES)
    alog = jnp.pad(a_log, (0, pad)).reshape(1, LANES)
    dsk = jnp.repeat(d_skip, SSM_HEAD_DIM).reshape(ng, 1, gw)
    nw = norm_w.reshape(ng, 1, gw)
    tile4 = lambda w: pl.BlockSpec((1, ng, L, w), lambda i, j: (i, 0, j, 0))
    return pl.pallas_call(
        _ssd_kernel,
        grid=(b, s // L),
        in_specs=[
            tile4(gw), tile4(gw), tile4(n), tile4(n),
            pl.BlockSpec((1, L, LANES), lambda i, j: (i, j, 0)),
            _resident(dtb.shape), _resident(alog.shape), _resident(dsk.shape), _resident(nw.shape),
        ],
        out_specs=tile4(gw),
        out_shape=jax.ShapeDtypeStruct((b, ng, s, gw), BF16),
        scratch_shapes=[
            pltpu.VMEM((ng, n, gw), F32),
            pltpu.VMEM((ng, L, LANES), F32),
            pltpu.VMEM((LANES, L), F32),
            pltpu.VMEM((LANES, L), F32),
        ],
        compiler_params=_cparams(("parallel", "arbitrary")),
        name="ssd",
    )(z, xs, bm, cm, dt, dtb, alog, dsk, nw)


def _route_in_group(logits, g):
    lane, gl, gmax = _group_logits(logits)
    gsum = jnp.sum(jnp.exp(gl - gmax), axis=-1, keepdims=True)
    lg = jnp.sum(jnp.where(lane == MOE_EXPERTS + g, logits, 0.0), axis=-1, keepdims=True)
    g_val = jnp.exp(lg - gmax) / gsum
    in_grp = (lane >= g * MOE_EPG) & (lane < (g + 1) * MOE_EPG)
    el = jnp.where(in_grp, logits, -jnp.inf)
    emax = jnp.max(el, axis=-1, keepdims=True)
    ee = jnp.exp(el - emax)
    p = ee * (1.0 / jnp.sum(ee, axis=-1, keepdims=True))
    p = jnp.where(in_grp, p, -1.0)
    m1 = jnp.max(p, axis=-1, keepdims=True)
    i1 = jnp.min(jnp.where(p == m1, lane, 2 * LANES), axis=-1, keepdims=True)
    p2 = jnp.where(lane == i1, -1.0, p)
    m2 = jnp.max(p2, axis=-1, keepdims=True)
    i2 = jnp.min(jnp.where(p2 == m2, lane, 2 * LANES), axis=-1, keepdims=True)
    tot = m1 + m2
    w1 = g_val * (m1 / tot)
    w2 = g_val * (m2 / tot)
    return lane, jnp.where(lane == i1, w1, 0.0) + jnp.where(lane == i2, w2, 0.0)


def _moe_expert_kernel(tg_ref, tr_ref, nt_ref, rows_ref, wr_ref, br_ref, w1_ref, w3_ref, w2_ref, y_ref,
                       w1s, w3s, w2s):
    i = pl.program_id(0)
    g = tg_ref[i]

    @pl.when((i == 0) | (g != tg_ref[jnp.maximum(i - 1, 0)]))
    def _():
        for q in range(MOE_EPG):
            w1s[:, q * MOE_D_FF:(q + 1) * MOE_D_FF] = w1_ref[q].astype(BF16)
            w3s[:, q * MOE_D_FF:(q + 1) * MOE_D_FF] = w3_ref[q].astype(BF16)
            w2s[q * MOE_D_FF:(q + 1) * MOE_D_FF, :] = w2_ref[q].astype(BF16)

    @pl.when(i < nt_ref[0])
    def _():
        rows = rows_ref[...]
        logits = jnp.dot(rows, wr_ref[...], preferred_element_type=F32) + br_ref[...]
        lane, comb = _route_in_group(logits, g)
        a = jnp.dot(rows, w1s[...], preferred_element_type=F32)
        b = jnp.dot(rows, w3s[...], preferred_element_type=F32)
        hid = _silu(a) * b
        parts = []
        for q in range(MOE_EPG):
            cw = jnp.sum(jnp.where(lane == g * MOE_EPG + q, comb, 0.0), axis=-1, keepdims=True)
            parts.append((hid[:, q * MOE_D_FF:(q + 1) * MOE_D_FF] * cw).astype(BF16))
        hb = jnp.concatenate(parts, axis=1)
        y_ref[...] = jnp.dot(hb, w2s[...], preferred_element_type=F32).astype(y_ref.dtype)


def moe_experts(tg, tr, nt, srt, wr, br, w1, w3, w2, nt_max):
    rows, d = srt.shape
    gw = MOE_EPG * MOE_D_FF
    grid_spec = pltpu.PrefetchScalarGridSpec(
        num_scalar_prefetch=3,
        grid=(nt_max,),
        in_specs=[
            pl.BlockSpec((MOE_RT, d), lambda i, tg, tr, nt: (tr[i], 0)),
            _resident(wr.shape),
            _resident(br.shape),
            pl.BlockSpec((MOE_EPG, d, MOE_D_FF), lambda i, tg, tr, nt: (tg[i], 0, 0)),
            pl.BlockSpec((MOE_EPG, d, MOE_D_FF), lambda i, tg, tr, nt: (tg[i], 0, 0)),
            pl.BlockSpec((MOE_EPG, MOE_D_FF, d), lambda i, tg, tr, nt: (tg[i], 0, 0)),
        ],
        out_specs=pl.BlockSpec((MOE_RT, d), lambda i, tg, tr, nt: (tr[i], 0)),
        scratch_shapes=[pltpu.VMEM((d, gw), BF16), pltpu.VMEM((d, gw), BF16), pltpu.VMEM((gw, d), BF16)],
    )
    return pl.pallas_call(
        _moe_expert_kernel,
        grid_spec=grid_spec,
        out_shape=jax.ShapeDtypeStruct((rows, d), BF16),
        compiler_params=_cparams(("arbitrary",)),
        name="moe_experts",
    )(tg, tr, nt, srt, wr, br, w1, w3, w2)


def _moe_combine_kernel(meta_ref, x_ref, lp_ref, mod_ref, nw_ref, y_ref, o_ref, ybuf, sem, *, final):
    i = pl.program_id(0)
    n = pl.num_programs(0)
    ch = MOE_CH

    def run_copies(tile, sl, start):
        for g in range(MOE_GROUPS):
            dst0 = meta_ref[tile * MOE_META + 3 * g]
            nch = meta_ref[tile * MOE_META + 3 * g + 1]
            lo = meta_ref[tile * MOE_META + 3 * g + 2]

            def body(c, carry, dst0=dst0, lo=lo):
                off = pl.multiple_of(c * ch, ch)
                cp = _chunk_copy(y_ref.at[pl.ds(pl.multiple_of(dst0 + off, ch), ch)],
                                 ybuf.at[sl, pl.ds(pl.multiple_of(lo + off, ch), ch)], sem.at[sl])
                if start:
                    cp.start()
                else:
                    cp.wait()
                return carry

            lax.fori_loop(0, nch, body, 0)

    @pl.when(i == 0)
    def _():
        ybuf[...] = jnp.zeros_like(ybuf)
        run_copies(0, 0, True)

    @pl.when(i + 1 < n)
    def _():
        run_copies(i + 1, (i + 1) % 2, True)

    run_copies(i, i % 2, False)
    lane = lax.broadcasted_iota(jnp.int32, (MOE_TM, MOE_LMAX), 1)
    pmt = jnp.where(lane == lp_ref[...], 1.0, 0.0).astype(BF16)
    ffn = jnp.dot(pmt, ybuf[i % 2], preferred_element_type=F32)
    xn = x_ref[...] + mod_ref[0][5:6] * ffn
    if final:
        ms = jnp.mean(xn * xn, axis=-1, keepdims=True)
        xn = xn * lax.rsqrt(ms + EPS) * nw_ref[...]
    o_ref[...] = xn


def moe_combine(meta, x, lp, mod, y, nw, final, tpb):
    t, d = x.shape
    tm = MOE_TM
    grid_spec = pltpu.PrefetchScalarGridSpec(
        num_scalar_prefetch=1,
        grid=(t // tm,),
        in_specs=[
            pl.BlockSpec((tm, d), lambda i, m: (i, 0)),
            pl.BlockSpec((tm, 1), lambda i, m: (i, 0)),
            pl.BlockSpec((1, 6, d), lambda i, m: (i // tpb, 0, 0)),
            _resident((1, d)),
            pl.BlockSpec(memory_space=pl.ANY),
        ],
        out_specs=pl.BlockSpec((tm, d), lambda i, m: (i, 0)),
        scratch_shapes=[pltpu.VMEM((2, MOE_LMAX, d), BF16), pltpu.SemaphoreType.DMA((2,))],
    )
    return pl.pallas_call(
        functools.partial(_moe_combine_kernel, final=final),
        grid_spec=grid_spec,
        out_shape=jax.ShapeDtypeStruct((t, d), F32),
        compiler_params=_cparams(("arbitrary",)),
        name="moe_combine",
    )(meta, x, lp, mod, nw.reshape(1, d), y)


def _expert_schedule(tot, cap, nt_max):
    nt_g = tot[:MOE_GROUPS] // MOE_RT
    ends = jnp.cumsum(nt_g)
    starts = ends - nt_g
    nt = ends[MOE_GROUPS - 1]
    ic = jnp.minimum(jnp.arange(nt_max, dtype=jnp.int32), nt - 1)
    tg = jnp.sum((ic[:, None] >= ends[None, :]).astype(jnp.int32), axis=1)
    tr = tg * (cap // MOE_RT) + ic - starts[tg]
    return tg.astype(jnp.int32), tr.astype(jnp.int32), nt.reshape(1).astype(jnp.int32)


def _even_chunks(n):
    cw = 512
    return tuple((c, cw, ((0, cw, 0, None, c),)) for c in range(0, n, cw))


def _router_weights(w_group, b_group, w_expert, b_expert):
    d = w_group.shape[0]
    pad = LANES - MOE_EXPERTS - MOE_GROUPS
    wr = jnp.concatenate([w_expert.reshape(d, MOE_EXPERTS), w_group, jnp.zeros((d, pad), F32)], axis=1)
    br = jnp.concatenate([b_expert.reshape(MOE_EXPERTS), b_group, jnp.zeros((pad,), F32)]).reshape(1, LANES)
    return wr.astype(BF16), br


def kernel(x, c, ada_w, ada_b, norm1_w, norm2_w, mix_w_in, pool_w, pool_scale, rel_bias, mix_w_out, ssm_w_in, ssm_conv_w, ssm_conv_b, ssm_dt_bias, ssm_A_log, ssm_D, ssm_norm_w, ssm_w_out, moe_w_group, moe_b_group, moe_w_expert, moe_b_expert, moe_w1, moe_w3, moe_w2, final_norm_w):
    b, s, d = x.shape
    depth = ada_w.shape[0]
    tm = 512
    mod_all = adaln(c, ada_w, ada_b)
    n_tiles = (b * s) // MOE_TM
    cap = -(-(b * s + n_tiles * MOE_CH) // MOE_RT) * MOE_RT
    nt_max = (b * s + n_tiles * MOE_GROUPS * MOE_CH) // MOE_RT + MOE_GROUPS
    for layer in range(depth):
        mod = mod_all[layer].reshape(b, 6, d)
        i = layer // 2
        if layer % 2 == 0:
            col = jnp.arange(mix_w_in.shape[2])
            q_cols = (col >= POOL_WIDTH) & (col < POOL_WIDTH + ATT_WIDTH)
            w_scale = jnp.where(q_cols, ATT_HEAD_DIM ** -0.5 * LOG2E, 1.0).astype(F32)
            n_in = mix_w_in.shape[2]
            (proj,) = norm_proj(x, norm1_w[layer], mod, mix_w_in, i, w_scale,
                                [jax.ShapeDtypeStruct((b, s, n_in), BF16)], _even_chunks(n_in), tm)
            a = even_mixer(proj, _attn_bias_tile(rel_bias[i]), pool_w[i].astype(BF16), pool_scale[i])
            mixed, w_out = a, mix_w_out
        else:
            z, xs, bm, cm, dt = ssm_in_proj(x, norm1_w[layer], mod, ssm_w_in, i, ssm_conv_w[i], ssm_conv_b[i], tm)
            g = ssd(z, xs, bm, cm, dt, ssm_dt_bias[i], ssm_A_log[i], ssm_D[i], ssm_norm_w[i])
            mixed, w_out = g, ssm_w_out
        wr, br = _router_weights(moe_w_group[layer], moe_b_group[layer], moe_w_expert[layer], moe_b_expert[layer])
        xk, lp, meta, tot, srt = out_proj_dispatch(mixed, w_out, i, x, mod, norm2_w[layer], wr, br, cap)
        tg, tr, nt = _expert_schedule(tot, cap, nt_max)
        y = moe_experts(tg, tr, nt, srt, wr, br, moe_w1[layer], moe_w3[layer], moe_w2[layer], nt_max)
        x = moe_combine(meta, xk, lp, mod, y, final_norm_w, layer == depth - 1, s // MOE_TM).reshape(b, s, d)
    return x
```

```python
import functools
import math

import jax
import jax.numpy as jnp
from jax import lax
from jax.experimental import pallas as pl
from jax.experimental.pallas import tpu as pltpu

F32 = jnp.float32
BF16 = jnp.bfloat16

EPS = 1e-6
LANES = 128

POOL_WINDOWS = (2, 4, 8, 16)
POOL_GROUP = 128
POOL_WIDTH = POOL_GROUP * len(POOL_WINDOWS)
POOL_HALO = 16
ATT_HEAD_DIM = 64
ATT_HEADS = 8
ATT_WIDTH = ATT_HEADS * ATT_HEAD_DIM
ATT_CHUNK = 64
ATT_PREV_CHUNKS = 8
ATT_TQ = 256
ATT_PREV = ATT_PREV_CHUNKS * ATT_CHUNK
ATT_TK = ATT_PREV + ATT_TQ
REL_CLIP = 256
NEG_BIG = -1e30

SSM_GROUPS = 8
SSM_HEADS = 32
SSM_HEADS_PER_GROUP = 4
SSM_HEAD_DIM = 64
SSM_D_STATE = 128
SSM_D_INNER = SSM_HEADS * SSM_HEAD_DIM
SSM_GROUP_WIDTH = SSM_HEADS_PER_GROUP * SSM_HEAD_DIM
SSM_D_CONV = 4
PROJ_HALO = 16
SSD_L = 256
LOG2E = math.log2(math.e)

MOE_GROUPS = 4
MOE_EPG = 8
MOE_EXPERTS = 32
MOE_D_FF = 128
MOE_TM = 512
MOE_CH = 32
MOE_LMAX = MOE_TM + MOE_GROUPS * MOE_CH
MOE_RT = 256
MOE_META = 3 * MOE_GROUPS

VMEM_LIMIT = 56 * 1024 * 1024


def _cparams(sem):
    return pltpu.CompilerParams(dimension_semantics=sem, vmem_limit_bytes=VMEM_LIMIT)


def _resident(shape):
    nd = len(shape)
    return pl.BlockSpec(shape, lambda *_: (0,) * nd, pipeline_mode=pl.Buffered(1))


def _silu(v):
    return v * jax.nn.sigmoid(v)


def _rms_mod(xv, nw, shift, scale):
    ms = jnp.mean(xv * xv, axis=-1, keepdims=True)
    y = xv * lax.rsqrt(ms + EPS) * nw
    return y * (1.0 + scale) + shift


def _adaln_kernel(c_ref, w_ref, b_ref, o_ref):
    c = c_ref[...]
    ca = _silu(c).astype(BF16)
    o_ref[0] = jnp.dot(ca, w_ref[0].astype(BF16), preferred_element_type=F32) + b_ref[0]


def adaln(c, ada_w, ada_b):
    depth, d, n = ada_w.shape
    b = c.shape[0]
    tn = 1536
    return pl.pallas_call(
        _adaln_kernel,
        grid=(depth, n // tn),
        in_specs=[
            pl.BlockSpec((b, d), lambda l, j: (0, 0)),
            pl.BlockSpec((1, d, tn), lambda l, j: (l, 0, j)),
            pl.BlockSpec((1, 1, tn), lambda l, j: (l, 0, j)),
        ],
        out_specs=pl.BlockSpec((1, b, tn), lambda l, j: (l, 0, j)),
        out_shape=jax.ShapeDtypeStruct((depth, b, n), F32),
        compiler_params=_cparams(("arbitrary", "arbitrary")),
        name="adaln",
    )(c, ada_w, ada_b.reshape(depth, 1, n))


def _norm_proj_kernel(x_ref, nw_ref, mod_ref, w_ref, *o_refs, chunks):
    m = mod_ref[0]
    h = _rms_mod(x_ref[0], nw_ref[...], m[0:1], m[1:2]).astype(BF16)
    for col, width, stores in chunks:
        r = jnp.dot(h, w_ref[:, col:col + width], preferred_element_type=F32)
        for off, w, oi, g, dst in stores:
            o_ref = o_refs[oi]
            v = r[:, off:off + w].astype(o_ref.dtype)
            if g is None:
                o_ref[0, :, dst:dst + w] = v
            else:
                o_ref[0, g, :, dst:dst + w] = v


def norm_proj(x, nw, mod, w, out_shapes, chunks, tm):
    b, s, d = x.shape
    out_specs = []
    for sh in out_shapes:
        if len(sh.shape) == 3:
            out_specs.append(pl.BlockSpec((1, tm, sh.shape[2]), lambda i, j: (i, j, 0)))
        else:
            out_specs.append(pl.BlockSpec((1, sh.shape[1], tm, sh.shape[3]), lambda i, j: (i, 0, j, 0)))
    return pl.pallas_call(
        functools.partial(_norm_proj_kernel, chunks=chunks),
        grid=(b, s // tm),
        in_specs=[
            pl.BlockSpec((1, tm, d), lambda i, j: (i, j, 0)),
            _resident((1, d)),
            pl.BlockSpec((1, 6, d), lambda i, j: (i, 0, 0)),
            _resident(w.shape),
        ],
        out_specs=out_specs,
        out_shape=out_shapes,
        compiler_params=_cparams(("parallel", "parallel")),
        name="norm_proj",
    )(x, nw.reshape(1, d), mod, w)


def _ssm_in_proj_kernel(x_ref, nw_ref, mod_ref, w_ref, cw_ref, cb_ref,
                        z_ref, xs_ref, b_ref, c_ref, dt_ref, halo, *, chunks):
    assert SSM_D_CONV == 4
    j = pl.program_id(1)
    tm = x_ref.shape[1]
    hr = PROJ_HALO
    m = mod_ref[0]
    h = _rms_mod(x_ref[0], nw_ref[...], m[0:1], m[1:2]).astype(BF16)
    outs = (z_ref, xs_ref, b_ref, c_ref, dt_ref)
    for col, width, conv_col, stores in chunks:
        r = jnp.dot(h, w_ref[:, col:col + width], preferred_element_type=F32)
        if conv_col is None:
            res = r
        else:
            prev = jnp.where(j > 0, halo[:, conv_col:conv_col + width], 0.0)
            halo[:, conv_col:conv_col + width] = r[tm - hr:, :]
            re = jnp.concatenate([prev, r], axis=0)
            cw = cw_ref[:, conv_col:conv_col + width]
            u = pltpu.roll(re, 1, 0)
            near = r * cw[3:4] + u[hr:, :] * cw[2:3]
            far = pltpu.roll(re * cw[1:2] + u * cw[0:1], 2, 0)
            res = _silu(cb_ref[:, conv_col:conv_col + width] + near + far[hr:, :])
        for off, w, oi, g in stores:
            o_ref = outs[oi]
            v = res[:, off:off + w].astype(o_ref.dtype)
            if g is None:
                o_ref[0] = v
            else:
                o_ref[0, g] = v


def _ssm_in_proj_chunks():
    gw, n, ng, cw = SSM_GROUP_WIDTH, SSM_D_STATE, SSM_GROUPS, 512
    chunks = []
    col = 0
    for oi, conv0 in ((0, None), (1, 0)):
        for g in range(0, ng, 2):
            conv_col = None if conv0 is None else conv0 + g * gw
            chunks.append((col, cw, conv_col, ((0, gw, oi, g), (gw, gw, oi, g + 1))))
            col += cw
    for oi, conv0 in ((2, SSM_D_INNER), (3, SSM_D_INNER + ng * n)):
        for g in range(0, ng, 4):
            chunks.append((col, cw, conv0 + g * n, tuple((k * n, n, oi, g + k) for k in range(4))))
            col += cw
    chunks.append((col, LANES, None, ((0, LANES, 4, None),)))
    return tuple(chunks), col + LANES


def ssm_in_proj(x, nw, mod, w, conv_w, conv_b, tm):
    b, s, d = x.shape
    ng, gw, n = SSM_GROUPS, SSM_GROUP_WIDTH, SSM_D_STATE
    chunks, cols = _ssm_in_proj_chunks()
    w = jnp.pad(w, ((0, 0), (0, cols - w.shape[1]))).astype(BF16)
    tile4 = lambda wd: pl.BlockSpec((1, ng, tm, wd), lambda i, j: (i, 0, j, 0))
    return pl.pallas_call(
        functools.partial(_ssm_in_proj_kernel, chunks=chunks),
        grid=(b, s // tm),
        in_specs=[
            pl.BlockSpec((1, tm, d), lambda i, j: (i, j, 0)),
            _resident((1, d)),
            pl.BlockSpec((1, 6, d), lambda i, j: (i, 0, 0)),
            _resident(w.shape),
            _resident(conv_w.shape),
            _resident((1, conv_b.shape[0])),
        ],
        out_specs=[tile4(gw), tile4(gw), tile4(n), tile4(n), pl.BlockSpec((1, tm, LANES), lambda i, j: (i, j, 0))],
        out_shape=[
            jax.ShapeDtypeStruct((b, ng, s, gw), BF16),
            jax.ShapeDtypeStruct((b, ng, s, gw), BF16),
            jax.ShapeDtypeStruct((b, ng, s, n), BF16),
            jax.ShapeDtypeStruct((b, ng, s, n), BF16),
            jax.ShapeDtypeStruct((b, s, LANES), F32),
        ],
        scratch_shapes=[pltpu.VMEM((PROJ_HALO, conv_w.shape[1]), F32)],
        compiler_params=_cparams(("parallel", "arbitrary")),
        name="ssm_in_proj",
    )(x, nw.reshape(1, d), mod, w, conv_w, conv_b.reshape(1, -1))


def _even_mixer_kernel(uq_ref, kv2_ref, kv1_ref, kv0_ref, halo_ref, bias_ref, pw_ref, ps_ref, o_ref, ubuf):
    j = pl.program_id(1)
    tq = ATT_TQ

    u = uq_ref[0, :, 0:POOL_WIDTH].astype(F32)
    halo = halo_ref[0].astype(F32)
    ubuf[0:POOL_HALO, :] = jnp.where(j > 0, halo, 0.0)
    ubuf[POOL_HALO:POOL_HALO + tq, :] = u
    pos1 = j * tq + 1 + lax.broadcasted_iota(jnp.int32, (tq, 1), 0)
    for g, win in enumerate(POOL_WINDOWS):
        lo = g * POOL_GROUP
        assert win & (win - 1) == 0 and win <= POOL_HALO
        acc = ubuf[:, lo:lo + POOL_GROUP]
        k = 1
        while k < win:
            acc = acc + pltpu.roll(acc, k, 0)
            k *= 2
        acc = acc[POOL_HALO:, :]
        cnt = jnp.minimum(pos1, win).astype(F32)
        pooled = acc * (1.0 / cnt) - u[:, lo:lo + POOL_GROUP]
        y = jnp.dot(pooled.astype(BF16), pw_ref[g], preferred_element_type=F32)
        o_ref[0, :, lo:lo + POOL_GROUP] = (y * ps_ref[:, lo:lo + POOL_GROUP]).astype(o_ref.dtype)

    kpos = j * tq - ATT_PREV + lax.broadcasted_iota(jnp.int32, (1, ATT_TK), 1)
    kvalid = kpos >= 0
    lane_hi = lax.broadcasted_iota(jnp.int32, (1, LANES), 1) >= ATT_HEAD_DIM
    for hp in range(ATT_HEADS // 2):
        c0 = hp * LANES
        q2 = uq_ref[0, :, POOL_WIDTH + c0:POOL_WIDTH + c0 + LANES]
        k2 = jnp.concatenate([r[0, :, c0:c0 + LANES] for r in (kv2_ref, kv1_ref, kv0_ref)], axis=0)
        v2 = jnp.concatenate(
            [r[0, :, ATT_WIDTH + c0:ATT_WIDTH + c0 + LANES] for r in (kv2_ref, kv1_ref, kv0_ref)], axis=0)
        zero = jnp.zeros_like(q2)
        qm = jnp.concatenate([jnp.where(lane_hi, zero, q2), jnp.where(lane_hi, q2, zero)], axis=0)
        sc = lax.dot_general(qm, k2, (((1,), (1,)), ((), ())), preferred_element_type=F32)
        bias2 = bias_ref[2 * hp:2 * hp + 2].reshape(2 * tq, ATT_TK)
        sc = jnp.where(kvalid, sc + bias2, NEG_BIG)
        mx = jnp.max(sc, axis=-1, keepdims=True)
        p = jnp.exp2(sc - mx)
        den = jnp.sum(p, axis=-1, keepdims=True)
        o = jnp.dot(p.astype(BF16), v2, preferred_element_type=F32) * (1.0 / den)
        o_pair = jnp.where(lane_hi, o[tq:, :], o[0:tq, :])
        o_ref[0, :, POOL_WIDTH + c0:POOL_WIDTH + c0 + LANES] = o_pair.astype(o_ref.dtype)


def _attn_bias_tile(rel_bias):
    nh = rel_bias.shape[0]
    i = jnp.arange(ATT_TQ)[:, None]
    s = jnp.arange(ATT_TK)[None, :]
    n_const = ATT_PREV + ATT_TQ - REL_CLIP
    n_tail = ATT_TK + ATT_TQ - 1 - n_const
    v = jnp.concatenate([jnp.broadcast_to(rel_bias[:, 2 * REL_CLIP:], (nh, n_const)),
                         rel_bias[:, 2 * REL_CLIP - n_tail:2 * REL_CLIP][:, ::-1]], axis=1).astype(F32)
    nv = n_const + n_tail
    stream = jnp.broadcast_to(jnp.pad(v, ((0, 0), (0, 1)))[:, None, :], (nh, ATT_TQ, nv + 1)).reshape(nh, -1)
    skew = stream[:, :ATT_TQ * nv].reshape(nh, ATT_TQ, nv)
    bias = skew[:, :, ATT_TQ - 1:ATT_TQ - 1 + ATT_TK] * LOG2E
    band = s - (i // ATT_CHUNK) * ATT_CHUNK
    ok = (band >= 0) & (band < (ATT_PREV_CHUNKS + 1) * ATT_CHUNK)
    return jnp.where(ok[None], bias, NEG_BIG)


def even_mixer(proj, bias_tile, pool_w, pool_scale):
    b, s, n = proj.shape
    tq = ATT_TQ
    half = n // 2
    halo_blocks = tq // POOL_HALO
    return pl.pallas_call(
        _even_mixer_kernel,
        grid=(b, s // tq),
        in_specs=[
            pl.BlockSpec((1, tq, half), lambda i, j: (i, j, 0)),
            pl.BlockSpec((1, tq, half), lambda i, j: (i, jnp.maximum(j - 2, 0), 1)),
            pl.BlockSpec((1, tq, half), lambda i, j: (i, jnp.maximum(j - 1, 0), 1)),
            pl.BlockSpec((1, tq, half), lambda i, j: (i, j, 1)),
            pl.BlockSpec((1, POOL_HALO, POOL_WIDTH), lambda i, j: (i, jnp.maximum(j * halo_blocks - 1, 0), 0)),
            _resident(bias_tile.shape),
            _resident(pool_w.shape),
            _resident((1, POOL_WIDTH)),
        ],
        out_specs=pl.BlockSpec((1, tq, POOL_WIDTH + ATT_WIDTH), lambda i, j: (i, j, 0)),
        out_shape=jax.ShapeDtypeStruct((b, s, POOL_WIDTH + ATT_WIDTH), BF16),
        scratch_shapes=[pltpu.VMEM((POOL_HALO + tq, POOL_WIDTH), F32)],
        compiler_params=_cparams(("parallel", "parallel")),
        name="even_mixer",
    )(proj, proj, proj, proj, proj, bias_tile, pool_w, pool_scale.reshape(1, POOL_WIDTH))


def _group_logits(logits):
    lane = lax.broadcasted_iota(jnp.int32, logits.shape, 1)
    is_g = (lane >= MOE_EXPERTS) & (lane < MOE_EXPERTS + MOE_GROUPS)
    gl = jnp.where(is_g, logits, -jnp.inf)
    return lane, gl, jnp.max(gl, axis=-1, keepdims=True)


def _chunk_copy(src, dst, sem):
    return pltpu.make_async_copy(src, dst, sem)


def _out_proj_dispatch_kernel(a_ref, w_ref, x_ref, mod_ref, nw_ref, wr_ref, br_ref, ltri_ref,
                              xo_ref, lp_ref, meta_ref, tot_ref, sorted_ref,
                              sbuf, zbuf, sem, run_s, nis_s, *, grouped, cap):
    i = pl.program_id(0)
    n = pl.num_programs(0)
    slot = i % 2
    tm, ch = MOE_TM, MOE_CH

    @pl.when(i == 0)
    def _():
        for g in range(MOE_GROUPS):
            run_s[g] = 0
        nis_s[0] = 0
        nis_s[1] = 0
        zbuf[...] = jnp.zeros_like(zbuf)

    m = mod_ref[0]
    if grouped:
        a_cat = jnp.concatenate([a_ref[0, g] for g in range(a_ref.shape[1])], axis=1)
        mix = jnp.dot(a_cat, w_ref[...], preferred_element_type=F32)
    else:
        mix = jnp.dot(a_ref[...], w_ref[...], preferred_element_type=F32)
    xn = x_ref[...] + m[2:3] * mix
    xo_ref[...] = xn
    h2 = _rms_mod(xn, nw_ref[...], m[3:4], m[4:5]).astype(BF16)

    logits = jnp.dot(h2, wr_ref[...], preferred_element_type=F32) + br_ref[...]
    lane, gl, gmax = _group_logits(logits)
    g_lane = jnp.min(jnp.where(gl == gmax, lane, 2 * LANES), axis=-1, keepdims=True)
    onehot = (lane == g_lane).astype(F32)
    rank = jnp.dot(ltri_ref[...], onehot.astype(BF16), preferred_element_type=F32)
    cnt_row = jnp.sum(onehot, axis=0, keepdims=True)
    lane1 = lax.broadcasted_iota(jnp.int32, (1, LANES), 1)
    nch, los = [], []
    lo = jnp.int32(0)
    lo_row = jnp.zeros((1, LANES), F32)
    for g in range(MOE_GROUPS):
        cnt = jnp.sum(jnp.where(lane1 == MOE_EXPERTS + g, cnt_row, 0.0)).astype(jnp.int32)
        nch.append((cnt + (ch - 1)) // ch)
        los.append(lo)
        lo_row = jnp.where(lane1 == MOE_EXPERTS + g, lo.astype(F32), lo_row)
        lo = lo + nch[g] * ch
    lp = jnp.sum(onehot * (rank + lo_row), axis=-1, keepdims=True)
    lp_ref[...] = lp.astype(jnp.int32)
    lpt = jnp.broadcast_to(lp, (tm, LANES)).T[0:1, :]
    rowi = lax.broadcasted_iota(jnp.int32, (MOE_LMAX, tm), 0).astype(F32)
    pm = jnp.where(rowi == lpt, 1.0, 0.0).astype(BF16)

    def _wait(sl, count):
        def body(c, carry):
            _chunk_copy(sbuf.at[sl, pl.ds(0, ch)], sorted_ref.at[pl.ds(0, ch)], sem.at[sl]).wait()
            return carry
        lax.fori_loop(0, count, body, 0)

    _wait(slot, nis_s[slot])
    sbuf[slot] = jnp.dot(pm, h2, preferred_element_type=F32).astype(BF16)
    total = jnp.int32(0)
    for g in range(MOE_GROUPS):
        dst0 = g * cap + run_s[g]

        def issue(c, carry, g=g, dst0=dst0):
            off = pl.multiple_of(c * ch, ch)
            _chunk_copy(sbuf.at[slot, pl.ds(pl.multiple_of(los[g] + off, ch), ch)],
                        sorted_ref.at[pl.ds(pl.multiple_of(dst0 + off, ch), ch)], sem.at[slot]).start()
            return carry

        lax.fori_loop(0, nch[g], issue, 0)
        meta_ref[i * MOE_META + 3 * g] = dst0
        meta_ref[i * MOE_META + 3 * g + 1] = nch[g]
        meta_ref[i * MOE_META + 3 * g + 2] = los[g]
        run_s[g] = run_s[g] + nch[g] * ch
        total = total + nch[g]
    nis_s[slot] = total

    @pl.when(i == n - 1)
    def _():
        for g in range(MOE_GROUPS):
            r = run_s[g]
            full = (r + (MOE_RT - 1)) // MOE_RT * MOE_RT
            nfill = (full - r) // ch
            base = g * cap + r

            def fill(c, carry, base=base):
                off = pl.multiple_of(c * ch, ch)
                _chunk_copy(zbuf, sorted_ref.at[pl.ds(pl.multiple_of(base + off, ch), ch)], sem.at[2]).start()
                return carry

            def fill_wait(c, carry):
                _chunk_copy(zbuf, sorted_ref.at[pl.ds(0, ch)], sem.at[2]).wait()
                return carry

            lax.fori_loop(0, nfill, fill, 0)
            lax.fori_loop(0, nfill, fill_wait, 0)
            tot_ref[g] = full
            tot_ref[MOE_GROUPS + g] = 0
        _wait(0, nis_s[0])
        _wait(1, nis_s[1])


def out_proj_dispatch(a, w, x, mod, nw2, wr, br, cap):
    b, s, d = x.shape
    t = b * s
    tm = MOE_TM
    tpb = s // tm
    nt = t // tm
    grouped = a.ndim == 4
    if grouped:
        a_spec = pl.BlockSpec((1, a.shape[1], tm, a.shape[3]), lambda i: (i // tpb, 0, i % tpb, 0))
    else:
        a = a.reshape(t, a.shape[2])
        a_spec = pl.BlockSpec((tm, a.shape[1]), lambda i: (i, 0))
    ltri = (jnp.arange(tm)[:, None] > jnp.arange(tm)[None, :]).astype(BF16)
    smem = pl.BlockSpec(memory_space=pltpu.SMEM)
    xo, lp, meta, tot, srt = pl.pallas_call(
        functools.partial(_out_proj_dispatch_kernel, grouped=grouped, cap=cap),
        grid=(nt,),
        in_specs=[
            a_spec,
            _resident(w.shape),
            pl.BlockSpec((tm, d), lambda i: (i, 0)),
            pl.BlockSpec((1, 6, d), lambda i: (i // tpb, 0, 0)),
            _resident((1, d)),
            _resident(wr.shape),
            _resident(br.shape),
            _resident(ltri.shape),
        ],
        out_specs=[
            pl.BlockSpec((tm, d), lambda i: (i, 0)),
            pl.BlockSpec((tm, 1), lambda i: (i, 0)),
            smem,
            smem,
            pl.BlockSpec(memory_space=pl.ANY),
        ],
        out_shape=[
            jax.ShapeDtypeStruct((t, d), F32),
            jax.ShapeDtypeStruct((t, 1), jnp.int32),
            jax.ShapeDtypeStruct((nt * MOE_META,), jnp.int32),
            jax.ShapeDtypeStruct((2 * MOE_GROUPS,), jnp.int32),
            jax.ShapeDtypeStruct((MOE_GROUPS * cap, d), BF16),
        ],
        scratch_shapes=[
            pltpu.VMEM((2, MOE_LMAX, d), BF16),
            pltpu.VMEM((MOE_CH, d), BF16),
            pltpu.SemaphoreType.DMA((3,)),
            pltpu.SMEM((MOE_GROUPS,), jnp.int32),
            pltpu.SMEM((2,), jnp.int32),
        ],
        compiler_params=_cparams(("arbitrary",)),
        name="out_proj_dispatch",
    )(a, w, x.reshape(t, d), mod, nw2.reshape(1, d), wr, br, ltri)
    return xo, lp, meta, tot, srt


def _ssd_kernel(z_ref, x_ref, b_ref, c_ref, dt_ref, dtb_ref, alog_ref, dskip_ref, nw_ref, o_ref,
                state, acsg, aspt, wt):
    j = pl.program_id(1)
    L = SSD_L

    @pl.when(j == 0)
    def _():
        state[...] = jnp.zeros_like(state)

    dt = jax.nn.softplus(dt_ref[0] + dtb_ref[...])
    a = dt * (-jnp.exp(alog_ref[...]) * LOG2E)
    row = lax.broadcasted_iota(jnp.int32, (L, L), 0)
    col = lax.broadcasted_iota(jnp.int32, (L, L), 1)
    tril = row >= col
    a_hi = a.astype(BF16)
    r1 = a - a_hi.astype(F32)
    a_mid = r1.astype(BF16)
    a_lo = (r1 - a_mid.astype(F32)).astype(BF16)
    parts = jnp.dot(tril.astype(BF16), jnp.concatenate([a_hi, a_mid, a_lo], axis=1), preferred_element_type=F32)
    a_cs = parts[:, 0:LANES] + parts[:, LANES:2 * LANES] + parts[:, 2 * LANES:]
    a_last = a_cs[L - 1:L, :]
    aspt[...] = (a_cs - jnp.log2(dt)).T
    wt[...] = (dt * jnp.exp2(a_last - a_cs)).T
    for gi in range(SSM_GROUPS):
        shift = (LANES - gi * SSM_HEADS_PER_GROUP) % LANES
        acsg[gi] = pltpu.roll(a_cs, shift, 1) if shift else a_cs

    lane_hi = lax.broadcasted_iota(jnp.int32, (1, LANES), 1) >= SSM_HEAD_DIM

    def group_body(g, carry):
        xsb = x_ref[0, g]
        bmb = b_ref[0, g]
        cmb = c_ref[0, g]
        cb = lax.dot_general(cmb, bmb, (((1,), (1,)), ((), ())), preferred_element_type=F32)
        bmt = bmb.astype(F32).T
        st = state[g]
        yoff = jnp.dot(cmb, st.astype(BF16), preferred_element_type=F32)
        zz = z_ref[0, g].astype(F32)
        dsk = dskip_ref[g]
        acs_g = acsg[g]
        gated = []
        for q in range(2):
            c0 = q * LANES
            xs2b = xsb[:, c0:c0 + LANES]
            albs, mrs, lhss, cds = [], [], [], []
            for jj in range(2):
                h = g * SSM_HEADS_PER_GROUP + 2 * q + jj
                r = 2 * q + jj
                alb = jnp.broadcast_to(acs_g[:, r:r + 1], (L, LANES))
                asr = aspt[pl.ds(h, 1), :]
                wr = wt[pl.ds(h, 1), :]
                al2 = jnp.concatenate([alb] * (L // LANES), axis=1)
                seg = jnp.exp2(jnp.where(tril, al2 - asr, -jnp.inf))
                albs.append(alb)
                mrs.append((cb * seg).astype(BF16))
                lhss.append((bmt * wr).astype(BF16))
                cds.append(jnp.exp2(alb[L - 1:L, :]))
            yd = jnp.dot(jnp.concatenate(mrs, axis=0), xs2b, preferred_element_type=F32)
            sn2 = jnp.dot(jnp.concatenate(lhss, axis=0), xs2b, preferred_element_type=F32)
            yo = yoff[:, c0:c0 + LANES]
            ys = [yd[jj * L:(jj + 1) * L, :] + jnp.exp2(albs[jj]) * yo for jj in range(2)]
            y = jnp.where(lane_hi, ys[1], ys[0]) + xs2b.astype(F32) * dsk[:, c0:c0 + LANES]
            sn = jnp.where(lane_hi, sn2[SSM_D_STATE:, :], sn2[0:SSM_D_STATE, :])
            cd = jnp.where(lane_hi, cds[1], cds[0])
            state[g, :, c0:c0 + LANES] = st[:, c0:c0 + LANES] * cd + sn
            gated.append(y * _silu(zz[:, c0:c0 + LANES]))
        ss = jnp.sum(gated[0] * gated[0], axis=-1, keepdims=True) + jnp.sum(gated[1] * gated[1], axis=-1, keepdims=True)
        rs = lax.rsqrt(ss / SSM_GROUP_WIDTH + EPS)
        nw = nw_ref[g]
        for q in range(2):
            c0 = q * LANES
            o_ref[0, g, :, c0:c0 + LANES] = (gated[q] * rs * nw[:, c0:c0 + LANES]).astype(o_ref.dtype)
        return carry

    lax.fori_loop(0, SSM_GROUPS, group_body, 0)


def ssd(z, xs, bm, cm, dt, dt_bias, a_log, d_skip, norm_w):
    b, ng, s, gw = z.shape
    L = SSD_L
    n = SSM_D_STATE
    pad = LANES - SSM_HEADS
    dtb = jnp.pad(dt_bias, (0, pad)).reshape(1, LANES)
    alog = jnp.pad(a_log, (0, pad)).reshape(1, LANES)
    dsk = jnp.repeat(d_skip, SSM_HEAD_DIM).reshape(ng, 1, gw)
    nw = norm_w.reshape(ng, 1, gw)
    tile4 = lambda w: pl.BlockSpec((1, ng, L, w), lambda i, j: (i, 0, j, 0))
    return pl.pallas_call(
        _ssd_kernel,
        grid=(b, s // L),
        in_specs=[
            tile4(gw), tile4(gw), tile4(n), tile4(n),
            pl.BlockSpec((1, L, LANES), lambda i, j: (i, j, 0)),
            _resident(dtb.shape), _resident(alog.shape), _resident(dsk.shape), _resident(nw.shape),
        ],
        out_specs=tile4(gw),
        out_shape=jax.ShapeDtypeStruct((b, ng, s, gw), BF16),
        scratch_shapes=[
            pltpu.VMEM((ng, n, gw), F32),
            pltpu.VMEM((ng, L, LANES), F32),
            pltpu.VMEM((LANES, L), F32),
            pltpu.VMEM((LANES, L), F32),
        ],
        compiler_params=_cparams(("parallel", "arbitrary")),
        name="ssd",
    )(z, xs, bm, cm, dt, dtb, alog, dsk, nw)


def _route_in_group(logits, g):
    lane, gl, gmax = _group_logits(logits)
    gsum = jnp.sum(jnp.exp(gl - gmax), axis=-1, keepdims=True)
    lg = jnp.sum(jnp.where(lane == MOE_EXPERTS + g, logits, 0.0), axis=-1, keepdims=True)
    g_val = jnp.exp(lg - gmax) / gsum
    in_grp = (lane >= g * MOE_EPG) & (lane < (g + 1) * MOE_EPG)
    el = jnp.where(in_grp, logits, -jnp.inf)
    emax = jnp.max(el, axis=-1, keepdims=True)
    ee = jnp.exp(el - emax)
    p = ee * (1.0 / jnp.sum(ee, axis=-1, keepdims=True))
    p = jnp.where(in_grp, p, -1.0)
    m1 = jnp.max(p, axis=-1, keepdims=True)
    i1 = jnp.min(jnp.where(p == m1, lane, 2 * LANES), axis=-1, keepdims=True)
    p2 = jnp.where(lane == i1, -1.0, p)
    m2 = jnp.max(p2, axis=-1, keepdims=True)
    i2 = jnp.min(jnp.where(p2 == m2, lane, 2 * LANES), axis=-1, keepdims=True)
    tot = m1 + m2
    w1 = g_val * (m1 / tot)
    w2 = g_val * (m2 / tot)
    return lane, jnp.where(lane == i1, w1, 0.0) + jnp.where(lane == i2, w2, 0.0)


def _moe_expert_kernel(tg_ref, tr_ref, nt_ref, rows_ref, wr_ref, br_ref, w1_ref, w3_ref, w2_ref, y_ref,
                       w1s, w3s, w2s):
    i = pl.program_id(0)
    g = tg_ref[i]

    @pl.when((i == 0) | (g != tg_ref[jnp.maximum(i - 1, 0)]))
    def _():
        for q in range(MOE_EPG):
            w1s[:, q * MOE_D_FF:(q + 1) * MOE_D_FF] = w1_ref[q].astype(BF16)
            w3s[:, q * MOE_D_FF:(q + 1) * MOE_D_FF] = w3_ref[q].astype(BF16)
            w2s[q * MOE_D_FF:(q + 1) * MOE_D_FF, :] = w2_ref[q].astype(BF16)

    @pl.when(i < nt_ref[0])
    def _():
        rows = rows_ref[...]
        logits = jnp.dot(rows, wr_ref[...], preferred_element_type=F32) + br_ref[...]
        lane, comb = _route_in_group(logits, g)
        a = jnp.dot(rows, w1s[...], preferred_element_type=F32)
        b = jnp.dot(rows, w3s[...], preferred_element_type=F32)
        hid = _silu(a) * b
        parts = []
        for q in range(MOE_EPG):
            cw = jnp.sum(jnp.where(lane == g * MOE_EPG + q, comb, 0.0), axis=-1, keepdims=True)
            parts.append((hid[:, q * MOE_D_FF:(q + 1) * MOE_D_FF] * cw).astype(BF16))
        hb = jnp.concatenate(parts, axis=1)
        y_ref[...] = jnp.dot(hb, w2s[...], preferred_element_type=F32).astype(y_ref.dtype)


def moe_experts(tg, tr, nt, srt, wr, br, w1, w3, w2, layer, nt_max):
    rows, d = srt.shape
    gw = MOE_EPG * MOE_D_FF
    grid_spec = pltpu.PrefetchScalarGridSpec(
        num_scalar_prefetch=3,
        grid=(nt_max,),
        in_specs=[
            pl.BlockSpec((MOE_RT, d), lambda i, tg, tr, nt: (tr[i], 0)),
            _resident(wr.shape),
            _resident(br.shape),
            pl.BlockSpec((None, MOE_EPG, d, MOE_D_FF), lambda i, tg, tr, nt: (layer, tg[i], 0, 0)),
            pl.BlockSpec((None, MOE_EPG, d, MOE_D_FF), lambda i, tg, tr, nt: (layer, tg[i], 0, 0)),
            pl.BlockSpec((None, MOE_EPG, MOE_D_FF, d), lambda i, tg, tr, nt: (layer, tg[i], 0, 0)),
        ],
        out_specs=pl.BlockSpec((MOE_RT, d), lambda i, tg, tr, nt: (tr[i], 0)),
        scratch_shapes=[pltpu.VMEM((d, gw), BF16), pltpu.VMEM((d, gw), BF16), pltpu.VMEM((gw, d), BF16)],
    )
    return pl.pallas_call(
        _moe_expert_kernel,
        grid_spec=grid_spec,
        out_shape=jax.ShapeDtypeStruct((rows, d), BF16),
        compiler_params=_cparams(("arbitrary",)),
        name="moe_experts",
    )(tg, tr, nt, srt, wr, br, w1, w3, w2)


def _moe_combine_kernel(meta_ref, x_ref, lp_ref, mod_ref, nw_ref, y_ref, o_ref, ybuf, sem, *, final):
    i = pl.program_id(0)
    n = pl.num_programs(0)
    ch = MOE_CH

    def run_copies(tile, sl, start):
        for g in range(MOE_GROUPS):
            dst0 = meta_ref[tile * MOE_META + 3 * g]
            nch = meta_ref[tile * MOE_META + 3 * g + 1]
            lo = meta_ref[tile * MOE_META + 3 * g + 2]

            def body(c, carry, dst0=dst0, lo=lo):
                off = pl.multiple_of(c * ch, ch)
                cp = _chunk_copy(y_ref.at[pl.ds(pl.multiple_of(dst0 + off, ch), ch)],
                                 ybuf.at[sl, pl.ds(pl.multiple_of(lo + off, ch), ch)], sem.at[sl])
                if start:
                    cp.start()
                else:
                    cp.wait()
                return carry

            lax.fori_loop(0, nch, body, 0)

    @pl.when(i == 0)
    def _():
        ybuf[...] = jnp.zeros_like(ybuf)
        run_copies(0, 0, True)

    @pl.when(i + 1 < n)
    def _():
        run_copies(i + 1, (i + 1) % 2, True)

    run_copies(i, i % 2, False)
    lane = lax.broadcasted_iota(jnp.int32, (MOE_TM, MOE_LMAX), 1)
    pmt = jnp.where(lane == lp_ref[...], 1.0, 0.0).astype(BF16)
    ffn = jnp.dot(pmt, ybuf[i % 2], preferred_element_type=F32)
    xn = x_ref[...] + mod_ref[0][5:6] * ffn
    if final:
        ms = jnp.mean(xn * xn, axis=-1, keepdims=True)
        xn = xn * lax.rsqrt(ms + EPS) * nw_ref[...]
    o_ref[...] = xn


def moe_combine(meta, x, lp, mod, y, nw, final, tpb):
    t, d = x.shape
    tm = MOE_TM
    grid_spec = pltpu.PrefetchScalarGridSpec(
        num_scalar_prefetch=1,
        grid=(t // tm,),
        in_specs=[
            pl.BlockSpec((tm, d), lambda i, m: (i, 0)),
            pl.BlockSpec((tm, 1), lambda i, m: (i, 0)),
            pl.BlockSpec((1, 6, d), lambda i, m: (i // tpb, 0, 0)),
            _resident((1, d)),
            pl.BlockSpec(memory_space=pl.ANY),
        ],
        out_specs=pl.BlockSpec((tm, d), lambda i, m: (i, 0)),
        scratch_shapes=[pltpu.VMEM((2, MOE_LMAX, d), BF16), pltpu.SemaphoreType.DMA((2,))],
    )
    return pl.pallas_call(
        functools.partial(_moe_combine_kernel, final=final),
        grid_spec=grid_spec,
        out_shape=jax.ShapeDtypeStruct((t, d), F32),
        compiler_params=_cparams(("arbitrary",)),
        name="moe_combine",
    )(meta, x, lp, mod, nw.reshape(1, d), y)


def _expert_schedule(tot, cap, nt_max):
    nt_g = tot[:MOE_GROUPS] // MOE_RT
    ends = jnp.cumsum(nt_g)
    starts = ends - nt_g
    nt = ends[MOE_GROUPS - 1]
    ic = jnp.minimum(jnp.arange(nt_max, dtype=jnp.int32), nt - 1)
    tg = jnp.sum((ic[:, None] >= ends[None, :]).astype(jnp.int32), axis=1)
    tr = tg * (cap // MOE_RT) + ic - starts[tg]
    return tg.astype(jnp.int32), tr.astype(jnp.int32), nt.reshape(1).astype(jnp.int32)


def _even_chunks(n):
    cw = 512
    return tuple((c, cw, ((0, cw, 0, None, c),)) for c in range(0, n, cw))


def _router_weights(w_group, b_group, w_expert, b_expert):
    d = w_group.shape[0]
    pad = LANES - MOE_EXPERTS - MOE_GROUPS
    wr = jnp.concatenate([w_expert.reshape(d, MOE_EXPERTS), w_group, jnp.zeros((d, pad), F32)], axis=1)
    br = jnp.concatenate([b_expert.reshape(MOE_EXPERTS), b_group, jnp.zeros((pad,), F32)]).reshape(1, LANES)
    return wr.astype(BF16), br


def kernel(x, c, ada_w, ada_b, norm1_w, norm2_w, mix_w_in, pool_w, pool_scale, rel_bias, mix_w_out, ssm_w_in, ssm_conv_w, ssm_conv_b, ssm_dt_bias, ssm_A_log, ssm_D, ssm_norm_w, ssm_w_out, moe_w_group, moe_b_group, moe_w_expert, moe_b_expert, moe_w1, moe_w3, moe_w2, final_norm_w):
    b, s, d = x.shape
    depth = ada_w.shape[0]
    tm = 512
    mod_all = adaln(c, ada_w, ada_b)
    n_tiles = (b * s) // MOE_TM
    cap = -(-(b * s + n_tiles * MOE_CH) // MOE_RT) * MOE_RT
    nt_max = (b * s + n_tiles * MOE_GROUPS * MOE_CH) // MOE_RT + MOE_GROUPS
    for layer in range(depth):
        mod = mod_all[layer].reshape(b, 6, d)
        i = layer // 2
        if layer % 2 == 0:
            col = jnp.arange(mix_w_in.shape[2])
            q_cols = (col >= POOL_WIDTH) & (col < POOL_WIDTH + ATT_WIDTH)
            w_in = (mix_w_in[i] * jnp.where(q_cols, ATT_HEAD_DIM ** -0.5 * LOG2E, 1.0)).astype(BF16)
            (proj,) = norm_proj(x, norm1_w[layer], mod, w_in,
                                [jax.ShapeDtypeStruct((b, s, w_in.shape[1]), BF16)], _even_chunks(w_in.shape[1]), tm)
            a = even_mixer(proj, _attn_bias_tile(rel_bias[i]), pool_w[i].astype(BF16), pool_scale[i])
            mixed, w_out = a, mix_w_out[i].astype(BF16)
        else:
            z, xs, bm, cm, dt = ssm_in_proj(x, norm1_w[layer], mod, ssm_w_in[i], ssm_conv_w[i], ssm_conv_b[i], tm)
            g = ssd(z, xs, bm, cm, dt, ssm_dt_bias[i], ssm_A_log[i], ssm_D[i], ssm_norm_w[i])
            mixed, w_out = g, ssm_w_out[i].astype(BF16)
        wr, br = _router_weights(moe_w_group[layer], moe_b_group[layer], moe_w_expert[layer], moe_b_expert[layer])
        xk, lp, meta, tot, srt = out_proj_dispatch(mixed, w_out, x, mod, norm2_w[layer], wr, br, cap)
        tg, tr, nt = _expert_schedule(tot, cap, nt_max)
        y = moe_experts(tg, tr, nt, srt, wr, br, moe_w1, moe_w3, moe_w2, layer, nt_max)
        x = moe_combine(meta, xk, lp, mod, y, final_norm_w, layer == depth - 1, s // MOE_TM).reshape(b, s, d)
    return x
```

```python
import functools
import math

import jax
import jax.numpy as jnp
from jax import lax
from jax.experimental import pallas as pl
from jax.experimental.pallas import tpu as pltpu

F32 = jnp.float32
BF16 = jnp.bfloat16

EPS = 1e-6
LANES = 128

POOL_WINDOWS = (2, 4, 8, 16)
POOL_GROUP = 128
POOL_WIDTH = POOL_GROUP * len(POOL_WINDOWS)
POOL_HALO = 16
ATT_HEAD_DIM = 64
ATT_HEADS = 8
ATT_WIDTH = ATT_HEADS * ATT_HEAD_DIM
ATT_CHUNK = 64
ATT_PREV_CHUNKS = 8
ATT_TQ = 256
ATT_PREV = ATT_PREV_CHUNKS * ATT_CHUNK
ATT_TK = ATT_PREV + ATT_TQ
REL_CLIP = 256
NEG_BIG = -1e30

SSM_GROUPS = 8
SSM_HEADS = 32
SSM_HEADS_PER_GROUP = 4
SSM_HEAD_DIM = 64
SSM_D_STATE = 128
SSM_D_INNER = SSM_HEADS * SSM_HEAD_DIM
SSM_GROUP_WIDTH = SSM_HEADS_PER_GROUP * SSM_HEAD_DIM
SSM_D_CONV = 4
PROJ_HALO = 16
SSD_L = 256
LOG2E = math.log2(math.e)

MOE_GROUPS = 4
MOE_EPG = 8
MOE_EXPERTS = 32
MOE_D_FF = 128
MOE_TM = 512
MOE_CH = 32
MOE_LMAX = MOE_TM + MOE_GROUPS * MOE_CH
MOE_RT = 512
MOE_META = 3 * MOE_GROUPS

VMEM_LIMIT = 56 * 1024 * 1024


def _cparams(sem):
    return pltpu.CompilerParams(dimension_semantics=sem, vmem_limit_bytes=VMEM_LIMIT)


def _resident(shape):
    nd = len(shape)
    return pl.BlockSpec(shape, lambda *_: (0,) * nd, pipeline_mode=pl.Buffered(1))


def _silu(v):
    return v * jax.nn.sigmoid(v)


def _rms_mod(xv, nw, shift, scale):
    ms = jnp.mean(xv * xv, axis=-1, keepdims=True)
    y = xv * lax.rsqrt(ms + EPS) * nw
    return y * (1.0 + scale) + shift


def _adaln_kernel(c_ref, w_ref, b_ref, o_ref):
    c = c_ref[...]
    ca = _silu(c).astype(BF16)
    o_ref[0] = jnp.dot(ca, w_ref[0].astype(BF16), preferred_element_type=F32) + b_ref[0]


def adaln(c, ada_w, ada_b):
    depth, d, n = ada_w.shape
    b = c.shape[0]
    tn = 1536
    return pl.pallas_call(
        _adaln_kernel,
        grid=(depth, n // tn),
        in_specs=[
            pl.BlockSpec((b, d), lambda l, j: (0, 0)),
            pl.BlockSpec((1, d, tn), lambda l, j: (l, 0, j)),
            pl.BlockSpec((1, 1, tn), lambda l, j: (l, 0, j)),
        ],
        out_specs=pl.BlockSpec((1, b, tn), lambda l, j: (l, 0, j)),
        out_shape=jax.ShapeDtypeStruct((depth, b, n), F32),
        compiler_params=_cparams(("arbitrary", "arbitrary")),
        name="adaln",
    )(c, ada_w, ada_b.reshape(depth, 1, n))


def _norm_proj_kernel(x_ref, nw_ref, mod_ref, w_ref, *o_refs, chunks):
    m = mod_ref[0]
    h = _rms_mod(x_ref[0], nw_ref[...], m[0:1], m[1:2]).astype(BF16)
    for col, width, stores in chunks:
        r = jnp.dot(h, w_ref[:, col:col + width], preferred_element_type=F32)
        for off, w, oi, g, dst in stores:
            o_ref = o_refs[oi]
            v = r[:, off:off + w].astype(o_ref.dtype)
            if g is None:
                o_ref[0, :, dst:dst + w] = v
            else:
                o_ref[0, g, :, dst:dst + w] = v


def norm_proj(x, nw, mod, w, out_shapes, chunks, tm):
    b, s, d = x.shape
    out_specs = []
    for sh in out_shapes:
        if len(sh.shape) == 3:
            out_specs.append(pl.BlockSpec((1, tm, sh.shape[2]), lambda i, j: (i, j, 0)))
        else:
            out_specs.append(pl.BlockSpec((1, sh.shape[1], tm, sh.shape[3]), lambda i, j: (i, 0, j, 0)))
    return pl.pallas_call(
        functools.partial(_norm_proj_kernel, chunks=chunks),
        grid=(b, s // tm),
        in_specs=[
            pl.BlockSpec((1, tm, d), lambda i, j: (i, j, 0)),
            _resident((1, d)),
            pl.BlockSpec((1, 6, d), lambda i, j: (i, 0, 0)),
            _resident(w.shape),
        ],
        out_specs=out_specs,
        out_shape=out_shapes,
        compiler_params=_cparams(("parallel", "parallel")),
        name="norm_proj",
    )(x, nw.reshape(1, d), mod, w)


def _ssm_in_proj_kernel(x_ref, nw_ref, mod_ref, w_ref, cw_ref, cb_ref,
                        z_ref, xs_ref, b_ref, c_ref, dt_ref, halo, *, chunks):
    assert SSM_D_CONV == 4
    j = pl.program_id(1)
    tm = x_ref.shape[1]
    hr = PROJ_HALO
    m = mod_ref[0]
    h = _rms_mod(x_ref[0], nw_ref[...], m[0:1], m[1:2]).astype(BF16)
    outs = (z_ref, xs_ref, b_ref, c_ref, dt_ref)
    for col, width, conv_col, stores in chunks:
        r = jnp.dot(h, w_ref[:, col:col + width], preferred_element_type=F32)
        if conv_col is None:
            res = r
        else:
            prev = jnp.where(j > 0, halo[:, conv_col:conv_col + width], 0.0)
            halo[:, conv_col:conv_col + width] = r[tm - hr:, :]
            re = jnp.concatenate([prev, r], axis=0)
            cw = cw_ref[:, conv_col:conv_col + width]
            u = pltpu.roll(re, 1, 0)
            near = r * cw[3:4] + u[hr:, :] * cw[2:3]
            far = pltpu.roll(re * cw[1:2] + u * cw[0:1], 2, 0)
            res = _silu(cb_ref[:, conv_col:conv_col + width] + near + far[hr:, :])
        for off, w, oi, g in stores:
            o_ref = outs[oi]
            v = res[:, off:off + w].astype(o_ref.dtype)
            if g is None:
                o_ref[0] = v
            else:
                o_ref[0, g] = v


def _ssm_in_proj_chunks():
    gw, n, ng, cw = SSM_GROUP_WIDTH, SSM_D_STATE, SSM_GROUPS, 512
    chunks = []
    col = 0
    for oi, conv0 in ((0, None), (1, 0)):
        for g in range(0, ng, 2):
            conv_col = None if conv0 is None else conv0 + g * gw
            chunks.append((col, cw, conv_col, ((0, gw, oi, g), (gw, gw, oi, g + 1))))
            col += cw
    for oi, conv0 in ((2, SSM_D_INNER), (3, SSM_D_INNER + ng * n)):
        for g in range(0, ng, 4):
            chunks.append((col, cw, conv0 + g * n, tuple((k * n, n, oi, g + k) for k in range(4))))
            col += cw
    chunks.append((col, LANES, None, ((0, LANES, 4, None),)))
    return tuple(chunks), col + LANES


def ssm_in_proj(x, nw, mod, w, conv_w, conv_b, tm):
    b, s, d = x.shape
    ng, gw, n = SSM_GROUPS, SSM_GROUP_WIDTH, SSM_D_STATE
    chunks, cols = _ssm_in_proj_chunks()
    w = jnp.pad(w, ((0, 0), (0, cols - w.shape[1]))).astype(BF16)
    tile4 = lambda wd: pl.BlockSpec((1, ng, tm, wd), lambda i, j: (i, 0, j, 0))
    return pl.pallas_call(
        functools.partial(_ssm_in_proj_kernel, chunks=chunks),
        grid=(b, s // tm),
        in_specs=[
            pl.BlockSpec((1, tm, d), lambda i, j: (i, j, 0)),
            _resident((1, d)),
            pl.BlockSpec((1, 6, d), lambda i, j: (i, 0, 0)),
            _resident(w.shape),
            _resident(conv_w.shape),
            _resident((1, conv_b.shape[0])),
        ],
        out_specs=[tile4(gw), tile4(gw), tile4(n), tile4(n), pl.BlockSpec((1, tm, LANES), lambda i, j: (i, j, 0))],
        out_shape=[
            jax.ShapeDtypeStruct((b, ng, s, gw), BF16),
            jax.ShapeDtypeStruct((b, ng, s, gw), BF16),
            jax.ShapeDtypeStruct((b, ng, s, n), BF16),
            jax.ShapeDtypeStruct((b, ng, s, n), BF16),
            jax.ShapeDtypeStruct((b, s, LANES), F32),
        ],
        scratch_shapes=[pltpu.VMEM((PROJ_HALO, conv_w.shape[1]), F32)],
        compiler_params=_cparams(("parallel", "arbitrary")),
        name="ssm_in_proj",
    )(x, nw.reshape(1, d), mod, w, conv_w, conv_b.reshape(1, -1))


def _even_mixer_kernel(uq_ref, kv2_ref, kv1_ref, kv0_ref, halo_ref, bias_ref, pw_ref, ps_ref, o_ref, ubuf):
    j = pl.program_id(1)
    tq = ATT_TQ

    u = uq_ref[0, :, 0:POOL_WIDTH].astype(F32)
    halo = halo_ref[0].astype(F32)
    ubuf[0:POOL_HALO, :] = jnp.where(j > 0, halo, 0.0)
    ubuf[POOL_HALO:POOL_HALO + tq, :] = u
    pos1 = j * tq + 1 + lax.broadcasted_iota(jnp.int32, (tq, 1), 0)
    for g, win in enumerate(POOL_WINDOWS):
        lo = g * POOL_GROUP
        assert win & (win - 1) == 0 and win <= POOL_HALO
        acc = ubuf[:, lo:lo + POOL_GROUP]
        k = 1
        while k < win:
            acc = acc + pltpu.roll(acc, k, 0)
            k *= 2
        acc = acc[POOL_HALO:, :]
        cnt = jnp.minimum(pos1, win).astype(F32)
        pooled = acc * (1.0 / cnt) - u[:, lo:lo + POOL_GROUP]
        y = jnp.dot(pooled.astype(BF16), pw_ref[g], preferred_element_type=F32)
        o_ref[0, :, lo:lo + POOL_GROUP] = (y * ps_ref[:, lo:lo + POOL_GROUP]).astype(o_ref.dtype)

    kpos = j * tq - ATT_PREV + lax.broadcasted_iota(jnp.int32, (1, ATT_TK), 1)
    kvalid = kpos >= 0
    lane_hi = lax.broadcasted_iota(jnp.int32, (1, LANES), 1) >= ATT_HEAD_DIM
    for hp in range(ATT_HEADS // 2):
        c0 = hp * LANES
        q2 = uq_ref[0, :, POOL_WIDTH + c0:POOL_WIDTH + c0 + LANES]
        k2 = jnp.concatenate([r[0, :, c0:c0 + LANES] for r in (kv2_ref, kv1_ref, kv0_ref)], axis=0)
        v2 = jnp.concatenate(
            [r[0, :, ATT_WIDTH + c0:ATT_WIDTH + c0 + LANES] for r in (kv2_ref, kv1_ref, kv0_ref)], axis=0)
        zero = jnp.zeros_like(q2)
        qm = jnp.concatenate([jnp.where(lane_hi, zero, q2), jnp.where(lane_hi, q2, zero)], axis=0)
        sc = lax.dot_general(qm, k2, (((1,), (1,)), ((), ())), preferred_element_type=F32)
        bias2 = bias_ref[2 * hp:2 * hp + 2].reshape(2 * tq, ATT_TK)
        sc = jnp.where(kvalid, sc + bias2, NEG_BIG)
        mx = jnp.max(sc, axis=-1, keepdims=True)
        p = jnp.exp2(sc - mx)
        den = jnp.sum(p, axis=-1, keepdims=True)
        o = jnp.dot(p.astype(BF16), v2, preferred_element_type=F32) * (1.0 / den)
        o_pair = jnp.where(lane_hi, o[tq:, :], o[0:tq, :])
        o_ref[0, :, POOL_WIDTH + c0:POOL_WIDTH + c0 + LANES] = o_pair.astype(o_ref.dtype)


def _attn_bias_tile(rel_bias):
    nh = rel_bias.shape[0]
    i = jnp.arange(ATT_TQ)[:, None]
    s = jnp.arange(ATT_TK)[None, :]
    n_const = ATT_PREV + ATT_TQ - REL_CLIP
    n_tail = ATT_TK + ATT_TQ - 1 - n_const
    v = jnp.concatenate([jnp.broadcast_to(rel_bias[:, 2 * REL_CLIP:], (nh, n_const)),
                         rel_bias[:, 2 * REL_CLIP - n_tail:2 * REL_CLIP][:, ::-1]], axis=1).astype(F32)
    nv = n_const + n_tail
    stream = jnp.broadcast_to(jnp.pad(v, ((0, 0), (0, 1)))[:, None, :], (nh, ATT_TQ, nv + 1)).reshape(nh, -1)
    skew = stream[:, :ATT_TQ * nv].reshape(nh, ATT_TQ, nv)
    bias = skew[:, :, ATT_TQ - 1:ATT_TQ - 1 + ATT_TK] * LOG2E
    band = s - (i // ATT_CHUNK) * ATT_CHUNK
    ok = (band >= 0) & (band < (ATT_PREV_CHUNKS + 1) * ATT_CHUNK)
    return jnp.where(ok[None], bias, NEG_BIG)


def even_mixer(proj, bias_tile, pool_w, pool_scale):
    b, s, n = proj.shape
    tq = ATT_TQ
    half = n // 2
    halo_blocks = tq // POOL_HALO
    return pl.pallas_call(
        _even_mixer_kernel,
        grid=(b, s // tq),
        in_specs=[
            pl.BlockSpec((1, tq, half), lambda i, j: (i, j, 0)),
            pl.BlockSpec((1, tq, half), lambda i, j: (i, jnp.maximum(j - 2, 0), 1)),
            pl.BlockSpec((1, tq, half), lambda i, j: (i, jnp.maximum(j - 1, 0), 1)),
            pl.BlockSpec((1, tq, half), lambda i, j: (i, j, 1)),
            pl.BlockSpec((1, POOL_HALO, POOL_WIDTH), lambda i, j: (i, jnp.maximum(j * halo_blocks - 1, 0), 0)),
            _resident(bias_tile.shape),
            _resident(pool_w.shape),
            _resident((1, POOL_WIDTH)),
        ],
        out_specs=pl.BlockSpec((1, tq, POOL_WIDTH + ATT_WIDTH), lambda i, j: (i, j, 0)),
        out_shape=jax.ShapeDtypeStruct((b, s, POOL_WIDTH + ATT_WIDTH), BF16),
        scratch_shapes=[pltpu.VMEM((POOL_HALO + tq, POOL_WIDTH), F32)],
        compiler_params=_cparams(("parallel", "parallel")),
        name="even_mixer",
    )(proj, proj, proj, proj, proj, bias_tile, pool_w, pool_scale.reshape(1, POOL_WIDTH))


def _group_logits(logits):
    lane = lax.broadcasted_iota(jnp.int32, logits.shape, 1)
    is_g = (lane >= MOE_EXPERTS) & (lane < MOE_EXPERTS + MOE_GROUPS)
    gl = jnp.where(is_g, logits, -jnp.inf)
    return lane, gl, jnp.max(gl, axis=-1, keepdims=True)


def _chunk_copy(src, dst, sem):
    return pltpu.make_async_copy(src, dst, sem)


def _out_proj_dispatch_kernel(a_ref, w_ref, x_ref, mod_ref, nw_ref, wr_ref, br_ref, ltri_ref,
                              xo_ref, lp_ref, meta_ref, tot_ref, sorted_ref,
                              sbuf, zbuf, sem, run_s, nis_s, *, grouped, cap):
    i = pl.program_id(0)
    n = pl.num_programs(0)
    slot = i % 2
    tm, ch = MOE_TM, MOE_CH

    @pl.when(i == 0)
    def _():
        for g in range(MOE_GROUPS):
            run_s[g] = 0
        nis_s[0] = 0
        nis_s[1] = 0
        zbuf[...] = jnp.zeros_like(zbuf)

    m = mod_ref[0]
    if grouped:
        a_cat = jnp.concatenate([a_ref[0, g] for g in range(a_ref.shape[1])], axis=1)
        mix = jnp.dot(a_cat, w_ref[...], preferred_element_type=F32)
    else:
        mix = jnp.dot(a_ref[...], w_ref[...], preferred_element_type=F32)
    xn = x_ref[...] + m[2:3] * mix
    xo_ref[...] = xn
    h2 = _rms_mod(xn, nw_ref[...], m[3:4], m[4:5]).astype(BF16)

    logits = jnp.dot(h2, wr_ref[...], preferred_element_type=F32) + br_ref[...]
    lane, gl, gmax = _group_logits(logits)
    g_lane = jnp.min(jnp.where(gl == gmax, lane, 2 * LANES), axis=-1, keepdims=True)
    onehot = (lane == g_lane).astype(F32)
    rank = jnp.dot(ltri_ref[...], onehot.astype(BF16), preferred_element_type=F32)
    cnt_row = jnp.sum(onehot, axis=0, keepdims=True)
    lane1 = lax.broadcasted_iota(jnp.int32, (1, LANES), 1)
    nch, los = [], []
    lo = jnp.int32(0)
    lo_row = jnp.zeros((1, LANES), F32)
    for g in range(MOE_GROUPS):
        cnt = jnp.sum(jnp.where(lane1 == MOE_EXPERTS + g, cnt_row, 0.0)).astype(jnp.int32)
        nch.append((cnt + (ch - 1)) // ch)
        los.append(lo)
        lo_row = jnp.where(lane1 == MOE_EXPERTS + g, lo.astype(F32), lo_row)
        lo = lo + nch[g] * ch
    lp = jnp.sum(onehot * (rank + lo_row), axis=-1, keepdims=True)
    lp_ref[...] = lp.astype(jnp.int32)
    lpt = jnp.broadcast_to(lp, (tm, LANES)).T[0:1, :]
    rowi = lax.broadcasted_iota(jnp.int32, (MOE_LMAX, tm), 0).astype(F32)
    pm = jnp.where(rowi == lpt, 1.0, 0.0).astype(BF16)

    def _wait(sl, count):
        def body(c, carry):
            _chunk_copy(sbuf.at[sl, pl.ds(0, ch)], sorted_ref.at[pl.ds(0, ch)], sem.at[sl]).wait()
            return carry
        lax.fori_loop(0, count, body, 0)

    _wait(slot, nis_s[slot])
    sbuf[slot] = jnp.dot(pm, h2, preferred_element_type=F32).astype(BF16)
    total = jnp.int32(0)
    for g in range(MOE_GROUPS):
        dst0 = g * cap + run_s[g]

        def issue(c, carry, g=g, dst0=dst0):
            off = pl.multiple_of(c * ch, ch)
            _chunk_copy(sbuf.at[slot, pl.ds(pl.multiple_of(los[g] + off, ch), ch)],
                        sorted_ref.at[pl.ds(pl.multiple_of(dst0 + off, ch), ch)], sem.at[slot]).start()
            return carry

        lax.fori_loop(0, nch[g], issue, 0)
        meta_ref[i * MOE_META + 3 * g] = dst0
        meta_ref[i * MOE_META + 3 * g + 1] = nch[g]
        meta_ref[i * MOE_META + 3 * g + 2] = los[g]
        run_s[g] = run_s[g] + nch[g] * ch
        total = total + nch[g]
    nis_s[slot] = total

    @pl.when(i == n - 1)
    def _():
        for g in range(MOE_GROUPS):
            r = run_s[g]
            full = (r + (MOE_RT - 1)) // MOE_RT * MOE_RT
            nfill = (full - r) // ch
            base = g * cap + r

            def fill(c, carry, base=base):
                off = pl.multiple_of(c * ch, ch)
                _chunk_copy(zbuf, sorted_ref.at[pl.ds(pl.multiple_of(base + off, ch), ch)], sem.at[2]).start()
                return carry

            def fill_wait(c, carry):
                _chunk_copy(zbuf, sorted_ref.at[pl.ds(0, ch)], sem.at[2]).wait()
                return carry

            lax.fori_loop(0, nfill, fill, 0)
            lax.fori_loop(0, nfill, fill_wait, 0)
            tot_ref[g] = full
            tot_ref[MOE_GROUPS + g] = 0
        _wait(0, nis_s[0])
        _wait(1, nis_s[1])


def out_proj_dispatch(a, w, x, mod, nw2, wr, br, cap):
    b, s, d = x.shape
    t = b * s
    tm = MOE_TM
    tpb = s // tm
    nt = t // tm
    grouped = a.ndim == 4
    if grouped:
        a_spec = pl.BlockSpec((1, a.shape[1], tm, a.shape[3]), lambda i: (i // tpb, 0, i % tpb, 0))
    else:
        a = a.reshape(t, a.shape[2])
        a_spec = pl.BlockSpec((tm, a.shape[1]), lambda i: (i, 0))
    ltri = (jnp.arange(tm)[:, None] > jnp.arange(tm)[None, :]).astype(BF16)
    smem = pl.BlockSpec(memory_space=pltpu.SMEM)
    xo, lp, meta, tot, srt = pl.pallas_call(
        functools.partial(_out_proj_dispatch_kernel, grouped=grouped, cap=cap),
        grid=(nt,),
        in_specs=[
            a_spec,
            _resident(w.shape),
            pl.BlockSpec((tm, d), lambda i: (i, 0)),
            pl.BlockSpec((1, 6, d), lambda i: (i // tpb, 0, 0)),
            _resident((1, d)),
            _resident(wr.shape),
            _resident(br.shape),
            _resident(ltri.shape),
        ],
        out_specs=[
            pl.BlockSpec((tm, d), lambda i: (i, 0)),
            pl.BlockSpec((tm, 1), lambda i: (i, 0)),
            smem,
            smem,
            pl.BlockSpec(memory_space=pl.ANY),
        ],
        out_shape=[
            jax.ShapeDtypeStruct((t, d), F32),
            jax.ShapeDtypeStruct((t, 1), jnp.int32),
            jax.ShapeDtypeStruct((nt * MOE_META,), jnp.int32),
            jax.ShapeDtypeStruct((2 * MOE_GROUPS,), jnp.int32),
            jax.ShapeDtypeStruct((MOE_GROUPS * cap, d), BF16),
        ],
        scratch_shapes=[
            pltpu.VMEM((2, MOE_LMAX, d), BF16),
            pltpu.VMEM((MOE_CH, d), BF16),
            pltpu.SemaphoreType.DMA((3,)),
            pltpu.SMEM((MOE_GROUPS,), jnp.int32),
            pltpu.SMEM((2,), jnp.int32),
        ],
        compiler_params=_cparams(("arbitrary",)),
        name="out_proj_dispatch",
    )(a, w, x.reshape(t, d), mod, nw2.reshape(1, d), wr, br, ltri)
    return xo, lp, meta, tot, srt


def _ssd_kernel(z_ref, x_ref, b_ref, c_ref, dt_ref, dtb_ref, alog_ref, dskip_ref, nw_ref, o_ref,
                state, acsg, aspt, wt):
    j = pl.program_id(1)
    L = SSD_L

    @pl.when(j == 0)
    def _():
        state[...] = jnp.zeros_like(state)

    dt = jax.nn.softplus(dt_ref[0] + dtb_ref[...])
    a = dt * (-jnp.exp(alog_ref[...]) * LOG2E)
    row = lax.broadcasted_iota(jnp.int32, (L, L), 0)
    col = lax.broadcasted_iota(jnp.int32, (L, L), 1)
    tril = row >= col
    a_hi = a.astype(BF16)
    r1 = a - a_hi.astype(F32)
    a_mid = r1.astype(BF16)
    a_lo = (r1 - a_mid.astype(F32)).astype(BF16)
    parts = jnp.dot(tril.astype(BF16), jnp.concatenate([a_hi, a_mid, a_lo], axis=1), preferred_element_type=F32)
    a_cs = parts[:, 0:LANES] + parts[:, LANES:2 * LANES] + parts[:, 2 * LANES:]
    a_last = a_cs[L - 1:L, :]
    aspt[...] = (a_cs - jnp.log2(dt)).T
    wt[...] = (dt * jnp.exp2(a_last - a_cs)).T
    for gi in range(SSM_GROUPS):
        shift = (LANES - gi * SSM_HEADS_PER_GROUP) % LANES
        acsg[gi] = pltpu.roll(a_cs, shift, 1) if shift else a_cs

    lane_hi = lax.broadcasted_iota(jnp.int32, (1, LANES), 1) >= SSM_HEAD_DIM

    def group_body(g, carry):
        xsb = x_ref[0, g]
        bmb = b_ref[0, g]
        cmb = c_ref[0, g]
        cb = lax.dot_general(cmb, bmb, (((1,), (1,)), ((), ())), preferred_element_type=F32)
        bmt = bmb.astype(F32).T
        st = state[g]
        yoff = jnp.dot(cmb, st.astype(BF16), preferred_element_type=F32)
        zz = z_ref[0, g].astype(F32)
        dsk = dskip_ref[g]
        acs_g = acsg[g]
        gated = []
        for q in range(2):
            c0 = q * LANES
            xs2b = xsb[:, c0:c0 + LANES]
            albs, mrs, lhss, cds = [], [], [], []
            for jj in range(2):
                h = g * SSM_HEADS_PER_GROUP + 2 * q + jj
                r = 2 * q + jj
                alb = jnp.broadcast_to(acs_g[:, r:r + 1], (L, LANES))
                asr = aspt[pl.ds(h, 1), :]
                wr = wt[pl.ds(h, 1), :]
                al2 = jnp.concatenate([alb] * (L // LANES), axis=1)
                seg = jnp.exp2(jnp.where(tril, al2 - asr, -jnp.inf))
                albs.append(alb)
                mrs.append((cb * seg).astype(BF16))
                lhss.append((bmt * wr).astype(BF16))
                cds.append(jnp.exp2(alb[L - 1:L, :]))
            yd = jnp.dot(jnp.concatenate(mrs, axis=0), xs2b, preferred_element_type=F32)
            sn2 = jnp.dot(jnp.concatenate(lhss, axis=0), xs2b, preferred_element_type=F32)
            yo = yoff[:, c0:c0 + LANES]
            ys = [yd[jj * L:(jj + 1) * L, :] + jnp.exp2(albs[jj]) * yo for jj in range(2)]
            y = jnp.where(lane_hi, ys[1], ys[0]) + xs2b.astype(F32) * dsk[:, c0:c0 + LANES]
            sn = jnp.where(lane_hi, sn2[SSM_D_STATE:, :], sn2[0:SSM_D_STATE, :])
            cd = jnp.where(lane_hi, cds[1], cds[0])
            state[g, :, c0:c0 + LANES] = st[:, c0:c0 + LANES] * cd + sn
            gated.append(y * _silu(zz[:, c0:c0 + LANES]))
        ss = jnp.sum(gated[0] * gated[0], axis=-1, keepdims=True) + jnp.sum(gated[1] * gated[1], axis=-1, keepdims=True)
        rs = lax.rsqrt(ss / SSM_GROUP_WIDTH + EPS)
        nw = nw_ref[g]
        for q in range(2):
            c0 = q * LANES
            o_ref[0, g, :, c0:c0 + LANES] = (gated[q] * rs * nw[:, c0:c0 + LANES]).astype(o_ref.dtype)
        return carry

    lax.fori_loop(0, SSM_GROUPS, group_body, 0)


def ssd(z, xs, bm, cm, dt, dt_bias, a_log, d_skip, norm_w):
    b, ng, s, gw = z.shape
    L = SSD_L
    n = SSM_D_STATE
    pad = LANES - SSM_HEADS
    dtb = jnp.pad(dt_bias, (0, pad)).reshape(1, LANES)
    alog = jnp.pad(a_log, (0, pad)).reshape(1, LANES)
    dsk = jnp.repeat(d_skip, SSM_HEAD_DIM).reshape(ng, 1, gw)
    nw = norm_w.reshape(ng, 1, gw)
    tile4 = lambda w: pl.BlockSpec((1, ng, L, w), lambda i, j: (i, 0, j, 0))
    return pl.pallas_call(
        _ssd_kernel,
        grid=(b, s // L),
        in_specs=[
            tile4(gw), tile4(gw), tile4(n), tile4(n),
            pl.BlockSpec((1, L, LANES), lambda i, j: (i, j, 0)),
            _resident(dtb.shape), _resident(alog.shape), _resident(dsk.shape), _resident(nw.shape),
        ],
        out_specs=tile4(gw),
        out_shape=jax.ShapeDtypeStruct((b, ng, s, gw), BF16),
        scratch_shapes=[
            pltpu.VMEM((ng, n, gw), F32),
            pltpu.VMEM((ng, L, LANES), F32),
            pltpu.VMEM((LANES, L), F32),
            pltpu.VMEM((LANES, L), F32),
        ],
        compiler_params=_cparams(("parallel", "arbitrary")),
        name="ssd",
    )(z, xs, bm, cm, dt, dtb, alog, dsk, nw)


def _route_in_group(logits, g):
    lane, gl, gmax = _group_logits(logits)
    gsum = jnp.sum(jnp.exp(gl - gmax), axis=-1, keepdims=True)
    lg = jnp.sum(jnp.where(lane == MOE_EXPERTS + g, logits, 0.0), axis=-1, keepdims=True)
    g_val = jnp.exp(lg - gmax) / gsum
    in_grp = (lane >= g * MOE_EPG) & (lane < (g + 1) * MOE_EPG)
    el = jnp.where(in_grp, logits, -jnp.inf)
    emax = jnp.max(el, axis=-1, keepdims=True)
    ee = jnp.exp(el - emax)
    p = ee * (1.0 / jnp.sum(ee, axis=-1, keepdims=True))
    p = jnp.where(in_grp, p, -1.0)
    m1 = jnp.max(p, axis=-1, keepdims=True)
    i1 = jnp.min(jnp.where(p == m1, lane, 2 * LANES), axis=-1, keepdims=True)
    p2 = jnp.where(lane == i1, -1.0, p)
    m2 = jnp.max(p2, axis=-1, keepdims=True)
    i2 = jnp.min(jnp.where(p2 == m2, lane, 2 * LANES), axis=-1, keepdims=True)
    tot = m1 + m2
    w1 = g_val * (m1 / tot)
    w2 = g_val * (m2 / tot)
    return lane, jnp.where(lane == i1, w1, 0.0) + jnp.where(lane == i2, w2, 0.0)


def _moe_expert_kernel(tg_ref, tr_ref, nt_ref, rows_ref, wr_ref, br_ref, w1_ref, w3_ref, w2_ref, y_ref,
                       w1s, w3s, w2s):
    i = pl.program_id(0)
    g = tg_ref[i]

    @pl.when((i == 0) | (g != tg_ref[jnp.maximum(i - 1, 0)]))
    def _():
        for q in range(MOE_EPG):
            w1s[:, q * MOE_D_FF:(q + 1) * MOE_D_FF] = w1_ref[q].astype(BF16)
            w3s[:, q * MOE_D_FF:(q + 1) * MOE_D_FF] = w3_ref[q].astype(BF16)
            w2s[q * MOE_D_FF:(q + 1) * MOE_D_FF, :] = w2_ref[q].astype(BF16)

    @pl.when(i < nt_ref[0])
    def _():
        rows = rows_ref[...]
        logits = jnp.dot(rows, wr_ref[...], preferred_element_type=F32) + br_ref[...]
        lane, comb = _route_in_group(logits, g)
        a = jnp.dot(rows, w1s[...], preferred_element_type=F32)
        b = jnp.dot(rows, w3s[...], preferred_element_type=F32)
        hid = _silu(a) * b
        parts = []
        for q in range(MOE_EPG):
            cw = jnp.sum(jnp.where(lane == g * MOE_EPG + q, comb, 0.0), axis=-1, keepdims=True)
            parts.append((hid[:, q * MOE_D_FF:(q + 1) * MOE_D_FF] * cw).astype(BF16))
        hb = jnp.concatenate(parts, axis=1)
        y_ref[...] = jnp.dot(hb, w2s[...], preferred_element_type=F32).astype(y_ref.dtype)


def moe_experts(tg, tr, nt, srt, wr, br, w1, w3, w2, layer, nt_max):
    rows, d = srt.shape
    gw = MOE_EPG * MOE_D_FF
    grid_spec = pltpu.PrefetchScalarGridSpec(
        num_scalar_prefetch=3,
        grid=(nt_max,),
        in_specs=[
            pl.BlockSpec((MOE_RT, d), lambda i, tg, tr, nt: (tr[i], 0)),
            _resident(wr.shape),
            _resident(br.shape),
            pl.BlockSpec((None, MOE_EPG, d, MOE_D_FF), lambda i, tg, tr, nt: (layer, tg[i], 0, 0)),
            pl.BlockSpec((None, MOE_EPG, d, MOE_D_FF), lambda i, tg, tr, nt: (layer, tg[i], 0, 0)),
            pl.BlockSpec((None, MOE_EPG, MOE_D_FF, d), lambda i, tg, tr, nt: (layer, tg[i], 0, 0)),
        ],
        out_specs=pl.BlockSpec((MOE_RT, d), lambda i, tg, tr, nt: (tr[i], 0)),
        scratch_shapes=[pltpu.VMEM((d, gw), BF16), pltpu.VMEM((d, gw), BF16), pltpu.VMEM((gw, d), BF16)],
    )
    return pl.pallas_call(
        _moe_expert_kernel,
        grid_spec=grid_spec,
        out_shape=jax.ShapeDtypeStruct((rows, d), BF16),
        compiler_params=_cparams(("arbitrary",)),
        name="moe_experts",
    )(tg, tr, nt, srt, wr, br, w1, w3, w2)


def _moe_combine_kernel(meta_ref, x_ref, lp_ref, mod_ref, nw_ref, y_ref, o_ref, ybuf, sem, *, final):
    i = pl.program_id(0)
    n = pl.num_programs(0)
    ch = MOE_CH

    def run_copies(tile, sl, start):
        for g in range(MOE_GROUPS):
            dst0 = meta_ref[tile * MOE_META + 3 * g]
            nch = meta_ref[tile * MOE_META + 3 * g + 1]
            lo = meta_ref[tile * MOE_META + 3 * g + 2]

            def body(c, carry, dst0=dst0, lo=lo):
                off = pl.multiple_of(c * ch, ch)
                cp = _chunk_copy(y_ref.at[pl.ds(pl.multiple_of(dst0 + off, ch), ch)],
                                 ybuf.at[sl, pl.ds(pl.multiple_of(lo + off, ch), ch)], sem.at[sl])
                if start:
                    cp.start()
                else:
                    cp.wait()
                return carry

            lax.fori_loop(0, nch, body, 0)

    @pl.when(i == 0)
    def _():
        ybuf[...] = jnp.zeros_like(ybuf)
        run_copies(0, 0, True)

    @pl.when(i + 1 < n)
    def _():
        run_copies(i + 1, (i + 1) % 2, True)

    run_copies(i, i % 2, False)
    lane = lax.broadcasted_iota(jnp.int32, (MOE_TM, MOE_LMAX), 1)
    pmt = jnp.where(lane == lp_ref[...], 1.0, 0.0).astype(BF16)
    ffn = jnp.dot(pmt, ybuf[i % 2], preferred_element_type=F32)
    xn = x_ref[...] + mod_ref[0][5:6] * ffn
    if final:
        ms = jnp.mean(xn * xn, axis=-1, keepdims=True)
        xn = xn * lax.rsqrt(ms + EPS) * nw_ref[...]
    o_ref[...] = xn


def moe_combine(meta, x, lp, mod, y, nw, final, tpb):
    t, d = x.shape
    tm = MOE_TM
    grid_spec = pltpu.PrefetchScalarGridSpec(
        num_scalar_prefetch=1,
        grid=(t // tm,),
        in_specs=[
            pl.BlockSpec((tm, d), lambda i, m: (i, 0)),
            pl.BlockSpec((tm, 1), lambda i, m: (i, 0)),
            pl.BlockSpec((1, 6, d), lambda i, m: (i // tpb, 0, 0)),
            _resident((1, d)),
            pl.BlockSpec(memory_space=pl.ANY),
        ],
        out_specs=pl.BlockSpec((tm, d), lambda i, m: (i, 0)),
        scratch_shapes=[pltpu.VMEM((2, MOE_LMAX, d), BF16), pltpu.SemaphoreType.DMA((2,))],
    )
    return pl.pallas_call(
        functools.partial(_moe_combine_kernel, final=final),
        grid_spec=grid_spec,
        out_shape=jax.ShapeDtypeStruct((t, d), F32),
        compiler_params=_cparams(("arbitrary",)),
        name="moe_combine",
    )(meta, x, lp, mod, nw.reshape(1, d), y)


def _expert_schedule(tot, cap, nt_max):
    nt_g = tot[:MOE_GROUPS] // MOE_RT
    ends = jnp.cumsum(nt_g)
    starts = ends - nt_g
    nt = ends[MOE_GROUPS - 1]
    ic = jnp.minimum(jnp.arange(nt_max, dtype=jnp.int32), nt - 1)
    tg = jnp.sum((ic[:, None] >= ends[None, :]).astype(jnp.int32), axis=1)
    tr = tg * (cap // MOE_RT) + ic - starts[tg]
    return tg.astype(jnp.int32), tr.astype(jnp.int32), nt.reshape(1).astype(jnp.int32)


def _even_chunks(n):
    cw = 512
    return tuple((c, cw, ((0, cw, 0, None, c),)) for c in range(0, n, cw))


def _router_weights(w_group, b_group, w_expert, b_expert):
    d = w_group.shape[0]
    pad = LANES - MOE_EXPERTS - MOE_GROUPS
    wr = jnp.concatenate([w_expert.reshape(d, MOE_EXPERTS), w_group, jnp.zeros((d, pad), F32)], axis=1)
    br = jnp.concatenate([b_expert.reshape(MOE_EXPERTS), b_group, jnp.zeros((pad,), F32)]).reshape(1, LANES)
    return wr.astype(BF16), br


def kernel(x, c, ada_w, ada_b, norm1_w, norm2_w, mix_w_in, pool_w, pool_scale, rel_bias, mix_w_out, ssm_w_in, ssm_conv_w, ssm_conv_b, ssm_dt_bias, ssm_A_log, ssm_D, ssm_norm_w, ssm_w_out, moe_w_group, moe_b_group, moe_w_expert, moe_b_expert, moe_w1, moe_w3, moe_w2, final_norm_w):
    b, s, d = x.shape
    depth = ada_w.shape[0]
    tm = 512
    mod_all = adaln(c, ada_w, ada_b)
    n_tiles = (b * s) // MOE_TM
    cap = -(-(b * s + n_tiles * MOE_CH) // MOE_RT) * MOE_RT
    nt_max = (b * s + n_tiles * MOE_GROUPS * MOE_CH) // MOE_RT + MOE_GROUPS
    for layer in range(depth):
        mod = mod_all[layer].reshape(b, 6, d)
        i = layer // 2
        if layer % 2 == 0:
            col = jnp.arange(mix_w_in.shape[2])
            q_cols = (col >= POOL_WIDTH) & (col < POOL_WIDTH + ATT_WIDTH)
            w_in = (mix_w_in[i] * jnp.where(q_cols, ATT_HEAD_DIM ** -0.5 * LOG2E, 1.0)).astype(BF16)
            (proj,) = norm_proj(x, norm1_w[layer], mod, w_in,
                                [jax.ShapeDtypeStruct((b, s, w_in.shape[1]), BF16)], _even_chunks(w_in.shape[1]), tm)
            a = even_mixer(proj, _attn_bias_tile(rel_bias[i]), pool_w[i].astype(BF16), pool_scale[i])
            mixed, w_out = a, mix_w_out[i].astype(BF16)
        else:
            z, xs, bm, cm, dt = ssm_in_proj(x, norm1_w[layer], mod, ssm_w_in[i], ssm_conv_w[i], ssm_conv_b[i], tm)
            g = ssd(z, xs, bm, cm, dt, ssm_dt_bias[i], ssm_A_log[i], ssm_D[i], ssm_norm_w[i])
            mixed, w_out = g, ssm_w_out[i].astype(BF16)
        wr, br = _router_weights(moe_w_group[layer], moe_b_group[layer], moe_w_expert[layer], moe_b_expert[layer])
        xk, lp, meta, tot, srt = out_proj_dispatch(mixed, w_out, x, mod, norm2_w[layer], wr, br, cap)
        tg, tr, nt = _expert_schedule(tot, cap, nt_max)
        y = moe_experts(tg, tr, nt, srt, wr, br, moe_w1, moe_w3, moe_w2, layer, nt_max)
        x = moe_combine(meta, xk, lp, mod, y, final_norm_w, layer == depth - 1, s // MOE_TM).reshape(b, s, d)
    return x
```
